```python
import math
import jax
import jax.numpy as jnp
from jax import lax
import numpy as np

D_MODEL = 1024
BATCH = 8
SEQ = 2048
DEPTH = 4

CTX_LEN = 256
GRID_W = 64
N_MIXERS = 4
NORM_EPS = 1e-6
ATTN_QBLOCK = 128
ROPE_BASE = 10000.0

NA_HEADS = 16
NA_HEAD_DIM = D_MODEL // NA_HEADS
NA_WIN_H = 8
NA_WIN_W = 16
NA_QBLOCK_W = 16
NA_KBLOCK_W = 2 * NA_WIN_W

MLA_HEADS = 16
MLA_Q_RANK = 3 * D_MODEL // 8
MLA_KV_RANK = D_MODEL // 4
MLA_NOPE_DIM = D_MODEL // MLA_HEADS
MLA_ROPE_DIM = MLA_NOPE_DIM // 2
MLA_V_DIM = D_MODEL // MLA_HEADS

HY_SHORT_CONV = 3
HY_EMB_DIM = 33
HY_FILTER_HIDDEN = 64
HY_DECAY_TARGET = 1e-2
HY_FAST_DECAY = 0.3
HY_SLOW_DECAY = 1.5
HY_MOD_SHIFT = 0.05

ML_HEADS = 8
ML_V_DIM = D_MODEL // ML_HEADS
ML_QK_DIM = ML_V_DIM // 2
ML_CHUNK = 64
ML_SHORT_CONV = 3
ML_FORGET_BIAS = 3.0

N_EXPERTS = 16
EXPERT_FF = 2816
EC_CAPACITY_FACTOR = 2

kernel_name = 'hybrid_dit_na_mla_hyena_mlstm_ecmoe'


def rmsnorm(x, g):
    x32 = x.astype(jnp.float32)
    y = x32 * lax.rsqrt(jnp.mean(x32 * x32, axis=-1, keepdims=True) + NORM_EPS)
    return (y * g.astype(jnp.float32)).astype(x.dtype)


def modulate(h, shift, scale):
    return h * (1 + scale) + shift


def short_conv(z, w, b):
    taps = w.shape[0]
    length = z.shape[1]
    pad = taps // 2
    zp = jnp.pad(z, ((0, 0), (pad, taps - 1 - pad), (0, 0)))
    y = b
    for t in range(taps):
        y = y + zp[:, t:t + length] * w[t]
    return y


def dense_attention(q, k, v, scale):
    bsz, lq, nh, dk = q.shape
    dv = v.shape[-1]
    nb = lq // ATTN_QBLOCK
    qb = jnp.moveaxis(q.reshape(bsz, nb, ATTN_QBLOCK, nh, dk), 1, 0)

    def block(qi):
        s = jnp.einsum('bqhd,bkhd->bhqk', qi, k).astype(jnp.float32) * scale
        p = jax.nn.softmax(s, axis=-1).astype(v.dtype)
        return jnp.einsum('bhqk,bkhd->bqhd', p, v)

    o = lax.map(block, qb)
    return jnp.moveaxis(o, 0, 1).reshape(bsz, lq, nh * dv)


def axial_rope(length):
    t = jnp.arange(length)
    row = (t // GRID_W).astype(jnp.float32)
    col = (t % GRID_W).astype(jnp.float32)
    n_freq = MLA_ROPE_DIM // 4
    inv = ROPE_BASE ** (-jnp.arange(n_freq, dtype=jnp.float32) / n_freq)
    ang = jnp.concatenate([row[:, None] * inv, col[:, None] * inv], axis=-1)
    return jnp.cos(ang), jnp.sin(ang)


def apply_rope(x, cos, sin):
    x32 = x.astype(jnp.float32)
    half = x.shape[-1] // 2
    x1, x2 = x32[..., :half], x32[..., half:]
    return jnp.concatenate([x1 * cos - x2 * sin, x2 * cos + x1 * sin], axis=-1).astype(x.dtype)


def na_indices(rows):
    kh = min(NA_WIN_H, rows)
    r = np.arange(rows)
    rs = np.clip(r - kh // 2, 0, rows - kh)
    key_rows = rs[:, None] + np.arange(kh)[None, :]
    dr = key_rows - r[:, None] + (NA_WIN_H - 1)
    ncb = GRID_W // NA_QBLOCK_W
    qcol = np.arange(GRID_W).reshape(ncb, NA_QBLOCK_W)
    cs = np.clip(qcol - NA_WIN_W // 2, 0, GRID_W - NA_WIN_W)
    kc0 = np.clip(np.arange(ncb) * NA_QBLOCK_W - NA_WIN_W // 2, 0, GRID_W - NA_KBLOCK_W)
    kcol = kc0[:, None] + np.arange(NA_KBLOCK_W)[None, :]
    col_ok = (kcol[:, None, :] >= cs[:, :, None]) & (kcol[:, None, :] < cs[:, :, None] + NA_WIN_W)
    dc = np.clip(kcol[:, None, :] - qcol[:, :, None], -(NA_WIN_W - 1), NA_WIN_W - 1) + NA_WIN_W - 1
    key_idx = key_rows[:, None, :, None] * GRID_W + kcol[None, :, None, :]
    return (kh, key_idx.reshape(rows, ncb, kh * NA_KBLOCK_W).astype(np.int32), dr.astype(np.int32),
            col_ok, dc.astype(np.int32))


def neighbourhood_attention(h_lat, h_ctx, w_qkv, rpb, w_o):
    bsz, length, _ = h_lat.shape
    rows = length // GRID_W
    nh, hd = NA_HEADS, NA_HEAD_DIM
    scale = hd ** -0.5
    qkv = (h_lat @ w_qkv).reshape(bsz, length, 3, nh, hd)
    q, k, v = qkv[:, :, 0], qkv[:, :, 1], qkv[:, :, 2]
    qkv_c = (h_ctx @ w_qkv).reshape(bsz, h_ctx.shape[1], 3, nh, hd)
    qc, kc, vc = qkv_c[:, :, 0], qkv_c[:, :, 1], qkv_c[:, :, 2]
    lc = kc.shape[1]
    y_ctx = dense_attention(qc, kc, vc, scale)

    kh, key_idx, dr, col_ok, dc = na_indices(rows)
    ncb = GRID_W // NA_QBLOCK_W
    nk = kh * NA_KBLOCK_W
    mask = np.broadcast_to(col_ok[:, :, None, :], (ncb, NA_QBLOCK_W, kh, NA_KBLOCK_W)).reshape(ncb, NA_QBLOCK_W, nk)
    dc_b = dc[:, :, None, :]
    q_rows = jnp.moveaxis(q.reshape(bsz, rows, GRID_W, nh, hd), 1, 0)

    def row_block(args):
        q_r, kidx, dr_r = args
        qr = q_r.reshape(bsz, ncb, NA_QBLOCK_W, nh, hd)
        kg = k[:, kidx]
        vg = v[:, kidx]
        bias = rpb[:, dr_r[None, None, :, None], dc_b].reshape(nh, ncb, NA_QBLOCK_W, nk)
        s_loc = jnp.einsum('bjuhd,bjkhd->bhjuk', qr, kg).astype(jnp.float32) * scale + bias.astype(jnp.float32)[None]
        s_loc = jnp.where(mask, s_loc, -jnp.inf)
        s_ctx = jnp.einsum('bjuhd,bchd->bhjuc', qr, kc).astype(jnp.float32) * scale
        p = jax.nn.softmax(jnp.concatenate([s_ctx, s_loc], axis=-1), axis=-1).astype(v.dtype)
        o = (jnp.einsum('bhjuc,bchd->bjuhd', p[..., :lc], vc)
             + jnp.einsum('bhjuk,bjkhd->bjuhd', p[..., lc:], vg))
        return o.reshape(bsz, GRID_W, nh, hd)

    o = lax.map(row_block, (q_rows, jnp.asarray(key_idx), jnp.asarray(dr)))
    y_lat = jnp.moveaxis(o, 0, 1).reshape(bsz, length, nh * hd)
    return y_lat @ w_o, y_ctx @ w_o


def mla_project(u, w_in, q_norm_g, w_q_b, kv_norm_g, w_kv_b, rope):
    bsz, length, _ = u.shape
    nh = MLA_HEADS
    z = u @ w_in
    cq = z[..., :MLA_Q_RANK]
    ckv = z[..., MLA_Q_RANK:MLA_Q_RANK + MLA_KV_RANK]
    k_rope = z[..., MLA_Q_RANK + MLA_KV_RANK:]
    q = (rmsnorm(cq, q_norm_g) @ w_q_b).reshape(bsz, length, nh, MLA_NOPE_DIM + MLA_ROPE_DIM)
    kv = (rmsnorm(ckv, kv_norm_g) @ w_kv_b).reshape(bsz, length, nh, MLA_NOPE_DIM + MLA_V_DIM)
    q_nope, q_rope = q[..., :MLA_NOPE_DIM], q[..., MLA_NOPE_DIM:]
    k_nope, v = kv[..., :MLA_NOPE_DIM], kv[..., MLA_NOPE_DIM:]
    if rope is not None:
        cos, sin = rope
        q_rope = apply_rope(q_rope, cos[:, None, :], sin[:, None, :])
        k_rope = apply_rope(k_rope, cos, sin)
    k = jnp.concatenate([k_nope, jnp.broadcast_to(k_rope[:, :, None, :], (bsz, length, nh, MLA_ROPE_DIM))], axis=-1)
    q = jnp.concatenate([q_nope, q_rope], axis=-1)
    return q, k, v


def mla_attention(h_lat, h_ctx, w_in, q_norm_g, w_q_b, kv_norm_g, w_kv_b, w_o):
    rope = axial_rope(h_lat.shape[1])
    ql, kl, vl = mla_project(h_lat, w_in, q_norm_g, w_q_b, kv_norm_g, w_kv_b, rope)
    qc, kc, vc = mla_project(h_ctx, w_in, q_norm_g, w_q_b, kv_norm_g, w_kv_b, None)
    scale = (MLA_NOPE_DIM + MLA_ROPE_DIM) ** -0.5
    y_lat = dense_attention(ql, jnp.concatenate([kc, kl], axis=1), jnp.concatenate([vc, vl], axis=1), scale)
    y_ctx = dense_attention(qc, kc, vc, scale)
    return y_lat @ w_o, y_ctx @ w_o


def hyena_filter(length, w1, b1, w2, b2, w3, b3, sin_freq):
    f32 = jnp.float32
    t = jnp.linspace(0.0, 1.0, length, dtype=f32)[:, None]
    bands = (HY_EMB_DIM - 1) // 2
    w = (2.0 * math.pi / length) * jnp.arange(length, dtype=f32)[:, None]
    f = jnp.linspace(1e-4, bands - 1, bands, dtype=f32)[None, :]
    z = jnp.concatenate([t, jnp.cos(f * w), -jnp.sin(f * w)], axis=-1)
    sf = sin_freq.astype(f32)
    h = jnp.sin(sf[0] * (z @ w1.astype(f32) + b1.astype(f32)))
    h = jnp.sin(sf[1] * (h @ w2.astype(f32) + b2.astype(f32)))
    h = h @ w3.astype(f32) + b3.astype(f32)
    max_decay = math.log(HY_DECAY_TARGET) / HY_FAST_DECAY
    min_decay = math.log(HY_DECAY_TARGET) / HY_SLOW_DECAY
    deltas = jnp.abs(jnp.linspace(min_decay, max_decay, h.shape[-1] // 2, dtype=f32))
    deltas = jnp.tile(deltas, 2)
    return h * (jnp.exp(-t * deltas) + HY_MOD_SHIFT)


def bidir_long_conv(u, h):
    length, d = u.shape[1], u.shape[2]
    hf, hb = h[:, :d], h[:, d:]
    k = jnp.concatenate([hf, jnp.zeros((1, d), jnp.float32), hb[1:][::-1]], axis=0)
    kf = jnp.fft.rfft(k, n=2 * length, axis=0)
    uf = jnp.fft.rfft(u, n=2 * length, axis=1)
    return jnp.fft.irfft(uf * kf[None], n=2 * length, axis=1)[:, :length]


def hyena_mixer(h_lat, h_ctx, w_in, conv_w, conv_b, f_w1, f_b1, f_w2, f_b2, f_w3, f_b3, sin_freq, skip, w_o):
    def one(u):
        z = short_conv(u @ w_in, conv_w, conv_b)
        x0, x1, v = jnp.split(z, 3, axis=-1)
        filt = hyena_filter(u.shape[1], f_w1, f_b1, f_w2, f_b2, f_w3, f_b3, sin_freq)
        g = (v * x1).astype(jnp.float32)
        y = bidir_long_conv(g, filt) + g * skip.astype(jnp.float32)
        return (y.astype(u.dtype) * x0) @ w_o
    return one(h_lat), one(h_ctx)


def mlstm_scan(q, k, v, ig, lf, state):
    bsz, nh, length, dk = q.shape
    dv = v.shape[-1]
    nc = length // ML_CHUNK

    def chunks(a):
        return jnp.moveaxis(a.reshape(a.shape[:2] + (nc, ML_CHUNK) + a.shape[3:]), 2, 0)

    causal = jnp.tril(jnp.ones((ML_CHUNK, ML_CHUNK), dtype=bool))

    def step(carry, xs):
        c_st, n_st, m_st = carry
        qc, kc, vc, ic, fc = xs
        b = jnp.cumsum(fc, axis=-1)
        log_d = jnp.where(causal, b[..., :, None] - b[..., None, :] + ic[..., None, :], -jnp.inf)
        m_inter = b + m_st[..., None]
        m_t = jnp.maximum(jnp.max(log_d, axis=-1), m_inter)
        s = jnp.einsum('bhtd,bhsd->bhts', qc, kc) * jnp.exp(log_d - m_t[..., None])
        inter = jnp.exp(m_inter - m_t)
        num = jnp.einsum('bhts,bhsv->bhtv', s, vc) + inter[..., None] * jnp.einsum('bhvd,bhtd->bhtv', c_st, qc)
        qn = jnp.sum(s, axis=-1) + inter * jnp.einsum('bhd,bhtd->bht', n_st, qc)
        h = num / jnp.maximum(jnp.abs(qn), jnp.exp(-m_t))[..., None]
        b_end = b[..., -1]
        w_log = b_end[..., None] - b + ic
        m_new = jnp.maximum(b_end + m_st, jnp.max(w_log, axis=-1))
        w = jnp.exp(w_log - m_new[..., None])
        decay = jnp.exp(b_end + m_st - m_new)
        c_new = decay[..., None, None] * c_st + jnp.einsum('bht,bhtv,bhtd->bhvd', w, vc, kc)
        n_new = decay[..., None] * n_st + jnp.einsum('bht,bhtd->bhd', w, kc)
        return (c_new, n_new, m_new), h

    state, hs = lax.scan(step, state, (chunks(q), chunks(k), chunks(v), chunks(ig), chunks(lf)))
    return state, jnp.moveaxis(hs, 0, 2).reshape(bsz, nh, length, dv)


def mlstm_mixer(h_lat, h_ctx, w_in, conv_w, conv_b, gate_b, out_norm_g, w_o):
    nh, dk, dv = ML_HEADS, ML_QK_DIM, ML_V_DIM
    f32 = jnp.float32
    nqk = 2 * nh * dk

    def project(u):
        bsz, length, _ = u.shape
        z = u @ w_in
        qk = jax.nn.silu(short_conv(z[..., :nqk], conv_w, conv_b))
        v = z[..., nqk:nqk + nh * dv]
        o = z[..., nqk + nh * dv:nqk + 2 * nh * dv]
        g = (z[..., nqk + 2 * nh * dv:] + gate_b).astype(f32)

        def heads(a, d):
            return a.reshape(bsz, length, nh, d).transpose(0, 2, 1, 3).astype(f32)

        q = heads(qk[..., :nh * dk], dk)
        k = heads(qk[..., nh * dk:], dk) * (dk ** -0.5)
        vh = heads(v, dv)
        g = g.reshape(bsz, length, 4, nh).transpose(2, 0, 3, 1)
        gates = ((g[0], jax.nn.log_sigmoid(g[1])), (g[2], jax.nn.log_sigmoid(g[3])))
        return q, k, vh, o, gates

    ql, kl, vl, ol, gl = project(h_lat)
    qc, kc, vc, oc, gc = project(h_ctx)
    bsz = ql.shape[0]
    zero = (jnp.zeros((bsz, nh, dv, dk), f32), jnp.zeros((bsz, nh, dk), f32), jnp.zeros((bsz, nh), f32))

    def flip(a):
        return jnp.flip(a, axis=2)

    st_f, hc_f = mlstm_scan(qc, kc, vc, gc[0][0], gc[0][1], zero)
    _, hl_f = mlstm_scan(ql, kl, vl, gl[0][0], gl[0][1], st_f)
    st_b, hc_b = mlstm_scan(flip(qc), flip(kc), flip(vc), flip(gc[1][0]), flip(gc[1][1]), zero)
    _, hl_b = mlstm_scan(flip(ql), flip(kl), flip(vl), flip(gl[1][0]), flip(gl[1][1]), st_b)

    def finish(h, o):
        b_, _, length, _ = h.shape
        h = h * lax.rsqrt(jnp.mean(h * h, axis=-1, keepdims=True) + NORM_EPS)
        h = h.transpose(0, 2, 1, 3).reshape(b_, length, nh * dv) * out_norm_g.astype(f32)
        return (h.astype(o.dtype) * jax.nn.sigmoid(o)) @ w_o

    return finish(hl_f + flip(hl_b), ol), finish(hc_f + flip(hc_b), oc)


def ec_moe(h, router_w, w_gate, w_up, w_down):
    bsz, length, _ = h.shape
    cap = max(1, EC_CAPACITY_FACTOR * length // N_EXPERTS)
    aff = jax.nn.softmax((h @ router_w).astype(jnp.float32), axis=-1)
    gval, tok = lax.top_k(jnp.swapaxes(aff, 1, 2), cap)
    bidx = jnp.arange(bsz)[:, None, None]
    xg = h[bidx, tok]
    a = jnp.einsum('becd,edf->becf', xg, w_gate)
    u = jnp.einsum('becd,edf->becf', xg, w_up)
    y = jnp.einsum('becf,efd->becd', jax.nn.silu(a) * u, w_down) * gval[..., None].astype(h.dtype)
    return jnp.zeros_like(h).at[bidx, tok].add(y)


def setup_inputs(seed: int = 0) -> dict:
    key = jax.random.key(seed)
    ks = iter(jax.random.split(key, 48))

    def nrm(shape, scale):
        return scale * jax.random.normal(next(ks), shape, jnp.float32)

    d = D_MODEL
    e, ff = N_EXPERTS, EXPERT_FF
    n_a, n_b, n_c, n_d = [len(range(m, DEPTH, N_MIXERS)) for m in range(N_MIXERS)]
    hid = HY_FILTER_HIDDEN
    ml_in = 2 * ML_HEADS * ML_QK_DIM + 2 * ML_HEADS * ML_V_DIM + 4 * ML_HEADS
    gate_off = jnp.tile(jnp.repeat(jnp.array([0.0, ML_FORGET_BIAS], jnp.float32), ML_HEADS), 2)
    return {
        'x': nrm((BATCH, SEQ, d), 1.0),
        'c': nrm((BATCH, d), 1.0),
        'ctx': nrm((BATCH, CTX_LEN, d), 1.0),
        'c_ctx': nrm((d,), 1.0),
        'mod_w': nrm((DEPTH, d, 6 * d), 0.5 * d ** -0.5),
        'mod_b': nrm((DEPTH, 6 * d), 0.02),
        'norm_mix_g': 1.0 + nrm((DEPTH, d), 0.02),
        'norm_ffn_g': 1.0 + nrm((DEPTH, d), 0.02),
        'router_w': nrm((DEPTH, d, e), d ** -0.5),
        'moe_w_gate': nrm((DEPTH, e, d, ff), d ** -0.5),
        'moe_w_up': nrm((DEPTH, e, d, ff), d ** -0.5),
        'moe_w_down': nrm((DEPTH, e, ff, d), ff ** -0.5),
        'na_w_qkv': nrm((n_a, d, 3 * d), d ** -0.5),
        'na_rpb': nrm((n_a, NA_HEADS, 2 * NA_WIN_H - 1, 2 * NA_WIN_W - 1), 0.02),
        'na_w_o': nrm((n_a, d, d), d ** -0.5),
        'mla_w_in': nrm((n_b, d, MLA_Q_RANK + MLA_KV_RANK + MLA_ROPE_DIM), d ** -0.5),
        'mla_q_norm_g': 1.0 + nrm((n_b, MLA_Q_RANK), 0.02),
        'mla_w_q_b': nrm((n_b, MLA_Q_RANK, MLA_HEADS * (MLA_NOPE_DIM + MLA_ROPE_DIM)), MLA_Q_RANK ** -0.5),
        'mla_kv_norm_g': 1.0 + nrm((n_b, MLA_KV_RANK), 0.02),
        'mla_w_kv_b': nrm((n_b, MLA_KV_RANK, MLA_HEADS * (MLA_NOPE_DIM + MLA_V_DIM)), MLA_KV_RANK ** -0.5),
        'mla_w_o': nrm((n_b, MLA_HEADS * MLA_V_DIM, d), (MLA_HEADS * MLA_V_DIM) ** -0.5),
        'hy_w_in': nrm((n_c, d, 3 * d), d ** -0.5),
        'hy_conv_w': nrm((n_c, HY_SHORT_CONV, 3 * d), HY_SHORT_CONV ** -0.5),
        'hy_conv_b': nrm((n_c, 3 * d), 0.02),
        'hy_f_w1': nrm((n_c, HY_EMB_DIM, hid), HY_EMB_DIM ** -0.5),
        'hy_f_b1': nrm((n_c, hid), 0.02),
        'hy_f_w2': nrm((n_c, hid, hid), hid ** -0.5),
        'hy_f_b2': nrm((n_c, hid), 0.02),
        'hy_f_w3': nrm((n_c, hid, 2 * d), 0.2 * hid ** -0.5),
        'hy_f_b3': nrm((n_c, 2 * d), 0.02),
        'hy_sin_freq': 1.0 + nrm((n_c, 2, hid), 0.02),
        'hy_skip': nrm((n_c, d), 0.5),
        'hy_w_o': nrm((n_c, d, d), d ** -0.5),
        'ml_w_in': nrm((n_d, d, ml_in), d ** -0.5),
        'ml_conv_w': nrm((n_d, ML_SHORT_CONV, 2 * ML_HEADS * ML_QK_DIM), ML_SHORT_CONV ** -0.5),
        'ml_conv_b': nrm((n_d, 2 * ML_HEADS * ML_QK_DIM), 0.02),
        'ml_gate_b': gate_off + nrm((n_d, 4 * ML_HEADS), 0.1),
        'ml_out_norm_g': 1.0 + nrm((n_d, ML_HEADS * ML_V_DIM), 0.02),
        'ml_w_o': nrm((n_d, ML_HEADS * ML_V_DIM, d), (ML_HEADS * ML_V_DIM) ** -0.5),
        'final_norm_g': 1.0 + nrm((d,), 0.02),
    }


def reference(x, c, ctx, c_ctx, mod_w, mod_b, norm_mix_g, norm_ffn_g, router_w, moe_w_gate, moe_w_up,
              moe_w_down, na_w_qkv, na_rpb, na_w_o, mla_w_in, mla_q_norm_g, mla_w_q_b, mla_kv_norm_g,
              mla_w_kv_b, mla_w_o, hy_w_in, hy_conv_w, hy_conv_b, hy_f_w1, hy_f_b1, hy_f_w2, hy_f_b2, hy_f_w3,
              hy_f_b3, hy_sin_freq, hy_skip, hy_w_o, ml_w_in, ml_conv_w, ml_conv_b, ml_gate_b, ml_out_norm_g,
              ml_w_o, final_norm_g):
    silu_c = jax.nn.silu(c)
    silu_cc = jax.nn.silu(c_ctx)
    for i in range(DEPTH):
        last = i == DEPTH - 1
        mod = (silu_c @ mod_w[i] + mod_b[i])[:, None, :]
        mod_c = silu_cc @ mod_w[i] + mod_b[i]
        sh1, sc1, g1, sh2, sc2, g2 = jnp.split(mod, 6, axis=-1)
        csh1, csc1, cg1, csh2, csc2, cg2 = jnp.split(mod_c, 6, axis=-1)
        h = modulate(rmsnorm(x, norm_mix_g[i]), sh1, sc1)
        hc = modulate(rmsnorm(ctx, norm_mix_g[i]), csh1, csc1)
        kind, j = i % N_MIXERS, i // N_MIXERS
        if kind == 0:
            y, yc = neighbourhood_attention(h, hc, na_w_qkv[j], na_rpb[j], na_w_o[j])
        elif kind == 1:
            y, yc = mla_attention(h, hc, mla_w_in[j], mla_q_norm_g[j], mla_w_q_b[j], mla_kv_norm_g[j],
                                  mla_w_kv_b[j], mla_w_o[j])
        elif kind == 2:
            y, yc = hyena_mixer(h, hc, hy_w_in[j], hy_conv_w[j], hy_conv_b[j], hy_f_w1[j], hy_f_b1[j],
                                hy_f_w2[j], hy_f_b2[j], hy_f_w3[j], hy_f_b3[j], hy_sin_freq[j], hy_skip[j],
                                hy_w_o[j])
        else:
            y, yc = mlstm_mixer(h, hc, ml_w_in[j], ml_conv_w[j], ml_conv_b[j], ml_gate_b[j],
                                ml_out_norm_g[j], ml_w_o[j])
        x = x + g1 * y
        h = modulate(rmsnorm(x, norm_ffn_g[i]), sh2, sc2)
        x = x + g2 * ec_moe(h, router_w[i], moe_w_gate[i], moe_w_up[i], moe_w_down[i])
        if not last:
            ctx = ctx + cg1 * yc
            hc = modulate(rmsnorm(ctx, norm_ffn_g[i]), csh2, csc2)
            ctx = ctx + cg2 * ec_moe(hc, router_w[i], moe_w_gate[i], moe_w_up[i], moe_w_down[i])
    return rmsnorm(x, final_norm_g)
```

```python
import functools
import math

import numpy as np
import jax
import jax.numpy as jnp
from jax import lax
from jax.experimental import pallas as pl
from jax.experimental.pallas import tpu as pltpu

F32 = jnp.float32
BF16 = jnp.bfloat16
HIGHEST = lax.Precision.HIGHEST

V7X_VMEM_LIMIT_BYTES = 56 * 1024 * 1024
LANES = 128

GRID_W = 64
NORM_EPS = 1e-6
ROPE_BASE = 10000.0

NA_HEADS = 16
NA_HEAD_DIM = 64
NA_WIN_H = 8
NA_WIN_W = 16
NA_QROWS = 4
NA_KROWS = NA_QROWS - 1 + NA_WIN_H

MLA_HEADS = 16
MLA_Q_RANK = 384
MLA_KV_RANK = 256
MLA_NOPE_DIM = 64
MLA_ROPE_DIM = 32
MLA_V_DIM = 64

HY_EMB_DIM = 33
HY_DECAY_TARGET = 1e-2
HY_FAST_DECAY = 0.3
HY_SLOW_DECAY = 1.5
HY_MOD_SHIFT = 0.05

ML_HEADS = 8
ML_V_DIM = 128
ML_QK_DIM = 64
ML_CHUNK = 256

N_EXPERTS = 16
EC_CAPACITY_FACTOR = 2
SELECT_BISECTIONS = 64


def _cparams(*sem):
    return pltpu.CompilerParams(dimension_semantics=sem, vmem_limit_bytes=V7X_VMEM_LIMIT_BYTES)


def _pick(n, prefs):
    for p in prefs:
        if n % p == 0:
            return p
    return n


def _proj_kernel(x_ref, g_ref, sh_ref, sc_ref, w_ref, o_ref, h_ref, *, norm):
    @pl.when(pl.program_id(2) == 0)
    def _():
        x = x_ref[0].astype(F32)
        if norm:
            ms = jnp.mean(x * x, axis=-1, keepdims=True)
            x = x * lax.rsqrt(ms + NORM_EPS) * g_ref[...]
            x = x * (1.0 + sc_ref[0]) + sh_ref[0]
        h_ref[...] = x.astype(BF16)

    o_ref[0] = jnp.dot(h_ref[...], w_ref[...].astype(BF16),
                       preferred_element_type=F32).astype(o_ref.dtype)


def _proj(x, w, g=None, shift=None, scale=None, out_dtype=F32):
    bsz, length, kdim = x.shape
    n = w.shape[1]
    norm = g is not None
    if not norm:
        g = jnp.ones((kdim,), F32)
    if shift is None:
        shift = jnp.zeros((bsz, kdim), F32)
        scale = jnp.zeros((bsz, kdim), F32)
    tm = _pick(length, (1024, 512, 256))
    tn = _pick(n, (512, 384, 256, 128))
    return pl.pallas_call(
        functools.partial(_proj_kernel, norm=norm),
        grid=(bsz, length // tm, n // tn),
        in_specs=[
            pl.BlockSpec((1, tm, kdim), lambda b, i, j: (b, i, 0)),
            pl.BlockSpec((1, kdim), lambda b, i, j: (0, 0)),
            pl.BlockSpec((1, 1, kdim), lambda b, i, j: (b, 0, 0)),
            pl.BlockSpec((1, 1, kdim), lambda b, i, j: (b, 0, 0)),
            pl.BlockSpec((kdim, tn), lambda b, i, j: (0, j)),
        ],
        out_specs=pl.BlockSpec((1, tm, tn), lambda b, i, j: (b, i, j)),
        out_shape=jax.ShapeDtypeStruct((bsz, length, n), out_dtype),
        scratch_shapes=[pltpu.VMEM((tm, kdim), BF16)],
        compiler_params=_cparams("parallel", "parallel", "arbitrary"),
        name="proj",
    )(x, g.reshape(1, kdim).astype(F32), shift.reshape(bsz, 1, kdim), scale.reshape(bsz, 1, kdim), w)


def _proj_res_kernel(y_ref, w_ref, r_ref, gt_ref, o_ref):
    acc = jnp.dot(y_ref[0].astype(BF16), w_ref[...].astype(BF16), preferred_element_type=F32)
    o_ref[0] = r_ref[0] + gt_ref[0] * acc


def _proj_res(y, w, res, gate):
    bsz, length, kdim = y.shape
    n = w.shape[1]
    tm = _pick(length, (1024, 512, 256))
    tn = _pick(n, (512, 256, 128))
    return pl.pallas_call(
        _proj_res_kernel,
        grid=(bsz, length // tm, n // tn),
        in_specs=[
            pl.BlockSpec((1, tm, kdim), lambda b, i, j: (b, i, 0)),
            pl.BlockSpec((kdim, tn), lambda b, i, j: (0, j)),
            pl.BlockSpec((1, tm, tn), lambda b, i, j: (b, i, j)),
            pl.BlockSpec((1, 1, tn), lambda b, i, j: (b, 0, j)),
        ],
        out_specs=pl.BlockSpec((1, tm, tn), lambda b, i, j: (b, i, j)),
        out_shape=jax.ShapeDtypeStruct((bsz, length, n), F32),
        compiler_params=_cparams("parallel", "parallel", "arbitrary"),
        name="proj_res",
    )(y, w, res, gate.reshape(bsz, 1, n))


def _mod_kernel(c_ref, w_ref, b_ref, o_ref):
    c = c_ref[...]
    s = c * jax.nn.sigmoid(c)
    o_ref[...] = jnp.dot(s.astype(BF16), w_ref[...].astype(BF16), preferred_element_type=F32) + b_ref[...]


def _modulation(cc, w, b):
    rows, d = cc.shape
    n = w.shape[1]
    tn = 512
    return pl.pallas_call(
        _mod_kernel,
        grid=(n // tn,),
        in_specs=[
            pl.BlockSpec((rows, d), lambda j: (0, 0)),
            pl.BlockSpec((d, tn), lambda j: (0, j)),
            pl.BlockSpec((1, tn), lambda j: (0, j)),
        ],
        out_specs=pl.BlockSpec((rows, tn), lambda j: (0, j)),
        out_shape=jax.ShapeDtypeStruct((rows, n), F32),
        compiler_params=_cparams("parallel"),
        name="modulation",
    )(cc, w, b.reshape(1, n))


def _final_norm_kernel(x_ref, g_ref, o_ref):
    x = x_ref[0]
    ms = jnp.mean(x * x, axis=-1, keepdims=True)
    o_ref[0] = x * lax.rsqrt(ms + NORM_EPS) * g_ref[...]


def _final_norm(x, g):
    bsz, length, d = x.shape
    tm = 1024
    return pl.pallas_call(
        _final_norm_kernel,
        grid=(bsz, length // tm),
        in_specs=[pl.BlockSpec((1, tm, d), lambda b, i: (b, i, 0)),
                  pl.BlockSpec((1, d), lambda b, i: (0, 0))],
        out_specs=pl.BlockSpec((1, tm, d), lambda b, i: (b, i, 0)),
        out_shape=jax.ShapeDtypeStruct(x.shape, F32),
        compiler_params=_cparams("parallel", "parallel"),
        name="final_norm",
    )(x, g.reshape(1, d))


def _softmax_pv(s_parts, v_parts):
    m = s_parts[0].max(axis=-1, keepdims=True)
    for s in s_parts[1:]:
        m = jnp.maximum(m, s.max(axis=-1, keepdims=True))
    den = None
    acc = None
    for s, v in zip(s_parts, v_parts):
        p = jnp.exp(s - m)
        l = p.sum(axis=-1, keepdims=True)
        o = jnp.dot(p.astype(BF16), v, preferred_element_type=F32)
        den = l if den is None else den + l
        acc = o if acc is None else acc + o
    return acc / den


def _attn_kernel(q_ref, k_ref, v_ref, o_ref, *, scale):
    outs = []
    for j in range(2):
        s = lax.dot_general(q_ref[0, j], k_ref[0, j], (((1,), (1,)), ((), ())),
                            preferred_element_type=F32) * scale
        outs.append(_softmax_pv([s], [v_ref[0, j]]))
    o_ref[0] = jnp.concatenate(outs, axis=-1).astype(o_ref.dtype)


def _attention(q, k, v, scale):
    bsz, nh, lq, dk = q.shape
    lk, dv = k.shape[2], v.shape[3]
    tq = _pick(lq, (256,))
    return pl.pallas_call(
        functools.partial(_attn_kernel, scale=scale),
        grid=(bsz, nh // 2, lq // tq),
        in_specs=[
            pl.BlockSpec((1, 2, tq, dk), lambda b, h, i: (b, h, i, 0)),
            pl.BlockSpec((1, 2, lk, dk), lambda b, h, i: (b, h, 0, 0)),
            pl.BlockSpec((1, 2, lk, dv), lambda b, h, i: (b, h, 0, 0)),
        ],
        out_specs=pl.BlockSpec((1, tq, 2 * dv), lambda b, h, i: (b, i, h)),
        out_shape=jax.ShapeDtypeStruct((bsz, lq, nh * dv), BF16),
        compiler_params=_cparams("parallel", "parallel", "arbitrary"),
        name="attention",
    )(q, k, v)


def _heads(a, nh):
    bsz, length, _ = a.shape
    return a.reshape(bsz, length, nh, -1).transpose(0, 2, 1, 3)


def _na_bias_table(rpb, rows):
    kh = min(NA_WIN_H, rows)
    nblk = rows // NA_QROWS
    tables = []
    for blk in (0, 1, nblk - 1):
        start = int(np.clip(blk * NA_QROWS - NA_WIN_H // 2, 0, rows - NA_KROWS))
        qi = np.arange(NA_QROWS * GRID_W)
        kj = np.arange(NA_KROWS * GRID_W)
        r = blk * NA_QROWS + qi // GRID_W
        c = qi % GRID_W
        kr = start + kj // GRID_W
        kc = kj % GRID_W
        rs = np.clip(r - kh // 2, 0, rows - kh)
        cs = np.clip(c - NA_WIN_W // 2, 0, GRID_W - NA_WIN_W)
        ok = ((kr[None, :] >= rs[:, None]) & (kr[None, :] < rs[:, None] + kh)
              & (kc[None, :] >= cs[:, None]) & (kc[None, :] < cs[:, None] + NA_WIN_W))
        dr = np.clip(kr[None, :] - r[:, None] + (NA_WIN_H - 1), 0, 2 * NA_WIN_H - 2)
        dc = np.clip(kc[None, :] - c[:, None], -(NA_WIN_W - 1), NA_WIN_W - 1) + NA_WIN_W - 1
        bias = rpb[:, dr, dc].astype(F32)
        tables.append(jnp.where(ok[None], bias, -1e30))
    return jnp.stack(tables, axis=1)


def _na_kernel(q_ref, kl_ref, vl_ref, kc_ref, vc_ref, bias_ref, o_ref, *, scale, nblk, rows):
    i = pl.program_id(2)
    start = jnp.clip(i * NA_QROWS - NA_WIN_H // 2, 0, rows - NA_KROWS) * GRID_W
    start = pl.multiple_of(start, GRID_W)
    pat = jnp.where(i == 0, 0, jnp.where(i == nblk - 1, 2, 1))
    nk = NA_KROWS * GRID_W
    q2 = q_ref[0]
    kl2 = kl_ref[0, pl.ds(start, nk), :]
    vl2 = vl_ref[0, pl.ds(start, nk), :]
    kc2 = kc_ref[0]
    vc2 = vc_ref[0]
    outs = []
    for j in range(2):
        sl = slice(j * NA_HEAD_DIM, (j + 1) * NA_HEAD_DIM)
        q = q2[:, sl]
        nt = (((1,), (1,)), ((), ()))
        s_ctx = lax.dot_general(q, kc2[:, sl], nt, preferred_element_type=F32) * scale
        s_loc = lax.dot_general(q, kl2[:, sl], nt, preferred_element_type=F32) * scale + bias_ref[j, pat]
        outs.append(_softmax_pv([s_ctx, s_loc], [vc2[:, sl], vl2[:, sl]]))
    o_ref[0] = jnp.concatenate(outs, axis=-1).astype(o_ref.dtype)


def _na_mixer(x, ctx, g, sh, sc, csh, csc, w_qkv, rpb, w_o, gate, cgate):
    bsz, length, d = x.shape
    lc = ctx.shape[1]
    rows = length // GRID_W
    nblk = rows // NA_QROWS
    nh, hd = NA_HEADS, NA_HEAD_DIM
    scale = hd ** -0.5
    qkv = _proj(x, w_qkv, g, sh, sc, out_dtype=BF16)
    qkv_c = _proj(ctx, w_qkv, g, csh, csc, out_dtype=BF16)
    bias = _na_bias_table(rpb, rows)
    tq = NA_QROWS * GRID_W
    npair = nh // 2
    pw = 2 * hd
    y = pl.pallas_call(
        functools.partial(_na_kernel, scale=scale, nblk=nblk, rows=rows),
        grid=(npair, bsz, nblk),
        in_specs=[
            pl.BlockSpec((1, tq, pw), lambda h, b, i: (b, i, h)),
            pl.BlockSpec((1, length, pw), lambda h, b, i: (b, 0, npair + h)),
            pl.BlockSpec((1, length, pw), lambda h, b, i: (b, 0, 2 * npair + h)),
            pl.BlockSpec((1, lc, pw), lambda h, b, i: (b, 0, npair + h)),
            pl.BlockSpec((1, lc, pw), lambda h, b, i: (b, 0, 2 * npair + h)),
            pl.BlockSpec((2, 3, tq, NA_KROWS * GRID_W), lambda h, b, i: (h, 0, 0, 0)),
        ],
        out_specs=pl.BlockSpec((1, tq, pw), lambda h, b, i: (b, i, h)),
        out_shape=jax.ShapeDtypeStruct((bsz, length, d), BF16),
        compiler_params=_cparams("parallel", "parallel", "arbitrary"),
        name="na_local",
    )(qkv, qkv, qkv, qkv_c, qkv_c, bias)
    qc = _heads(qkv_c[..., :d], nh)
    kc = _heads(qkv_c[..., d:2 * d], nh)
    vc = _heads(qkv_c[..., 2 * d:], nh)
    yc = _attention(qc, kc, vc, scale)
    return _proj_res(y, w_o, x, gate), (_proj_res(yc, w_o, ctx, cgate) if cgate is not None else None)


def _axial_rope(length):
    t = jnp.arange(length)
    row = (t // GRID_W).astype(F32)
    col = (t % GRID_W).astype(F32)
    n_freq = MLA_ROPE_DIM // 4
    inv = ROPE_BASE ** (-jnp.arange(n_freq, dtype=F32) / n_freq)
    ang = jnp.concatenate([row[:, None] * inv, col[:, None] * inv], axis=-1)
    return jnp.cos(ang), jnp.sin(ang)


def _apply_rope(x, cos, sin):
    half = x.shape[-1] // 2
    x1, x2 = x[..., :half], x[..., half:]
    return jnp.concatenate([x1 * cos - x2 * sin, x2 * cos + x1 * sin], axis=-1)


def _mla_project(u, g, sh, sc, w_in, q_norm_g, w_q_b, kv_norm_g, w_kv_b, rope):
    bsz, length, _ = u.shape
    nh = MLA_HEADS
    z = _proj(u, w_in, g, sh, sc)
    cq = z[..., :MLA_Q_RANK]
    ckv = z[..., MLA_Q_RANK:MLA_Q_RANK + MLA_KV_RANK]
    k_rope = z[..., MLA_Q_RANK + MLA_KV_RANK:]
    q = _proj(cq, w_q_b, q_norm_g).reshape(bsz, length, nh, MLA_NOPE_DIM + MLA_ROPE_DIM)
    kv = _proj(ckv, w_kv_b, kv_norm_g).reshape(bsz, length, nh, MLA_NOPE_DIM + MLA_V_DIM)
    q_nope, q_rope = q[..., :MLA_NOPE_DIM], q[..., MLA_NOPE_DIM:]
    k_nope, v = kv[..., :MLA_NOPE_DIM], kv[..., MLA_NOPE_DIM:]
    if rope is not None:
        cos, sin = rope
        q_rope = _apply_rope(q_rope, cos[:, None, :], sin[:, None, :])
        k_rope = _apply_rope(k_rope, cos, sin)
    k = jnp.concatenate(
        [k_nope, jnp.broadcast_to(k_rope[:, :, None, :], (bsz, length, nh, MLA_ROPE_DIM))], axis=-1)
    q = jnp.concatenate([q_nope, q_rope], axis=-1)
    tr = lambda a: a.astype(BF16).transpose(0, 2, 1, 3)
    return tr(q), tr(k), tr(v)


def _mla_mixer(x, ctx, g, sh, sc, csh, csc, w_in, q_norm_g, w_q_b, kv_norm_g, w_kv_b, w_o, gate, cgate):
    rope = _axial_rope(x.shape[1])
    ql, kl, vl = _mla_project(x, g, sh, sc, w_in, q_norm_g, w_q_b, kv_norm_g, w_kv_b, rope)
    qc, kc, vc = _mla_project(ctx, g, csh, csc, w_in, q_norm_g, w_q_b, kv_norm_g, w_kv_b, None)
    scale = (MLA_NOPE_DIM + MLA_ROPE_DIM) ** -0.5
    y = _attention(ql, jnp.concatenate([kc, kl], axis=2), jnp.concatenate([vc, vl], axis=2), scale)
    yc = _attention(qc, kc, vc, scale)
    return _proj_res(y, w_o, x, gate), (_proj_res(yc, w_o, ctx, cgate) if cgate is not None else None)


def _conv3(z, w_ref, b_ref):
    length = z.shape[0]
    t = lax.broadcasted_iota(jnp.int32, z.shape, 0)
    prev = jnp.where(t == 0, 0.0, pltpu.roll(z, 1, axis=0))
    nxt = jnp.where(t == length - 1, 0.0, pltpu.roll(z, length - 1, axis=0))
    return b_ref[...] + prev * w_ref[0:1, :] + z * w_ref[1:2, :] + nxt * w_ref[2:3, :]


def _hy_conv_kernel(z0_ref, z1_ref, z2_ref, w0_ref, w1_ref, w2_ref, b0_ref, b1_ref, b2_ref,
                    x0_ref, gg_ref):
    x0_ref[0] = _conv3(z0_ref[0], w0_ref, b0_ref)
    x1 = _conv3(z1_ref[0], w1_ref, b1_ref)
    v = _conv3(z2_ref[0], w2_ref, b2_ref)
    gg_ref[0] = v * x1


def _hy_conv(z, conv_w, conv_b, d):
    bsz, length, _ = z.shape
    tc = 256
    nb = d // tc
    zspec = lambda k: pl.BlockSpec((1, length, tc), lambda b, j: (b, 0, k * nb + j))
    wspec = lambda k: pl.BlockSpec((3, tc), lambda b, j: (0, k * nb + j))
    bspec = lambda k: pl.BlockSpec((1, tc), lambda b, j: (0, k * nb + j))
    ospec = pl.BlockSpec((1, length, tc), lambda b, j: (b, 0, j))
    cb = conv_b.reshape(1, 3 * d)
    return pl.pallas_call(
        _hy_conv_kernel,
        grid=(bsz, nb),
        in_specs=[zspec(0), zspec(1), zspec(2), wspec(0), wspec(1), wspec(2), bspec(0), bspec(1), bspec(2)],
        out_specs=[ospec, ospec],
        out_shape=[jax.ShapeDtypeStruct((bsz, length, d), F32)] * 2,
        compiler_params=_cparams("parallel", "parallel"),
        name="hy_conv",
    )(z, z, z, conv_w, conv_w, conv_w, cb, cb, cb)


def _dft_tables(length):
    n2 = 2 * length
    f = jnp.arange(length, dtype=jnp.int32)[:, None]
    n = jnp.arange(length, dtype=jnp.int32)[None, :]
    ang = ((f * n) % n2).astype(F32) * (2.0 * math.pi / n2)
    cf = jnp.cos(ang)
    sgn_n = jnp.where(n % 2 == 0, 1.0, -1.0).astype(F32)
    sf = jnp.where(f == 0, sgn_n, -jnp.sin(ang))
    return cf.astype(BF16), sf.astype(BF16), sf.T.astype(BF16)


def _hy_fwd_kernel(cf_ref, sf_ref, gg_ref, kre_ref, kim_ref, y_ref, gb_ref):
    j = pl.program_id(1)

    @pl.when(j == 0)
    def _():
        gb_ref[...] = gg_ref[0].astype(BF16)

    ure = jnp.dot(cf_ref[...], gb_ref[...], preferred_element_type=F32)
    uim = jnp.dot(sf_ref[...], gb_ref[...], preferred_element_type=F32)
    kre = kre_ref[...]
    kim = kim_ref[...]
    tf = ure.shape[0]
    row = lax.broadcasted_iota(jnp.int32, ure.shape, 0) + j * tf
    packed = row == 0
    yre = ure * kre - jnp.where(packed, 0.0, uim * kim)
    yim = uim * jnp.where(packed, kim, kre) + jnp.where(packed, 0.0, ure * kim)
    y_ref[0, 0] = yre.astype(BF16)
    y_ref[0, 1] = yim.astype(BF16)


def _hy_inv_kernel(ci_ref, si_ref, y_ref, gg_ref, x0_ref, skip_ref, o_ref):
    y = (jnp.dot(ci_ref[...], y_ref[0, 0], preferred_element_type=F32)
         + jnp.dot(si_ref[...], y_ref[0, 1], preferred_element_type=F32))
    o_ref[0] = ((y + gg_ref[0] * skip_ref[...]) * x0_ref[0]).astype(o_ref.dtype)


def _hy_long_conv_gate(gg, x0, kre, kim, skip, tables):
    bsz, length, d = gg.shape
    cf, sf, sft = tables
    tf = 256
    y = pl.pallas_call(
        _hy_fwd_kernel,
        grid=(bsz, length // tf),
        in_specs=[
            pl.BlockSpec((tf, length), lambda b, j: (j, 0)),
            pl.BlockSpec((tf, length), lambda b, j: (j, 0)),
            pl.BlockSpec((1, length, d), lambda b, j: (b, 0, 0)),
            pl.BlockSpec((tf, d), lambda b, j: (j, 0)),
            pl.BlockSpec((tf, d), lambda b, j: (j, 0)),
        ],
        out_specs=pl.BlockSpec((1, 2, tf, d), lambda b, j: (b, 0, j, 0)),
        out_shape=jax.ShapeDtypeStruct((bsz, 2, length, d), BF16),
        scratch_shapes=[pltpu.VMEM((length, d), BF16)],
        compiler_params=_cparams("parallel", "arbitrary"),
        name="hy_dft_fwd",
    )(cf, sf, gg, kre, kim)
    tt = 256
    return pl.pallas_call(
        _hy_inv_kernel,
        grid=(bsz, length // tt),
        in_specs=[
            pl.BlockSpec((tt, length), lambda b, i: (i, 0)),
            pl.BlockSpec((tt, length), lambda b, i: (i, 0)),
            pl.BlockSpec((1, 2, length, d), lambda b, i: (b, 0, 0, 0)),
            pl.BlockSpec((1, tt, d), lambda b, i: (b, i, 0)),
            pl.BlockSpec((1, tt, d), lambda b, i: (b, i, 0)),
            pl.BlockSpec((1, d), lambda b, i: (0, 0)),
        ],
        out_specs=pl.BlockSpec((1, tt, d), lambda b, i: (b, i, 0)),
        out_shape=jax.ShapeDtypeStruct((bsz, length, d), BF16),
        compiler_params=_cparams("parallel", "arbitrary"),
        name="hy_dft_inv",
    )(cf, sft, y, gg, x0, skip.reshape(1, d))


def _hyena_filter(length, w1, b1, w2, b2, w3, b3, sin_freq):
    t = jnp.linspace(0.0, 1.0, length, dtype=F32)[:, None]
    bands = (HY_EMB_DIM - 1) // 2
    w = (2.0 * math.pi / length) * jnp.arange(length, dtype=F32)[:, None]
    f = jnp.linspace(1e-4, bands - 1, bands, dtype=F32)[None, :]
    z = jnp.concatenate([t, jnp.cos(f * w), -jnp.sin(f * w)], axis=-1)
    hp = lax.Precision.HIGHEST
    h = jnp.sin(sin_freq[0] * (jnp.dot(z, w1, precision=hp) + b1))
    h = jnp.sin(sin_freq[1] * (jnp.dot(h, w2, precision=hp) + b2))
    h = jnp.dot(h, w3, precision=hp) + b3
    max_decay = math.log(HY_DECAY_TARGET) / HY_FAST_DECAY
    min_decay = math.log(HY_DECAY_TARGET) / HY_SLOW_DECAY
    deltas = jnp.abs(jnp.linspace(min_decay, max_decay, h.shape[-1] // 2, dtype=F32))
    deltas = jnp.tile(deltas, 2)
    return h * (jnp.exp(-t * deltas) + HY_MOD_SHIFT)


def _hy_filter_spectrum(filt, d, tables):
    length = filt.shape[0]
    cf, sf, _ = tables
    hf, hb = filt[:, :d], filt[:, d:]
    ka = hf
    kb = jnp.concatenate([jnp.zeros((1, d), F32), hb[1:][::-1]], axis=0)
    tab = jnp.concatenate([cf, sf], axis=0)[None]
    spec = _proj(tab, jnp.concatenate([ka, kb], axis=1))[0]
    sgn_f = jnp.where(jnp.arange(length) % 2 == 0, 1.0, -1.0).astype(F32)[:, None]
    wgt = jnp.where(jnp.arange(length) == 0, 1.0, 2.0).astype(F32)[:, None] / (2 * length)
    kre = (spec[:length, :d] + sgn_f * spec[:length, d:]) * wgt
    kim = (spec[length:, :d] + sgn_f * spec[length:, d:]) * wgt
    return kre, kim


def _hyena_mixer(x, ctx, g, sh, sc, csh, csc, w_in, conv_w, conv_b, f_w1, f_b1, f_w2, f_b2, f_w3, f_b3,
                 sin_freq, skip, w_o, gate, cgate):
    d = x.shape[-1]

    def one(u, shift, scale, res, gt):
        length = u.shape[1]
        z = _proj(u, w_in, g, shift, scale)
        x0, gg = _hy_conv(z, conv_w, conv_b, d)
        tables = _dft_tables(length)
        filt = _hyena_filter(length, f_w1, f_b1, f_w2, f_b2, f_w3, f_b3, sin_freq)
        kre, kim = _hy_filter_spectrum(filt, d, tables)
        y = _hy_long_conv_gate(gg, x0, kre, kim, skip, tables)
        return _proj_res(y, w_o, res, gt)

    return one(x, sh, sc, x, gate), (one(ctx, csh, csc, ctx, cgate) if cgate is not None else None)


def _ml_conv_kernel(z_ref, w_ref, b_ref, o_ref):
    y = _conv3(z_ref[0], w_ref, b_ref)
    o_ref[0] = y * jax.nn.sigmoid(y)


def _ml_conv(z, conv_w, conv_b, width):
    bsz, length, _ = z.shape
    tc = 256
    return pl.pallas_call(
        _ml_conv_kernel,
        grid=(bsz, width // tc),
        in_specs=[pl.BlockSpec((1, length, tc), lambda b, j: (b, 0, j)),
                  pl.BlockSpec((3, tc), lambda b, j: (0, j)),
                  pl.BlockSpec((1, tc), lambda b, j: (0, j))],
        out_specs=pl.BlockSpec((1, length, tc), lambda b, j: (b, 0, j)),
        out_shape=jax.ShapeDtypeStruct((bsz, length, width), F32),
        compiler_params=_cparams("parallel", "parallel"),
        name="ml_conv",
    )(z, conv_w, conv_b.reshape(1, width))


def _log_sigmoid(x):
    return jnp.minimum(x, 0.0) - jnp.log1p(jnp.exp(-jnp.abs(x)))


def _ml_chunk(q, kt, k, v, i_row, f_row, i_col, f_col, state, reverse):
    ct, n, m = state
    tlen = q.shape[0]
    ti = lax.broadcasted_iota(jnp.int32, (tlen, tlen), 0)
    si = lax.broadcasted_iota(jnp.int32, (tlen, tlen), 1)
    causal = (si >= ti) if reverse else (si <= ti)
    tri = causal.astype(F32)
    lf_row = _log_sigmoid(f_row)
    lf_col = _log_sigmoid(f_col)
    b_col = jnp.dot(tri, lf_col, precision=HIGHEST, preferred_element_type=F32)
    b_row = lax.dot_general(lf_row, tri, (((1,), (1,)), ((), ())), precision=HIGHEST,
                            preferred_element_type=F32)
    log_d = jnp.where(causal, b_col - b_row + i_row, -jnp.inf)
    m_inter = b_col + m
    m_t = jnp.maximum(log_d.max(axis=-1, keepdims=True), m_inter)
    qb = q.astype(BF16)
    s = jnp.dot(qb, kt.astype(BF16), preferred_element_type=F32) * jnp.exp(log_d - m_t)
    inter = jnp.exp(m_inter - m_t)
    num = (jnp.dot(s.astype(BF16), v.astype(BF16), preferred_element_type=F32)
           + inter * jnp.dot(qb, ct.astype(BF16), preferred_element_type=F32))
    qn = s.sum(axis=-1, keepdims=True) + inter * (q * n).sum(axis=-1, keepdims=True)
    h = num / jnp.maximum(jnp.abs(qn), jnp.exp(-m_t))
    b_end = b_row[:, 0:1] if reverse else b_row[:, tlen - 1:tlen]
    w_log = b_end - b_row + i_row
    m_new = jnp.maximum(b_end + m, w_log.max(axis=-1, keepdims=True))
    w = jnp.exp(w_log - m_new)
    decay = jnp.exp(b_end + m - m_new)
    ct_new = decay * ct + jnp.dot((kt * w).astype(BF16), v.astype(BF16), preferred_element_type=F32)
    n_new = decay * n + jnp.dot(w.astype(BF16), k.astype(BF16), preferred_element_type=F32)
    return h, (ct_new, n_new, m_new)


def _mlstm_kernel(ql_ref, kl_ref, ktl_ref, vl_ref, ol_ref, grl_ref, gcl_ref,
                  qc_ref, kc_ref, ktc_ref, vc_ref, oc_ref, grc_ref, gcc_ref, ng_ref,
                  yl_ref, yc_ref, hl_ref, hc_ref):
    tlen = ML_CHUNK
    dk, dv = ML_QK_DIM, ML_V_DIM
    n_lat = ql_ref.shape[1] // tlen
    n_ctx = qc_ref.shape[1] // tlen

    for j in range(2):
        qs = slice(j * dk, (j + 1) * dk)
        vs = slice(j * dv, (j + 1) * dv)

        def run(q_ref, k_ref, kt_ref, v_ref, gr_ref, gc_ref, h_ref, c, state, direction):
            t0 = pl.multiple_of(c * tlen, tlen)
            rows = pl.ds(t0, tlen)
            q = q_ref[0, rows, qs]
            k = k_ref[0, rows, qs]
            kt = kt_ref[0, j, :, rows]
            v = v_ref[0, rows, vs]
            gi, gf = 2 * direction, 2 * direction + 1
            i_row = gr_ref[0, j, gi:gi + 1, rows]
            f_row = gr_ref[0, j, gf:gf + 1, rows]
            i_col = gc_ref[0, j, rows, gi:gi + 1]
            f_col = gc_ref[0, j, rows, gf:gf + 1]
            h, state = _ml_chunk(q, kt, k, v, i_row, f_row, i_col, f_col, state, direction == 1)
            if direction == 0:
                h_ref[rows, vs] = h
            else:
                h_ref[rows, vs] = h_ref[rows, vs] + h
            return state

        lat = (ql_ref, kl_ref, ktl_ref, vl_ref, grl_ref, gcl_ref, hl_ref)
        ctx = (qc_ref, kc_ref, ktc_ref, vc_ref, grc_ref, gcc_ref, hc_ref)
        for direction in range(2):
            state = (jnp.zeros((dk, dv), F32), jnp.zeros((1, dk), F32), jnp.zeros((1, 1), F32))
            if direction == 0:
                state = lax.fori_loop(0, n_ctx, lambda c, st: run(*ctx, c, st, 0), state)
                lax.fori_loop(0, n_lat, lambda c, st: run(*lat, c, st, 0), state)
            else:
                state = lax.fori_loop(0, n_ctx, lambda c, st: run(*ctx, n_ctx - 1 - c, st, 1), state)
                lax.fori_loop(0, n_lat, lambda c, st: run(*lat, n_lat - 1 - c, st, 1), state)

    def finish(h_ref, o_ref, y_ref):
        for j in range(2):
            vs = slice(j * dv, (j + 1) * dv)
            h = h_ref[:, vs]
            h = h * lax.rsqrt(jnp.mean(h * h, axis=-1, keepdims=True) + NORM_EPS) * ng_ref[:, vs]
            y_ref[0, :, vs] = (h * jax.nn.sigmoid(o_ref[0, :, vs])).astype(y_ref.dtype)

    finish(hl_ref, ol_ref, yl_ref)
    finish(hc_ref, oc_ref, yc_ref)


def _mlstm_mixer(x, ctx, g, sh, sc, csh, csc, w_in, conv_w, conv_b, gate_b, out_norm_g, w_o, gate, cgate):
    nh, dk, dv = ML_HEADS, ML_QK_DIM, ML_V_DIM
    nqk = 2 * nh * dk
    wide = nqk + 2 * nh * dv
    w_main = w_in[:, :wide]
    w_gate = jnp.pad(w_in[:, wide:], ((0, 0), (0, LANES - 4 * nh)))

    def project(u, shift, scale):
        bsz, length, _ = u.shape
        z = _proj(u, w_main, g, shift, scale)
        gates = _proj(u, w_gate, g, shift, scale)[..., :4 * nh] + gate_b
        qk = _ml_conv(z, conv_w, conv_b, nqk)
        q = qk[..., :nh * dk]
        k = qk[..., nh * dk:] * (dk ** -0.5)
        kt = k.reshape(bsz, length, nh, dk).transpose(0, 2, 3, 1)
        g4 = gates.reshape(bsz, length, 4, nh)
        g_row = g4.transpose(0, 3, 2, 1)
        g_col = g4.transpose(0, 3, 1, 2)
        return z, q, k, kt, g_row, g_col

    zl, ql, kl, ktl, grl, gcl = project(x, sh, sc)
    zc, qc, kc, ktc, grc, gcc = project(ctx, csh, csc)
    bsz, length, d = x.shape
    lc = ctx.shape[1]
    npair = nh // 2
    qw, vw = 2 * dk, 2 * dv
    v_off, o_off = nqk // vw, (nqk + nh * dv) // vw

    def specs(n):
        return [
            pl.BlockSpec((1, n, qw), lambda b, h: (b, 0, h)),
            pl.BlockSpec((1, n, qw), lambda b, h: (b, 0, h)),
            pl.BlockSpec((1, 2, dk, n), lambda b, h: (b, h, 0, 0)),
            pl.BlockSpec((1, n, vw), lambda b, h: (b, 0, v_off + h)),
            pl.BlockSpec((1, n, vw), lambda b, h: (b, 0, o_off + h)),
            pl.BlockSpec((1, 2, 4, n), lambda b, h: (b, h, 0, 0)),
            pl.BlockSpec((1, 2, n, 4), lambda b, h: (b, h, 0, 0)),
        ]

    yl, yc = pl.pallas_call(
        _mlstm_kernel,
        grid=(bsz, npair),
        in_specs=specs(length) + specs(lc) + [pl.BlockSpec((1, vw), lambda b, h: (0, h))],
        out_specs=[pl.BlockSpec((1, length, vw), lambda b, h: (b, 0, h)),
                   pl.BlockSpec((1, lc, vw), lambda b, h: (b, 0, h))],
        out_shape=[jax.ShapeDtypeStruct((bsz, length, nh * dv), BF16),
                   jax.ShapeDtypeStruct((bsz, lc, nh * dv), BF16)],
        scratch_shapes=[pltpu.VMEM((length, vw), F32), pltpu.VMEM((lc, vw), F32)],
        compiler_params=_cparams("parallel", "parallel"),
        name="mlstm",
    )(ql, kl, ktl, zl, zl, grl, gcl, qc, kc, ktc, zc, zc, grc, gcc, out_norm_g.reshape(1, nh * dv))
    return _proj_res(yl, w_o, x, gate), (_proj_res(yc, w_o, ctx, cgate) if cgate is not None else None)


def _router_kernel(x_ref, g_ref, sh_ref, sc_ref, rw_ref, h_ref, aff_ref):
    x = x_ref[0]
    ms = jnp.mean(x * x, axis=-1, keepdims=True)
    h = x * lax.rsqrt(ms + NORM_EPS) * g_ref[...]
    h = h * (1.0 + sc_ref[0]) + sh_ref[0]
    h_ref[0] = h.astype(BF16)
    logits = lax.dot_general(rw_ref[...], h, (((1,), (1,)), ((), ())), precision=HIGHEST,
                             preferred_element_type=F32)
    mx = logits.max(axis=0, keepdims=True)
    p = jnp.exp(logits - mx)
    aff_ref[0] = p / p.sum(axis=0, keepdims=True)


def _router(x, g, shift, scale, router_w):
    bsz, length, d = x.shape
    ne = router_w.shape[1]
    tm = _pick(length, (1024, 256))
    return pl.pallas_call(
        _router_kernel,
        grid=(bsz, length // tm),
        in_specs=[
            pl.BlockSpec((1, tm, d), lambda b, i: (b, i, 0)),
            pl.BlockSpec((1, d), lambda b, i: (0, 0)),
            pl.BlockSpec((1, 1, d), lambda b, i: (b, 0, 0)),
            pl.BlockSpec((1, 1, d), lambda b, i: (b, 0, 0)),
            pl.BlockSpec((ne, d), lambda b, i: (0, 0)),
        ],
        out_specs=[pl.BlockSpec((1, tm, d), lambda b, i: (b, i, 0)),
                   pl.BlockSpec((1, ne, tm), lambda b, i: (b, 0, i))],
        out_shape=[jax.ShapeDtypeStruct((bsz, length, d), BF16),
                   jax.ShapeDtypeStruct((bsz, ne, length), F32)],
        compiler_params=_cparams("parallel", "parallel"),
        name="moe_router",
    )(x, g.reshape(1, d), shift.reshape(bsz, 1, d), scale.reshape(bsz, 1, d), router_w.T)


def _excl_cumsum_lanes(flags):
    rows, length = flags.shape
    cw = min(length, 256)
    si = lax.broadcasted_iota(jnp.int32, (cw, cw), 0)
    ti = lax.broadcasted_iota(jnp.int32, (cw, cw), 1)
    upper = jnp.where(si < ti, 1.0, 0.0).astype(BF16)
    carry = jnp.zeros((rows, 1), F32)
    parts = []
    for c in range(length // cw):
        blk = flags[:, c * cw:(c + 1) * cw]
        parts.append(jnp.dot(blk.astype(BF16), upper, preferred_element_type=F32) + carry)
        carry = carry + blk.sum(axis=-1, keepdims=True)
    return parts[0] if len(parts) == 1 else jnp.concatenate(parts, axis=-1)


def _select_kernel(aff_ref, pos_ref, *, cap):
    a = aff_ref[...]
    rows = a.shape[0]
    capf = float(cap)

    def bisect(_, lohi):
        lo, hi = lohi
        mid = 0.5 * (lo + hi)
        cnt = jnp.where(a >= mid, 1.0, 0.0).sum(axis=-1, keepdims=True)
        ge = cnt >= capf
        return jnp.where(ge, mid, lo), jnp.where(ge, hi, mid)

    lo, hi = lax.fori_loop(0, SELECT_BISECTIONS, bisect,
                           (jnp.zeros((rows, 1), F32), jnp.full((rows, 1), 2.0, F32)))
    above = a >= hi
    tie = jnp.logical_and(a >= lo, jnp.logical_not(above))
    n_above = jnp.where(above, 1.0, 0.0).sum(axis=-1, keepdims=True)
    tie_rank = _excl_cumsum_lanes(jnp.where(tie, 1.0, 0.0))
    sel = jnp.logical_or(above, jnp.logical_and(tie, tie_rank < capf - n_above))
    slot = _excl_cumsum_lanes(jnp.where(sel, 1.0, 0.0))
    pos_ref[...] = jnp.where(sel, slot, -1.0).astype(jnp.int32)


def _select(aff2d, cap):
    return pl.pallas_call(
        functools.partial(_select_kernel, cap=cap),
        out_shape=jax.ShapeDtypeStruct(aff2d.shape, jnp.int32),
        compiler_params=pltpu.CompilerParams(vmem_limit_bytes=V7X_VMEM_LIMIT_BYTES),
        name="moe_select",
    )(aff2d)


def _gather_kernel(h_ref, pos_ref, aff_ref, xg_ref, gs_ref, *, cap):
    pos = pos_ref[0, 0]
    length = pos.shape[1]
    slot = lax.broadcasted_iota(jnp.int32, (cap, length), 0)
    hit = slot == pos
    onehot = jnp.where(hit, 1.0, 0.0).astype(BF16)
    xg_ref[0] = jnp.dot(onehot, h_ref[0], preferred_element_type=F32).astype(BF16)
    gs_ref[0] = jnp.where(hit, aff_ref[0, 0], 0.0).sum(axis=-1, keepdims=True)


def _gather(h, pos, aff, cap):
    bsz, length, d = h.shape
    ne = pos.shape[1]
    return pl.pallas_call(
        functools.partial(_gather_kernel, cap=cap),
        grid=(bsz, ne),
        in_specs=[
            pl.BlockSpec((1, length, d), lambda b, e: (b, 0, 0)),
            pl.BlockSpec((1, 1, 1, length), lambda b, e: (b, e, 0, 0)),
            pl.BlockSpec((1, 1, 1, length), lambda b, e: (b, e, 0, 0)),
        ],
        out_specs=[pl.BlockSpec((1, cap, d), lambda b, e: (e, b, 0)),
                   pl.BlockSpec((1, cap, 1), lambda b, e: (e, b, 0))],
        out_shape=[jax.ShapeDtypeStruct((ne, bsz * cap, d), BF16),
                   jax.ShapeDtypeStruct((ne, bsz * cap, 1), F32)],
        compiler_params=_cparams("parallel", "arbitrary"),
        name="moe_gather",
    )(h, pos.reshape(bsz, ne, 1, length), aff.reshape(bsz, ne, 1, length))


def _ffn_kernel(xg_ref, gs_ref, wg_ref, wu_ref, wd_ref, y_ref, acc_ref, *, row_chunk):
    f = pl.program_id(1)
    nf = pl.num_programs(1)
    wg = wg_ref[0].astype(BF16)
    wu = wu_ref[0].astype(BF16)
    wd = wd_ref[0].astype(BF16)
    m = xg_ref.shape[1]

    @pl.when(f == 0)
    def _():
        acc_ref[...] = jnp.zeros_like(acc_ref)

    for r in range(m // row_chunk):
        rows = slice(r * row_chunk, (r + 1) * row_chunk)
        xc = xg_ref[0, rows, :]
        a = jnp.dot(xc, wg, preferred_element_type=F32)
        u = jnp.dot(xc, wu, preferred_element_type=F32)
        z = (a * jax.nn.sigmoid(a) * u).astype(BF16)
        acc_ref[rows, :] += jnp.dot(z, wd, preferred_element_type=F32)

    @pl.when(f == nf - 1)
    def _():
        y_ref[0] = (acc_ref[...] * gs_ref[0]).astype(BF16)


def _expert_ffn(xg, gs, w_gate, w_up, w_down):
    ne, m, d = xg.shape
    ff = w_gate.shape[2]
    tf = 256
    return pl.pallas_call(
        functools.partial(_ffn_kernel, row_chunk=256),
        grid=(ne, ff // tf),
        in_specs=[
            pl.BlockSpec((1, m, d), lambda e, f: (e, 0, 0)),
            pl.BlockSpec((1, m, 1), lambda e, f: (e, 0, 0)),
            pl.BlockSpec((1, d, tf), lambda e, f: (e, 0, f)),
            pl.BlockSpec((1, d, tf), lambda e, f: (e, 0, f)),
            pl.BlockSpec((1, tf, d), lambda e, f: (e, f, 0)),
        ],
        out_specs=pl.BlockSpec((1, m, d), lambda e, f: (e, 0, 0)),
        out_shape=jax.ShapeDtypeStruct((ne, m, d), BF16),
        scratch_shapes=[pltpu.VMEM((m, d), F32)],
        compiler_params=_cparams("parallel", "arbitrary"),
        name="moe_ffn",
    )(xg, gs, w_gate, w_up, w_down)


def _combine_kernel(post_ref, y_ref, x_ref, gt_ref, o_ref, *, cap):
    post = post_ref[0]
    tl, ne = post.shape
    slot = lax.broadcasted_iota(jnp.int32, (tl, cap), 1)
    acc = jnp.zeros(o_ref.shape[1:], F32)
    for e in range(ne):
        onehot = jnp.where(post[:, e:e + 1] == slot, 1.0, 0.0).astype(BF16)
        acc = acc + jnp.dot(onehot, y_ref[e], preferred_element_type=F32)
    o_ref[0] = x_ref[0] + gt_ref[0] * acc


def _combine(pos_t, y, x, gate, cap, row0):
    bsz, length, d = x.shape
    ne = y.shape[0]
    tl = _pick(length, (512, 256))
    blk0 = row0 // cap
    return pl.pallas_call(
        functools.partial(_combine_kernel, cap=cap),
        grid=(bsz, length // tl),
        in_specs=[
            pl.BlockSpec((1, tl, ne), lambda b, i: (b, i, 0)),
            pl.BlockSpec((ne, cap, d), lambda b, i: (0, blk0 + b, 0)),
            pl.BlockSpec((1, tl, d), lambda b, i: (b, i, 0)),
            pl.BlockSpec((1, 1, d), lambda b, i: (b, 0, 0)),
        ],
        out_specs=pl.BlockSpec((1, tl, d), lambda b, i: (b, i, 0)),
        out_shape=jax.ShapeDtypeStruct(x.shape, F32),
        compiler_params=_cparams("parallel", "arbitrary"),
        name="moe_combine",
    )(pos_t, y, x, gate.reshape(bsz, 1, d))


def _route(x, g, shift, scale, router_w):
    bsz, length, _ = x.shape
    ne = router_w.shape[1]
    cap = max(1, EC_CAPACITY_FACTOR * length // ne)
    h, aff = _router(x, g, shift, scale, router_w)
    pos = _select(aff.reshape(bsz * ne, length), cap).reshape(bsz, ne, length)
    xg, gs = _gather(h, pos, aff, cap)
    return xg, gs, pos.transpose(0, 2, 1), cap


def _moe(x, ctx, g, sh, sc, gate, csh, csc, cgate, router_w, w_gate, w_up, w_down):
    xg, gs, pos_t, cap = _route(x, g, sh, sc, router_w)
    if ctx is None:
        y = _expert_ffn(xg, gs, w_gate, w_up, w_down)
        return _combine(pos_t, y, x, gate, cap, 0), None
    xg_c, gs_c, pos_tc, cap_c = _route(ctx, g, csh, csc, router_w)
    m_lat = xg.shape[1]
    y = _expert_ffn(jnp.concatenate([xg, xg_c], axis=1), jnp.concatenate([gs, gs_c], axis=1),
                    w_gate, w_up, w_down)
    return (_combine(pos_t, y, x, gate, cap, 0),
            _combine(pos_tc, y, ctx, cgate, cap_c, m_lat))


def kernel(x, c, ctx, c_ctx, mod_w, mod_b, norm_mix_g, norm_ffn_g, router_w, moe_w_gate, moe_w_up, moe_w_down, na_w_qkv, na_rpb, na_w_o, mla_w_in, mla_q_norm_g, mla_w_q_b, mla_kv_norm_g, mla_w_kv_b, mla_w_o, hy_w_in, hy_conv_w, hy_conv_b, hy_f_w1, hy_f_b1, hy_f_w2, hy_f_b2, hy_f_w3, hy_f_b3, hy_sin_freq, hy_skip, hy_w_o, ml_w_in, ml_conv_w, ml_conv_b, ml_gate_b, ml_out_norm_g, ml_w_o, final_norm_g):
    bsz, _, d = x.shape
    depth = mod_w.shape[0]
    n_mixers = 4
    cond = jnp.concatenate([c, c_ctx[None, :]], axis=0)
    cond = jnp.pad(cond, ((0, (-cond.shape[0]) % 8), (0, 0)))
    for i in range(depth):
        last = i == depth - 1
        mod = _modulation(cond, mod_w[i], mod_b[i])
        sh1, sc1, g1, sh2, sc2, g2 = [mod[:bsz, k * d:(k + 1) * d] for k in range(6)]
        bc = lambda v: jnp.broadcast_to(v[None, :], (bsz, d))
        csh1, csc1, cg1, csh2, csc2, cg2 = [bc(mod[bsz, k * d:(k + 1) * d]) for k in range(6)]
        if last:
            cg1 = None
        kind, j = i % n_mixers, i // n_mixers
        gm = norm_mix_g[i]
        if kind == 0:
            x, ctx_new = _na_mixer(x, ctx, gm, sh1, sc1, csh1, csc1, na_w_qkv[j], na_rpb[j], na_w_o[j], g1, cg1)
        elif kind == 1:
            x, ctx_new = _mla_mixer(x, ctx, gm, sh1, sc1, csh1, csc1, mla_w_in[j], mla_q_norm_g[j],
                                    mla_w_q_b[j], mla_kv_norm_g[j], mla_w_kv_b[j], mla_w_o[j], g1, cg1)
        elif kind == 2:
            x, ctx_new = _hyena_mixer(x, ctx, gm, sh1, sc1, csh1, csc1, hy_w_in[j], hy_conv_w[j], hy_conv_b[j],
                                      hy_f_w1[j], hy_f_b1[j], hy_f_w2[j], hy_f_b2[j], hy_f_w3[j], hy_f_b3[j],
                                      hy_sin_freq[j], hy_skip[j], hy_w_o[j], g1, cg1)
        else:
            x, ctx_new = _mlstm_mixer(x, ctx, gm, sh1, sc1, csh1, csc1, ml_w_in[j], ml_conv_w[j], ml_conv_b[j],
                                      ml_gate_b[j], ml_out_norm_g[j], ml_w_o[j], g1, cg1)
        if last:
            x, _ = _moe(x, None, norm_ffn_g[i], sh2, sc2, g2, None, None, None,
                        router_w[i], moe_w_gate[i], moe_w_up[i], moe_w_down[i])
        else:
            x, ctx = _moe(x, ctx_new, norm_ffn_g[i], sh2, sc2, g2, csh2, csc2, cg2,
                          router_w[i], moe_w_gate[i], moe_w_up[i], moe_w_down[i])
    return _final_norm(x, final_norm_g)
```

```python
import functools
import math

import numpy as np
import jax
import jax.numpy as jnp
from jax import lax
from jax.experimental import pallas as pl
from jax.experimental.pallas import tpu as pltpu

F32 = jnp.float32
BF16 = jnp.bfloat16
HIGHEST = lax.Precision.HIGHEST

V7X_VMEM_LIMIT_BYTES = 56 * 1024 * 1024
LANES = 128

GRID_W = 64
NORM_EPS = 1e-6
ROPE_BASE = 10000.0

NA_HEADS = 16
NA_HEAD_DIM = 64
NA_WIN_H = 8
NA_WIN_W = 16
NA_QROWS = 4
NA_KROWS = NA_QROWS - 1 + NA_WIN_H

MLA_HEADS = 16
MLA_Q_RANK = 384
MLA_KV_RANK = 256
MLA_NOPE_DIM = 64
MLA_ROPE_DIM = 32
MLA_V_DIM = 64

HY_EMB_DIM = 33
HY_DECAY_TARGET = 1e-2
HY_FAST_DECAY = 0.3
HY_SLOW_DECAY = 1.5
HY_MOD_SHIFT = 0.05

ML_HEADS = 8
ML_V_DIM = 128
ML_QK_DIM = 64
ML_CHUNK = 256

N_EXPERTS = 16
EC_CAPACITY_FACTOR = 2
SELECT_BISECTIONS = 64


def _cparams(*sem):
    return pltpu.CompilerParams(dimension_semantics=sem, vmem_limit_bytes=V7X_VMEM_LIMIT_BYTES)


def _pick(n, prefs):
    for p in prefs:
        if n % p == 0:
            return p
    return n


def _proj_kernel(x_ref, g_ref, sh_ref, sc_ref, w_ref, o_ref, h_ref, *, norm):
    @pl.when(pl.program_id(2) == 0)
    def _():
        x = x_ref[0].astype(F32)
        if norm:
            ms = jnp.mean(x * x, axis=-1, keepdims=True)
            x = x * lax.rsqrt(ms + NORM_EPS) * g_ref[...]
            x = x * (1.0 + sc_ref[0]) + sh_ref[0]
        h_ref[...] = x.astype(BF16)

    o_ref[0] = jnp.dot(h_ref[...], w_ref[...].astype(BF16),
                       preferred_element_type=F32).astype(o_ref.dtype)


def _proj(x, w, g=None, shift=None, scale=None, out_dtype=F32):
    bsz, length, kdim = x.shape
    n = w.shape[1]
    norm = g is not None
    if not norm:
        g = jnp.ones((kdim,), F32)
    if shift is None:
        shift = jnp.zeros((bsz, kdim), F32)
        scale = jnp.zeros((bsz, kdim), F32)
    tm = _pick(length, (1024, 512, 256))
    tn = _pick(n, (512, 384, 256, 128))
    return pl.pallas_call(
        functools.partial(_proj_kernel, norm=norm),
        grid=(bsz, length // tm, n // tn),
        in_specs=[
            pl.BlockSpec((1, tm, kdim), lambda b, i, j: (b, i, 0)),
            pl.BlockSpec((1, kdim), lambda b, i, j: (0, 0)),
            pl.BlockSpec((1, 1, kdim), lambda b, i, j: (b, 0, 0)),
            pl.BlockSpec((1, 1, kdim), lambda b, i, j: (b, 0, 0)),
            pl.BlockSpec((kdim, tn), lambda b, i, j: (0, j)),
        ],
        out_specs=pl.BlockSpec((1, tm, tn), lambda b, i, j: (b, i, j)),
        out_shape=jax.ShapeDtypeStruct((bsz, length, n), out_dtype),
        scratch_shapes=[pltpu.VMEM((tm, kdim), BF16)],
        compiler_params=_cparams("parallel", "parallel", "arbitrary"),
        name="proj",
    )(x, g.reshape(1, kdim).astype(F32), shift.reshape(bsz, 1, kdim), scale.reshape(bsz, 1, kdim), w)


def _proj_res_kernel(y_ref, w_ref, r_ref, gt_ref, o_ref):
    acc = jnp.dot(y_ref[0].astype(BF16), w_ref[...].astype(BF16), preferred_element_type=F32)
    o_ref[0] = r_ref[0] + gt_ref[0] * acc


def _proj_res(y, w, res, gate):
    bsz, length, kdim = y.shape
    n = w.shape[1]
    tm = _pick(length, (1024, 512, 256))
    tn = _pick(n, (512, 256, 128))
    return pl.pallas_call(
        _proj_res_kernel,
        grid=(bsz, length // tm, n // tn),
        in_specs=[
            pl.BlockSpec((1, tm, kdim), lambda b, i, j: (b, i, 0)),
            pl.BlockSpec((kdim, tn), lambda b, i, j: (0, j)),
            pl.BlockSpec((1, tm, tn), lambda b, i, j: (b, i, j)),
            pl.BlockSpec((1, 1, tn), lambda b, i, j: (b, 0, j)),
        ],
        out_specs=pl.BlockSpec((1, tm, tn), lambda b, i, j: (b, i, j)),
        out_shape=jax.ShapeDtypeStruct((bsz, length, n), F32),
        compiler_params=_cparams("parallel", "parallel", "arbitrary"),
        name="proj_res",
    )(y, w, res, gate.reshape(bsz, 1, n))


def _mod_kernel(c_ref, w_ref, b_ref, o_ref):
    c = c_ref[...]
    s = c * jax.nn.sigmoid(c)
    o_ref[...] = jnp.dot(s.astype(BF16), w_ref[...].astype(BF16), preferred_element_type=F32) + b_ref[...]


def _modulation(cc, w, b):
    rows, d = cc.shape
    n = w.shape[1]
    tn = 512
    return pl.pallas_call(
        _mod_kernel,
        grid=(n // tn,),
        in_specs=[
            pl.BlockSpec((rows, d), lambda j: (0, 0)),
            pl.BlockSpec((d, tn), lambda j: (0, j)),
            pl.BlockSpec((1, tn), lambda j: (0, j)),
        ],
        out_specs=pl.BlockSpec((rows, tn), lambda j: (0, j)),
        out_shape=jax.ShapeDtypeStruct((rows, n), F32),
        compiler_params=_cparams("parallel"),
        name="modulation",
    )(cc, w, b.reshape(1, n))


def _final_norm_kernel(x_ref, g_ref, o_ref):
    x = x_ref[0]
    ms = jnp.mean(x * x, axis=-1, keepdims=True)
    o_ref[0] = x * lax.rsqrt(ms + NORM_EPS) * g_ref[...]


def _final_norm(x, g):
    bsz, length, d = x.shape
    tm = 1024
    return pl.pallas_call(
        _final_norm_kernel,
        grid=(bsz, length // tm),
        in_specs=[pl.BlockSpec((1, tm, d), lambda b, i: (b, i, 0)),
                  pl.BlockSpec((1, d), lambda b, i: (0, 0))],
        out_specs=pl.BlockSpec((1, tm, d), lambda b, i: (b, i, 0)),
        out_shape=jax.ShapeDtypeStruct(x.shape, F32),
        compiler_params=_cparams("parallel", "parallel"),
        name="final_norm",
    )(x, g.reshape(1, d))


def _softmax_pv(s_parts, v_parts):
    m = s_parts[0].max(axis=-1, keepdims=True)
    for s in s_parts[1:]:
        m = jnp.maximum(m, s.max(axis=-1, keepdims=True))
    den = None
    acc = None
    for s, v in zip(s_parts, v_parts):
        p = jnp.exp(s - m)
        l = p.sum(axis=-1, keepdims=True)
        o = jnp.dot(p.astype(BF16), v, preferred_element_type=F32)
        den = l if den is None else den + l
        acc = o if acc is None else acc + o
    return acc / den


HEAD_V = 64


def _pair_queries(q2, hw):
    if hw == LANES:
        return [q2[:, :LANES], q2[:, LANES:]]
    lane = lax.broadcasted_iota(jnp.int32, (1, LANES), 1)
    return [jnp.where(lane < hw, q2, jnp.zeros_like(q2)), jnp.where(lane >= hw, q2, jnp.zeros_like(q2))]


def _pair_keys(k_ref, hw, j, rows=None):
    rows = slice(None) if rows is None else rows
    return k_ref[0, rows, j * LANES:(j + 1) * LANES] if hw == LANES else k_ref[0, rows, :]


def _pair_select(o0, o1):
    lane = lax.broadcasted_iota(jnp.int32, (1, LANES), 1)
    return jnp.where(lane < HEAD_V, o0, o1)


def _pair_attn_kernel(*refs, scale, nparts, hw):
    q_ref, o_ref = refs[0], refs[-1]
    k_refs = refs[1:1 + nparts]
    v_refs = refs[1 + nparts:1 + 2 * nparts]
    nt = (((1,), (1,)), ((), ()))
    outs = []
    for j, q in enumerate(_pair_queries(q_ref[0], hw)):
        s_parts = [lax.dot_general(q, _pair_keys(k_ref, hw, j), nt, preferred_element_type=F32) * scale
                   for k_ref in k_refs]
        outs.append(_softmax_pv(s_parts, [v_ref[0] for v_ref in v_refs]))
    o_ref[0] = _pair_select(*outs).astype(o_ref.dtype)


def _pair_attention(q_src, kv_srcs, scale, hw, nh):
    q, q_off = q_src
    bsz, lq = q.shape[0], q.shape[1]
    tq = _pick(lq, (256,))
    qw = 2 * hw
    in_specs = [pl.BlockSpec((1, tq, qw), lambda b, h, i: (b, i, q_off // qw + h))]
    args = [q]
    for k_arr, k_off, _, _ in kv_srcs:
        in_specs.append(pl.BlockSpec((1, k_arr.shape[1], qw), lambda b, h, i, o=k_off // qw: (b, 0, o + h)))
        args.append(k_arr)
    for _, _, v_arr, v_off in kv_srcs:
        in_specs.append(pl.BlockSpec((1, v_arr.shape[1], LANES), lambda b, h, i, o=v_off // LANES: (b, 0, o + h)))
        args.append(v_arr)
    return pl.pallas_call(
        functools.partial(_pair_attn_kernel, scale=scale, nparts=len(kv_srcs), hw=hw),
        grid=(bsz, nh // 2, lq // tq),
        in_specs=in_specs,
        out_specs=pl.BlockSpec((1, tq, LANES), lambda b, h, i: (b, i, h)),
        out_shape=jax.ShapeDtypeStruct((bsz, lq, nh * HEAD_V), BF16),
        compiler_params=_cparams("parallel", "parallel", "arbitrary"),
        name="attention",
    )(*args)


def _na_bias_table(rpb, rows):
    nh, n_dr, n_dc = rpb.shape
    kh = min(NA_WIN_H, rows)
    nblk = rows // NA_QROWS
    w = GRID_W
    qc = np.arange(w)[:, None]
    kc = np.arange(w)[None, :]
    cs = np.clip(qc - NA_WIN_W // 2, 0, w - NA_WIN_W)
    col_ok = (kc >= cs) & (kc < cs + NA_WIN_W)
    dc = np.clip(kc - qc, -(NA_WIN_W - 1), NA_WIN_W - 1) + NA_WIN_W - 1
    pick_dc = (dc.reshape(1, w * w) == np.arange(n_dc)[:, None]).astype(np.float32)
    tile = jnp.dot(rpb.reshape(nh * n_dr, n_dc).astype(F32), jnp.asarray(pick_dc), precision=HIGHEST)
    tile = jnp.where(col_ok, tile.reshape(nh, n_dr, w, w), -1e30)
    masked = jnp.full((nh, w, w), -1e30, F32)
    tables = []
    for blk in (0, 1, nblk - 1):
        start = int(np.clip(blk * NA_QROWS - NA_WIN_H // 2, 0, rows - NA_KROWS))
        q_rows = []
        for qr in range(NA_QROWS):
            r = blk * NA_QROWS + qr
            rs = int(np.clip(r - kh // 2, 0, rows - kh))
            tiles = [tile[:, kr - r + NA_WIN_H - 1] if rs <= kr < rs + kh else masked
                     for kr in range(start, start + NA_KROWS)]
            q_rows.append(jnp.concatenate(tiles, axis=-1))
        tables.append(jnp.concatenate(q_rows, axis=1))
    return jnp.stack(tables, axis=1)


def _na_kernel(q_ref, kl_ref, vl_ref, kc_ref, vc_ref, bias_ref, o_ref, *, scale, nblk, rows):
    i = pl.program_id(2)
    start = jnp.clip(i * NA_QROWS - NA_WIN_H // 2, 0, rows - NA_KROWS) * GRID_W
    start = pl.multiple_of(start, GRID_W)
    pat = jnp.where(i == 0, 0, jnp.where(i == nblk - 1, 2, 1))
    local = pl.ds(start, NA_KROWS * GRID_W)
    nt = (((1,), (1,)), ((), ()))
    hw = NA_HEAD_DIM
    outs = []
    for j, q in enumerate(_pair_queries(q_ref[0], hw)):
        s_ctx = lax.dot_general(q, _pair_keys(kc_ref, hw, j), nt, preferred_element_type=F32) * scale
        s_loc = (lax.dot_general(q, _pair_keys(kl_ref, hw, j, local), nt, preferred_element_type=F32) * scale
                 + bias_ref[j, pat])
        outs.append(_softmax_pv([s_ctx, s_loc], [vc_ref[0], vl_ref[0, local, :]]))
    o_ref[0] = _pair_select(*outs).astype(o_ref.dtype)


def _na_mixer(x, ctx, g, sh, sc, csh, csc, w_qkv, rpb, w_o, gate, cgate):
    bsz, length, d = x.shape
    lc = ctx.shape[1]
    rows = length // GRID_W
    nblk = rows // NA_QROWS
    nh, hd = NA_HEADS, NA_HEAD_DIM
    scale = hd ** -0.5
    qkv = _proj(x, w_qkv, g, sh, sc, out_dtype=BF16)
    qkv_c = _proj(ctx, w_qkv, g, csh, csc, out_dtype=BF16)
    bias = _na_bias_table(rpb, rows)
    tq = NA_QROWS * GRID_W
    npair = nh // 2
    pw = 2 * hd
    y = pl.pallas_call(
        functools.partial(_na_kernel, scale=scale, nblk=nblk, rows=rows),
        grid=(npair, bsz, nblk),
        in_specs=[
            pl.BlockSpec((1, tq, pw), lambda h, b, i: (b, i, h)),
            pl.BlockSpec((1, length, pw), lambda h, b, i: (b, 0, npair + h)),
            pl.BlockSpec((1, length, pw), lambda h, b, i: (b, 0, 2 * npair + h)),
            pl.BlockSpec((1, lc, pw), lambda h, b, i: (b, 0, npair + h)),
            pl.BlockSpec((1, lc, pw), lambda h, b, i: (b, 0, 2 * npair + h)),
            pl.BlockSpec((2, 3, tq, NA_KROWS * GRID_W), lambda h, b, i: (h, 0, 0, 0)),
        ],
        out_specs=pl.BlockSpec((1, tq, pw), lambda h, b, i: (b, i, h)),
        out_shape=jax.ShapeDtypeStruct((bsz, length, d), BF16),
        compiler_params=_cparams("parallel", "parallel", "arbitrary"),
        name="na_local",
    )(qkv, qkv, qkv, qkv_c, qkv_c, bias)
    if cgate is None:
        return _proj_res(y, w_o, x, gate), None
    yc = _pair_attention((qkv_c, 0), [(qkv_c, d, qkv_c, 2 * d)], scale, hd, nh)
    return _proj_res(y, w_o, x, gate), _proj_res(yc, w_o, ctx, cgate)


def _axial_rope(length):
    t = jnp.arange(length)
    row = (t // GRID_W).astype(F32)
    col = (t % GRID_W).astype(F32)
    n_freq = MLA_ROPE_DIM // 4
    inv = ROPE_BASE ** (-jnp.arange(n_freq, dtype=F32) / n_freq)
    ang = jnp.concatenate([row[:, None] * inv, col[:, None] * inv], axis=-1)
    return jnp.cos(ang), jnp.sin(ang)


def _rot_half_cols(w):
    half = w.shape[-1] // 2
    return jnp.concatenate([-w[..., half:], w[..., :half]], axis=-1)


def _mla_weights(w_in, w_q_b, w_kv_b):
    nh, nope, rope = MLA_HEADS, MLA_NOPE_DIM, MLA_ROPE_DIM
    off = MLA_Q_RANK + MLA_KV_RANK
    w_in_ext = jnp.concatenate([w_in, _rot_half_cols(w_in[:, off:])], axis=1)
    rq, rkv = w_q_b.shape[0], w_kv_b.shape[0]
    wq = w_q_b.reshape(rq, nh, nope + rope)
    pad = jnp.zeros((rq, nh, LANES - nope - rope), F32)
    wq_pad = jnp.concatenate([wq, pad], axis=-1).reshape(rq, nh * LANES)
    wq_rot = jnp.concatenate([jnp.zeros((rq, nh, nope), F32), _rot_half_cols(wq[..., nope:]), pad],
                             axis=-1).reshape(rq, nh * LANES)
    wkv = w_kv_b.reshape(rkv, nh, nope + MLA_V_DIM)
    wk_pad = jnp.concatenate([wkv[..., :nope], jnp.zeros((rkv, nh, LANES - nope), F32)],
                             axis=-1).reshape(rkv, nh * LANES)
    wv = wkv[..., nope:].reshape(rkv, nh * MLA_V_DIM)
    place = np.zeros((rope, nh, LANES), np.float32)
    place[np.arange(rope), :, nope + np.arange(rope)] = 1.0
    place = jnp.asarray(place.reshape(rope, nh * LANES))
    return (w_in_ext,) + tuple(a.astype(BF16) for a in (wq_pad, wq_rot, wk_pad, wv, place))


def _mla_rope_tables(length, rope_on):
    if rope_on:
        cos, sin = _axial_rope(length)
    else:
        cos = jnp.ones((length, MLA_ROPE_DIM // 2), F32)
        sin = jnp.zeros((length, MLA_ROPE_DIM // 2), F32)
    ck = jnp.concatenate([cos, cos], axis=-1)
    sk = jnp.concatenate([sin, sin], axis=-1)
    one = jnp.ones((length, MLA_NOPE_DIM), F32)
    tail = LANES - MLA_NOPE_DIM - MLA_ROPE_DIM
    cq = jnp.concatenate([one, ck, one[:, :tail]], axis=-1)
    sq = jnp.concatenate([0.0 * one, sk, 0.0 * one[:, :tail]], axis=-1)
    return cq, sq, ck, sk


def _mla_qkv_kernel(z_ref, gq_ref, gkv_ref, wq_ref, wqr_ref, wk_ref, wv_ref, place_ref,
                    cq_ref, sq_ref, ck_ref, sk_ref, q_ref, k_ref, v_ref):
    z = z_ref[0]
    off = MLA_Q_RANK + MLA_KV_RANK
    rope = MLA_ROPE_DIM

    def norm(a, g_ref):
        ms = jnp.mean(a * a, axis=-1, keepdims=True)
        return (a * lax.rsqrt(ms + NORM_EPS) * g_ref[...]).astype(BF16)

    cq = norm(z[:, :MLA_Q_RANK], gq_ref)
    ckv = norm(z[:, MLA_Q_RANK:off], gkv_ref)
    k_rope = (z[:, off:off + rope] * ck_ref[...] + z[:, off + rope:off + 2 * rope] * sk_ref[...]).astype(BF16)
    qa = jnp.dot(cq, wq_ref[...], preferred_element_type=F32)
    qb = jnp.dot(cq, wqr_ref[...], preferred_element_type=F32)
    cos_q = cq_ref[...]
    sin_q = sq_ref[...]
    for h in range(MLA_HEADS):
        cols = slice(h * LANES, (h + 1) * LANES)
        q_ref[0, :, cols] = (qa[:, cols] * cos_q + qb[:, cols] * sin_q).astype(BF16)
    k_ref[0] = (jnp.dot(ckv, wk_ref[...], preferred_element_type=F32)
                + jnp.dot(k_rope, place_ref[...], preferred_element_type=F32)).astype(BF16)
    v_ref[0] = jnp.dot(ckv, wv_ref[...], preferred_element_type=F32).astype(BF16)


def _mla_qkv(z, q_norm_g, kv_norm_g, weights, tables):
    bsz, length, zw = z.shape
    wq, wqr, wk, wv, place = weights
    nh = MLA_HEADS
    tm = _pick(length, (512, 256))
    full = lambda a: pl.BlockSpec(a.shape, lambda b, i: (0, 0))
    rows = lambda a: pl.BlockSpec((tm, a.shape[1]), lambda b, i: (i, 0))
    gq = q_norm_g.reshape(1, -1)
    gkv = kv_norm_g.reshape(1, -1)
    out = lambda w: pl.BlockSpec((1, tm, w), lambda b, i: (b, i, 0))
    return pl.pallas_call(
        _mla_qkv_kernel,
        grid=(bsz, length // tm),
        in_specs=[pl.BlockSpec((1, tm, zw), lambda b, i: (b, i, 0)), full(gq), full(gkv),
                  full(wq), full(wqr), full(wk), full(wv), full(place)] + [rows(t) for t in tables],
        out_specs=[out(nh * LANES), out(nh * LANES), out(nh * MLA_V_DIM)],
        out_shape=[jax.ShapeDtypeStruct((bsz, length, nh * LANES), BF16),
                   jax.ShapeDtypeStruct((bsz, length, nh * LANES), BF16),
                   jax.ShapeDtypeStruct((bsz, length, nh * MLA_V_DIM), BF16)],
        compiler_params=_cparams("parallel", "parallel"),
        name="mla_qkv",
    )(z, gq, gkv, wq, wqr, wk, wv, place, *tables)


def _mla_mixer(x, ctx, g, sh, sc, csh, csc, w_in, q_norm_g, w_q_b, kv_norm_g, w_kv_b, w_o, gate, cgate):
    nh = MLA_HEADS
    scale = (MLA_NOPE_DIM + MLA_ROPE_DIM) ** -0.5
    w_in_ext, *weights = _mla_weights(w_in, w_q_b, w_kv_b)

    def project(u, shift, scl, rope_on):
        z = _proj(u, w_in_ext, g, shift, scl)
        return _mla_qkv(z, q_norm_g, kv_norm_g, weights, _mla_rope_tables(u.shape[1], rope_on))

    ql, kl, vl = project(x, sh, sc, True)
    qc, kc, vc = project(ctx, csh, csc, False)
    y = _pair_attention((ql, 0), [(kc, 0, vc, 0), (kl, 0, vl, 0)], scale, LANES, nh)
    if cgate is None:
        return _proj_res(y, w_o, x, gate), None
    yc = _pair_attention((qc, 0), [(kc, 0, vc, 0)], scale, LANES, nh)
    return _proj_res(y, w_o, x, gate), _proj_res(yc, w_o, ctx, cgate)


def _conv3(z, w_ref, b_ref):
    length = z.shape[0]
    t = lax.broadcasted_iota(jnp.int32, z.shape, 0)
    prev = jnp.where(t == 0, 0.0, pltpu.roll(z, 1, axis=0))
    nxt = jnp.where(t == length - 1, 0.0, pltpu.roll(z, length - 1, axis=0))
    return b_ref[...] + prev * w_ref[0:1, :] + z * w_ref[1:2, :] + nxt * w_ref[2:3, :]


def _hy_conv_kernel(z0_ref, z1_ref, z2_ref, w0_ref, w1_ref, w2_ref, b0_ref, b1_ref, b2_ref,
                    x0_ref, gg_ref):
    x0_ref[0] = _conv3(z0_ref[0], w0_ref, b0_ref)
    x1 = _conv3(z1_ref[0], w1_ref, b1_ref)
    v = _conv3(z2_ref[0], w2_ref, b2_ref)
    gg_ref[0] = v * x1


def _hy_conv(z, conv_w, conv_b, d):
    bsz, length, _ = z.shape
    tc = 256
    nb = d // tc
    zspec = lambda k: pl.BlockSpec((1, length, tc), lambda b, j: (b, 0, k * nb + j))
    wspec = lambda k: pl.BlockSpec((3, tc), lambda b, j: (0, k * nb + j))
    bspec = lambda k: pl.BlockSpec((1, tc), lambda b, j: (0, k * nb + j))
    ospec = pl.BlockSpec((1, length, tc), lambda b, j: (b, 0, j))
    cb = conv_b.reshape(1, 3 * d)
    return pl.pallas_call(
        _hy_conv_kernel,
        grid=(bsz, nb),
        in_specs=[zspec(0), zspec(1), zspec(2), wspec(0), wspec(1), wspec(2), bspec(0), bspec(1), bspec(2)],
        out_specs=[ospec, ospec],
        out_shape=[jax.ShapeDtypeStruct((bsz, length, d), F32)] * 2,
        compiler_params=_cparams("parallel", "parallel"),
        name="hy_conv",
    )(z, z, z, conv_w, conv_w, conv_w, cb, cb, cb)


def _dft_tables(length):
    n2 = 2 * length
    f = jnp.arange(length, dtype=jnp.int32)[:, None]
    n = jnp.arange(length, dtype=jnp.int32)[None, :]
    ang = ((f * n) % n2).astype(F32) * (2.0 * math.pi / n2)
    cf = jnp.cos(ang)
    sgn_n = jnp.where(n % 2 == 0, 1.0, -1.0).astype(F32)
    sf = jnp.where(f == 0, sgn_n, -jnp.sin(ang))
    return cf.astype(BF16), sf.astype(BF16), sf.T.astype(BF16)


def _hy_fwd_kernel(cf_ref, sf_ref, gg_ref, kre_ref, kim_ref, y_ref, gb_ref):
    j = pl.program_id(1)

    @pl.when(j == 0)
    def _():
        gb_ref[...] = gg_ref[0].astype(BF16)

    ure = jnp.dot(cf_ref[...], gb_ref[...], preferred_element_type=F32)
    uim = jnp.dot(sf_ref[...], gb_ref[...], preferred_element_type=F32)
    kre = kre_ref[...]
    kim = kim_ref[...]
    tf = ure.shape[0]
    row = lax.broadcasted_iota(jnp.int32, ure.shape, 0) + j * tf
    packed = row == 0
    yre = ure * kre - jnp.where(packed, 0.0, uim * kim)
    yim = uim * jnp.where(packed, kim, kre) + jnp.where(packed, 0.0, ure * kim)
    y_ref[0, 0] = yre.astype(BF16)
    y_ref[0, 1] = yim.astype(BF16)


def _hy_inv_kernel(ci_ref, si_ref, y_ref, gg_ref, x0_ref, skip_ref, o_ref):
    y = (jnp.dot(ci_ref[...], y_ref[0, 0], preferred_element_type=F32)
         + jnp.dot(si_ref[...], y_ref[0, 1], preferred_element_type=F32))
    o_ref[0] = ((y + gg_ref[0] * skip_ref[...]) * x0_ref[0]).astype(o_ref.dtype)


def _hy_long_conv_gate(gg, x0, kre, kim, skip, tables):
    bsz, length, d = gg.shape
    cf, sf, sft = tables
    tf = 256
    y = pl.pallas_call(
        _hy_fwd_kernel,
        grid=(bsz, length // tf),
        in_specs=[
            pl.BlockSpec((tf, length), lambda b, j: (j, 0)),
            pl.BlockSpec((tf, length), lambda b, j: (j, 0)),
            pl.BlockSpec((1, length, d), lambda b, j: (b, 0, 0)),
            pl.BlockSpec((tf, d), lambda b, j: (j, 0)),
            pl.BlockSpec((tf, d), lambda b, j: (j, 0)),
        ],
        out_specs=pl.BlockSpec((1, 2, tf, d), lambda b, j: (b, 0, j, 0)),
        out_shape=jax.ShapeDtypeStruct((bsz, 2, length, d), BF16),
        scratch_shapes=[pltpu.VMEM((length, d), BF16)],
        compiler_params=_cparams("parallel", "arbitrary"),
        name="hy_dft_fwd",
    )(cf, sf, gg, kre, kim)
    tt = 256
    return pl.pallas_call(
        _hy_inv_kernel,
        grid=(bsz, length // tt),
        in_specs=[
            pl.BlockSpec((tt, length), lambda b, i: (i, 0)),
            pl.BlockSpec((tt, length), lambda b, i: (i, 0)),
            pl.BlockSpec((1, 2, length, d), lambda b, i: (b, 0, 0, 0)),
            pl.BlockSpec((1, tt, d), lambda b, i: (b, i, 0)),
            pl.BlockSpec((1, tt, d), lambda b, i: (b, i, 0)),
            pl.BlockSpec((1, d), lambda b, i: (0, 0)),
        ],
        out_specs=pl.BlockSpec((1, tt, d), lambda b, i: (b, i, 0)),
        out_shape=jax.ShapeDtypeStruct((bsz, length, d), BF16),
        compiler_params=_cparams("parallel", "arbitrary"),
        name="hy_dft_inv",
    )(cf, sft, y, gg, x0, skip.reshape(1, d))


def _hyena_filter(length, w1, b1, w2, b2, w3, b3, sin_freq):
    t = jnp.linspace(0.0, 1.0, length, dtype=F32)[:, None]
    bands = (HY_EMB_DIM - 1) // 2
    w = (2.0 * math.pi / length) * jnp.arange(length, dtype=F32)[:, None]
    f = jnp.linspace(1e-4, bands - 1, bands, dtype=F32)[None, :]
    z = jnp.concatenate([t, jnp.cos(f * w), -jnp.sin(f * w)], axis=-1)
    hp = lax.Precision.HIGHEST
    h = jnp.sin(sin_freq[0] * (jnp.dot(z, w1, precision=hp) + b1))
    h = jnp.sin(sin_freq[1] * (jnp.dot(h, w2, precision=hp) + b2))
    h = jnp.dot(h, w3, precision=hp) + b3
    max_decay = math.log(HY_DECAY_TARGET) / HY_FAST_DECAY
    min_decay = math.log(HY_DECAY_TARGET) / HY_SLOW_DECAY
    deltas = jnp.abs(jnp.linspace(min_decay, max_decay, h.shape[-1] // 2, dtype=F32))
    deltas = jnp.tile(deltas, 2)
    return h * (jnp.exp(-t * deltas) + HY_MOD_SHIFT)


def _hy_filter_spectrum(filt, d, tables):
    length = filt.shape[0]
    cf, sf, _ = tables
    hf, hb = filt[:, :d], filt[:, d:]
    ka = hf
    kb = jnp.concatenate([jnp.zeros((1, d), F32), hb[1:][::-1]], axis=0)
    tab = jnp.concatenate([cf, sf], axis=0)[None]
    spec = _proj(tab, jnp.concatenate([ka, kb], axis=1))[0]
    sgn_f = jnp.where(jnp.arange(length) % 2 == 0, 1.0, -1.0).astype(F32)[:, None]
    wgt = jnp.where(jnp.arange(length) == 0, 1.0, 2.0).astype(F32)[:, None] / (2 * length)
    kre = (spec[:length, :d] + sgn_f * spec[:length, d:]) * wgt
    kim = (spec[length:, :d] + sgn_f * spec[length:, d:]) * wgt
    return kre, kim


def _hyena_mixer(x, ctx, g, sh, sc, csh, csc, w_in, conv_w, conv_b, f_w1, f_b1, f_w2, f_b2, f_w3, f_b3,
                 sin_freq, skip, w_o, gate, cgate):
    d = x.shape[-1]

    def one(u, shift, scale, res, gt):
        length = u.shape[1]
        z = _proj(u, w_in, g, shift, scale)
        x0, gg = _hy_conv(z, conv_w, conv_b, d)
        tables = _dft_tables(length)
        filt = _hyena_filter(length, f_w1, f_b1, f_w2, f_b2, f_w3, f_b3, sin_freq)
        kre, kim = _hy_filter_spectrum(filt, d, tables)
        y = _hy_long_conv_gate(gg, x0, kre, kim, skip, tables)
        return _proj_res(y, w_o, res, gt)

    return one(x, sh, sc, x, gate), (one(ctx, csh, csc, ctx, cgate) if cgate is not None else None)


def _ml_conv_kernel(z_ref, w_ref, b_ref, o_ref):
    y = _conv3(z_ref[0], w_ref, b_ref)
    o_ref[0] = y * jax.nn.sigmoid(y)


def _ml_conv(z, conv_w, conv_b, width):
    bsz, length, _ = z.shape
    tc = 256
    return pl.pallas_call(
        _ml_conv_kernel,
        grid=(bsz, width // tc),
        in_specs=[pl.BlockSpec((1, length, tc), lambda b, j: (b, 0, j)),
                  pl.BlockSpec((3, tc), lambda b, j: (0, j)),
                  pl.BlockSpec((1, tc), lambda b, j: (0, j))],
        out_specs=pl.BlockSpec((1, length, tc), lambda b, j: (b, 0, j)),
        out_shape=jax.ShapeDtypeStruct((bsz, length, width), F32),
        compiler_params=_cparams("parallel", "parallel"),
        name="ml_conv",
    )(z, conv_w, conv_b.reshape(1, width))


def _log_sigmoid(x):
    return jnp.minimum(x, 0.0) - jnp.log1p(jnp.exp(-jnp.abs(x)))


def _ml_chunk(q, kt, k, v, i_row, f_row, i_col, f_col, state, reverse):
    ct, n, m = state
    tlen = q.shape[0]
    ti = lax.broadcasted_iota(jnp.int32, (tlen, tlen), 0)
    si = lax.broadcasted_iota(jnp.int32, (tlen, tlen), 1)
    causal = (si >= ti) if reverse else (si <= ti)
    tri = causal.astype(F32)
    lf_row = _log_sigmoid(f_row)
    lf_col = _log_sigmoid(f_col)
    b_col = jnp.dot(tri, lf_col, precision=HIGHEST, preferred_element_type=F32)
    b_row = lax.dot_general(lf_row, tri, (((1,), (1,)), ((), ())), precision=HIGHEST,
                            preferred_element_type=F32)
    log_d = jnp.where(causal, b_col - b_row + i_row, -jnp.inf)
    m_inter = b_col + m
    m_t = jnp.maximum(log_d.max(axis=-1, keepdims=True), m_inter)
    qb = q.astype(BF16)
    s = jnp.dot(qb, kt.astype(BF16), preferred_element_type=F32) * jnp.exp(log_d - m_t)
    inter = jnp.exp(m_inter - m_t)
    num = (jnp.dot(s.astype(BF16), v.astype(BF16), preferred_element_type=F32)
           + inter * jnp.dot(qb, ct.astype(BF16), preferred_element_type=F32))
    qn = s.sum(axis=-1, keepdims=True) + inter * (q * n).sum(axis=-1, keepdims=True)
    h = num / jnp.maximum(jnp.abs(qn), jnp.exp(-m_t))
    b_end = b_row[:, 0:1] if reverse else b_row[:, tlen - 1:tlen]
    w_log = b_end - b_row + i_row
    m_new = jnp.maximum(b_end + m, w_log.max(axis=-1, keepdims=True))
    w = jnp.exp(w_log - m_new)
    decay = jnp.exp(b_end + m - m_new)
    ct_new = decay * ct + jnp.dot((kt * w).astype(BF16), v.astype(BF16), preferred_element_type=F32)
    n_new = decay * n + jnp.dot(w.astype(BF16), k.astype(BF16), preferred_element_type=F32)
    return h, (ct_new, n_new, m_new)


def _mlstm_kernel(ql_ref, kl_ref, ktl_ref, vl_ref, ol_ref, grl_ref, gcl_ref,
                  qc_ref, kc_ref, ktc_ref, vc_ref, oc_ref, grc_ref, gcc_ref, ng_ref,
                  yl_ref, yc_ref, hl_ref, hc_ref):
    tlen = ML_CHUNK
    dk, dv = ML_QK_DIM, ML_V_DIM
    n_lat = ql_ref.shape[1] // tlen
    n_ctx = qc_ref.shape[1] // tlen

    for j in range(2):
        qs = slice(j * dk, (j + 1) * dk)
        vs = slice(j * dv, (j + 1) * dv)

        def run(q_ref, k_ref, kt_ref, v_ref, gr_ref, gc_ref, h_ref, c, state, direction):
            t0 = pl.multiple_of(c * tlen, tlen)
            rows = pl.ds(t0, tlen)
            q = q_ref[0, rows, qs]
            k = k_ref[0, rows, qs]
            kt = kt_ref[0, j, :, rows]
            v = v_ref[0, rows, vs]
            gi, gf = 2 * direction, 2 * direction + 1
            i_row = gr_ref[0, j, gi:gi + 1, rows]
            f_row = gr_ref[0, j, gf:gf + 1, rows]
            i_col = gc_ref[0, j, rows, gi:gi + 1]
            f_col = gc_ref[0, j, rows, gf:gf + 1]
            h, state = _ml_chunk(q, kt, k, v, i_row, f_row, i_col, f_col, state, direction == 1)
            if direction == 0:
                h_ref[rows, vs] = h
            else:
                h_ref[rows, vs] = h_ref[rows, vs] + h
            return state

        lat = (ql_ref, kl_ref, ktl_ref, vl_ref, grl_ref, gcl_ref, hl_ref)
        ctx = (qc_ref, kc_ref, ktc_ref, vc_ref, grc_ref, gcc_ref, hc_ref)
        for direction in range(2):
            state = (jnp.zeros((dk, dv), F32), jnp.zeros((1, dk), F32), jnp.zeros((1, 1), F32))
            if direction == 0:
                state = lax.fori_loop(0, n_ctx, lambda c, st: run(*ctx, c, st, 0), state)
                lax.fori_loop(0, n_lat, lambda c, st: run(*lat, c, st, 0), state)
            else:
                state = lax.fori_loop(0, n_ctx, lambda c, st: run(*ctx, n_ctx - 1 - c, st, 1), state)
                lax.fori_loop(0, n_lat, lambda c, st: run(*lat, n_lat - 1 - c, st, 1), state)

    def finish(h_ref, o_ref, y_ref):
        for j in range(2):
            vs = slice(j * dv, (j + 1) * dv)
            h = h_ref[:, vs]
            h = h * lax.rsqrt(jnp.mean(h * h, axis=-1, keepdims=True) + NORM_EPS) * ng_ref[:, vs]
            y_ref[0, :, vs] = (h * jax.nn.sigmoid(o_ref[0, :, vs])).astype(y_ref.dtype)

    finish(hl_ref, ol_ref, yl_ref)
    finish(hc_ref, oc_ref, yc_ref)


def _mlstm_mixer(x, ctx, g, sh, sc, csh, csc, w_in, conv_w, conv_b, gate_b, out_norm_g, w_o, gate, cgate):
    nh, dk, dv = ML_HEADS, ML_QK_DIM, ML_V_DIM
    nqk = 2 * nh * dk
    wide = nqk + 2 * nh * dv
    w_main = w_in[:, :wide]
    w_gate = jnp.pad(w_in[:, wide:], ((0, 0), (0, LANES - 4 * nh)))

    def project(u, shift, scale):
        bsz, length, _ = u.shape
        z = _proj(u, w_main, g, shift, scale)
        gates = _proj(u, w_gate, g, shift, scale)[..., :4 * nh] + gate_b
        qk = _ml_conv(z, conv_w, conv_b, nqk)
        q = qk[..., :nh * dk]
        k = qk[..., nh * dk:] * (dk ** -0.5)
        kt = k.reshape(bsz, length, nh, dk).transpose(0, 2, 3, 1)
        g4 = gates.reshape(bsz, length, 4, nh)
        g_row = g4.transpose(0, 3, 2, 1)
        g_col = g4.transpose(0, 3, 1, 2)
        return z, q, k, kt, g_row, g_col

    zl, ql, kl, ktl, grl, gcl = project(x, sh, sc)
    zc, qc, kc, ktc, grc, gcc = project(ctx, csh, csc)
    bsz, length, d = x.shape
    lc = ctx.shape[1]
    npair = nh // 2
    qw, vw = 2 * dk, 2 * dv
    v_off, o_off = nqk // vw, (nqk + nh * dv) // vw

    def specs(n):
        return [
            pl.BlockSpec((1, n, qw), lambda b, h: (b, 0, h)),
            pl.BlockSpec((1, n, qw), lambda b, h: (b, 0, h)),
            pl.BlockSpec((1, 2, dk, n), lambda b, h: (b, h, 0, 0)),
            pl.BlockSpec((1, n, vw), lambda b, h: (b, 0, v_off + h)),
            pl.BlockSpec((1, n, vw), lambda b, h: (b, 0, o_off + h)),
            pl.BlockSpec((1, 2, 4, n), lambda b, h: (b, h, 0, 0)),
            pl.BlockSpec((1, 2, n, 4), lambda b, h: (b, h, 0, 0)),
        ]

    yl, yc = pl.pallas_call(
        _mlstm_kernel,
        grid=(bsz, npair),
        in_specs=specs(length) + specs(lc) + [pl.BlockSpec((1, vw), lambda b, h: (0, h))],
        out_specs=[pl.BlockSpec((1, length, vw), lambda b, h: (b, 0, h)),
                   pl.BlockSpec((1, lc, vw), lambda b, h: (b, 0, h))],
        out_shape=[jax.ShapeDtypeStruct((bsz, length, nh * dv), BF16),
                   jax.ShapeDtypeStruct((bsz, lc, nh * dv), BF16)],
        scratch_shapes=[pltpu.VMEM((length, vw), F32), pltpu.VMEM((lc, vw), F32)],
        compiler_params=_cparams("parallel", "parallel"),
        name="mlstm",
    )(ql, kl, ktl, zl, zl, grl, gcl, qc, kc, ktc, zc, zc, grc, gcc, out_norm_g.reshape(1, nh * dv))
    return _proj_res(yl, w_o, x, gate), (_proj_res(yc, w_o, ctx, cgate) if cgate is not None else None)


def _router_kernel(x_ref, g_ref, sh_ref, sc_ref, rw_ref, h_ref, aff_ref):
    x = x_ref[0]
    ms = jnp.mean(x * x, axis=-1, keepdims=True)
    h = x * lax.rsqrt(ms + NORM_EPS) * g_ref[...]
    h = h * (1.0 + sc_ref[0]) + sh_ref[0]
    h_ref[0] = h.astype(BF16)
    logits = lax.dot_general(rw_ref[...], h, (((1,), (1,)), ((), ())), precision=HIGHEST,
                             preferred_element_type=F32)
    mx = logits.max(axis=0, keepdims=True)
    p = jnp.exp(logits - mx)
    aff_ref[0] = p / p.sum(axis=0, keepdims=True)


def _router(x, g, shift, scale, router_w):
    bsz, length, d = x.shape
    ne = router_w.shape[1]
    tm = _pick(length, (1024, 256))
    return pl.pallas_call(
        _router_kernel,
        grid=(bsz, length // tm),
        in_specs=[
            pl.BlockSpec((1, tm, d), lambda b, i: (b, i, 0)),
            pl.BlockSpec((1, d), lambda b, i: (0, 0)),
            pl.BlockSpec((1, 1, d), lambda b, i: (b, 0, 0)),
            pl.BlockSpec((1, 1, d), lambda b, i: (b, 0, 0)),
            pl.BlockSpec((ne, d), lambda b, i: (0, 0)),
        ],
        out_specs=[pl.BlockSpec((1, tm, d), lambda b, i: (b, i, 0)),
                   pl.BlockSpec((1, ne, tm), lambda b, i: (b, 0, i))],
        out_shape=[jax.ShapeDtypeStruct((bsz, length, d), BF16),
                   jax.ShapeDtypeStruct((bsz, ne, length), F32)],
        compiler_params=_cparams("parallel", "parallel"),
        name="moe_router",
    )(x, g.reshape(1, d), shift.reshape(bsz, 1, d), scale.reshape(bsz, 1, d), router_w.T)


def _excl_cumsum_lanes(flags):
    rows, length = flags.shape
    cw = min(length, 256)
    si = lax.broadcasted_iota(jnp.int32, (cw, cw), 0)
    ti = lax.broadcasted_iota(jnp.int32, (cw, cw), 1)
    upper = jnp.where(si < ti, 1.0, 0.0).astype(BF16)
    carry = jnp.zeros((rows, 1), F32)
    parts = []
    for c in range(length // cw):
        blk = flags[:, c * cw:(c + 1) * cw]
        parts.append(jnp.dot(blk.astype(BF16), upper, preferred_element_type=F32) + carry)
        carry = carry + blk.sum(axis=-1, keepdims=True)
    return parts[0] if len(parts) == 1 else jnp.concatenate(parts, axis=-1)


def _select_kernel(aff_ref, pos_ref, *, cap):
    a = aff_ref[...]
    rows = a.shape[0]
    capf = float(cap)

    def bisect(_, lohi):
        lo, hi = lohi
        mid = 0.5 * (lo + hi)
        cnt = jnp.where(a >= mid, 1.0, 0.0).sum(axis=-1, keepdims=True)
        ge = cnt >= capf
        return jnp.where(ge, mid, lo), jnp.where(ge, hi, mid)

    lo, hi = lax.fori_loop(0, SELECT_BISECTIONS, bisect,
                           (jnp.zeros((rows, 1), F32), jnp.full((rows, 1), 2.0, F32)))
    above = a >= hi
    tie = jnp.logical_and(a >= lo, jnp.logical_not(above))
    n_above = jnp.where(above, 1.0, 0.0).sum(axis=-1, keepdims=True)
    tie_rank = _excl_cumsum_lanes(jnp.where(tie, 1.0, 0.0))
    sel = jnp.logical_or(above, jnp.logical_and(tie, tie_rank < capf - n_above))
    slot = _excl_cumsum_lanes(jnp.where(sel, 1.0, 0.0))
    pos_ref[...] = jnp.where(sel, slot, -1.0).astype(jnp.int32)


def _select(aff2d, cap):
    return pl.pallas_call(
        functools.partial(_select_kernel, cap=cap),
        out_shape=jax.ShapeDtypeStruct(aff2d.shape, jnp.int32),
        compiler_params=pltpu.CompilerParams(vmem_limit_bytes=V7X_VMEM_LIMIT_BYTES),
        name="moe_select",
    )(aff2d)


def _gather_kernel(h_ref, pos_ref, aff_ref, xg_ref, gs_ref, *, cap):
    pos = pos_ref[0, 0]
    length = pos.shape[1]
    slot = lax.broadcasted_iota(jnp.int32, (cap, length), 0)
    hit = slot == pos
    onehot = jnp.where(hit, 1.0, 0.0).astype(BF16)
    xg_ref[0] = jnp.dot(onehot, h_ref[0], preferred_element_type=F32).astype(BF16)
    gs_ref[0] = jnp.where(hit, aff_ref[0, 0], 0.0).sum(axis=-1, keepdims=True)


def _gather(h, pos, aff, cap):
    bsz, length, d = h.shape
    ne = pos.shape[1]
    return pl.pallas_call(
        functools.partial(_gather_kernel, cap=cap),
        grid=(bsz, ne),
        in_specs=[
            pl.BlockSpec((1, length, d), lambda b, e: (b, 0, 0)),
            pl.BlockSpec((1, 1, 1, length), lambda b, e: (b, e, 0, 0)),
            pl.BlockSpec((1, 1, 1, length), lambda b, e: (b, e, 0, 0)),
        ],
        out_specs=[pl.BlockSpec((1, cap, d), lambda b, e: (e, b, 0)),
                   pl.BlockSpec((1, cap, 1), lambda b, e: (e, b, 0))],
        out_shape=[jax.ShapeDtypeStruct((ne, bsz * cap, d), BF16),
                   jax.ShapeDtypeStruct((ne, bsz * cap, 1), F32)],
        compiler_params=_cparams("parallel", "arbitrary"),
        name="moe_gather",
    )(h, pos.reshape(bsz, ne, 1, length), aff.reshape(bsz, ne, 1, length))


def _ffn_kernel(*refs, n_groups):
    xg_refs = refs[:n_groups]
    gs_refs = refs[n_groups:2 * n_groups]
    wg_ref, wu_ref, wd_ref = refs[2 * n_groups:2 * n_groups + 3]
    y_refs = refs[2 * n_groups + 3:3 * n_groups + 3]
    acc_refs = refs[3 * n_groups + 3:]
    f = pl.program_id(1)
    nf = pl.num_programs(1)
    wg = wg_ref[0].astype(BF16)
    wu = wu_ref[0].astype(BF16)
    wd = wd_ref[0].astype(BF16)
    for xg_ref, gs_ref, y_ref, acc_ref in zip(xg_refs, gs_refs, y_refs, acc_refs):
        @pl.when(f == 0)
        def _():
            acc_ref[...] = jnp.zeros_like(acc_ref)

        xg = xg_ref[0]
        a = jnp.dot(xg, wg, preferred_element_type=F32)
        u = jnp.dot(xg, wu, preferred_element_type=F32)
        z = (a * jax.nn.sigmoid(a) * u).astype(BF16)
        acc_ref[...] += jnp.dot(z, wd, preferred_element_type=F32)

        @pl.when(f == nf - 1)
        def _():
            y_ref[0] = (acc_ref[...] * gs_ref[0]).astype(BF16)


def _expert_ffn(xgs, gss, w_gate, w_up, w_down):
    ne, _, d = xgs[0].shape
    ff = w_gate.shape[2]
    tf = 256
    n = len(xgs)
    tok = lambda a: pl.BlockSpec((1,) + a.shape[1:], lambda e, f: (e, 0, 0))
    return pl.pallas_call(
        functools.partial(_ffn_kernel, n_groups=n),
        grid=(ne, ff // tf),
        in_specs=[tok(a) for a in xgs] + [tok(a) for a in gss] + [
            pl.BlockSpec((1, d, tf), lambda e, f: (e, 0, f)),
            pl.BlockSpec((1, d, tf), lambda e, f: (e, 0, f)),
            pl.BlockSpec((1, tf, d), lambda e, f: (e, f, 0)),
        ],
        out_specs=[tok(a) for a in xgs],
        out_shape=[jax.ShapeDtypeStruct(a.shape, BF16) for a in xgs],
        scratch_shapes=[pltpu.VMEM(a.shape[1:], F32) for a in xgs],
        compiler_params=_cparams("parallel", "arbitrary"),
        name="moe_ffn",
    )(*xgs, *gss, w_gate, w_up, w_down)


def _combine_kernel(post_ref, y_ref, x_ref, gt_ref, o_ref, *, cap):
    post = post_ref[0]
    tl, ne = post.shape
    slot = lax.broadcasted_iota(jnp.int32, (tl, cap), 1)
    acc = jnp.zeros(o_ref.shape[1:], F32)
    for e in range(ne):
        onehot = jnp.where(post[:, e:e + 1] == slot, 1.0, 0.0).astype(BF16)
        acc = acc + jnp.dot(onehot, y_ref[e], preferred_element_type=F32)
    o_ref[0] = x_ref[0] + gt_ref[0] * acc


def _combine(pos_t, y, x, gate, cap):
    bsz, length, d = x.shape
    ne = y.shape[0]
    tl = _pick(length, (512, 256))
    return pl.pallas_call(
        functools.partial(_combine_kernel, cap=cap),
        grid=(bsz, length // tl),
        in_specs=[
            pl.BlockSpec((1, tl, ne), lambda b, i: (b, i, 0)),
            pl.BlockSpec((ne, cap, d), lambda b, i: (0, b, 0)),
            pl.BlockSpec((1, tl, d), lambda b, i: (b, i, 0)),
            pl.BlockSpec((1, 1, d), lambda b, i: (b, 0, 0)),
        ],
        out_specs=pl.BlockSpec((1, tl, d), lambda b, i: (b, i, 0)),
        out_shape=jax.ShapeDtypeStruct(x.shape, F32),
        compiler_params=_cparams("parallel", "arbitrary"),
        name="moe_combine",
    )(pos_t, y, x, gate.reshape(bsz, 1, d))


def _route(x, g, shift, scale, router_w):
    bsz, length, _ = x.shape
    ne = router_w.shape[1]
    cap = max(1, EC_CAPACITY_FACTOR * length // ne)
    h, aff = _router(x, g, shift, scale, router_w)
    pos = _select(aff.reshape(bsz * ne, length), cap).reshape(bsz, ne, length)
    xg, gs = _gather(h, pos, aff, cap)
    return xg, gs, pos.transpose(0, 2, 1), cap


def _moe(x, ctx, g, sh, sc, gate, csh, csc, cgate, router_w, w_gate, w_up, w_down):
    xg, gs, pos_t, cap = _route(x, g, sh, sc, router_w)
    if ctx is None:
        (y,) = _expert_ffn([xg], [gs], w_gate, w_up, w_down)
        return _combine(pos_t, y, x, gate, cap), None
    xg_c, gs_c, pos_tc, cap_c = _route(ctx, g, csh, csc, router_w)
    y, y_c = _expert_ffn([xg, xg_c], [gs, gs_c], w_gate, w_up, w_down)
    return _combine(pos_t, y, x, gate, cap), _combine(pos_tc, y_c, ctx, cgate, cap_c)


def kernel(x, c, ctx, c_ctx, mod_w, mod_b, norm_mix_g, norm_ffn_g, router_w, moe_w_gate, moe_w_up, moe_w_down, na_w_qkv, na_rpb, na_w_o, mla_w_in, mla_q_norm_g, mla_w_q_b, mla_kv_norm_g, mla_w_kv_b, mla_w_o, hy_w_in, hy_conv_w, hy_conv_b, hy_f_w1, hy_f_b1, hy_f_w2, hy_f_b2, hy_f_w3, hy_f_b3, hy_sin_freq, hy_skip, hy_w_o, ml_w_in, ml_conv_w, ml_conv_b, ml_gate_b, ml_out_norm_g, ml_w_o, final_norm_g):
    bsz, _, d = x.shape
    depth = mod_w.shape[0]
    n_mixers = 4
    cond = jnp.concatenate([c, c_ctx[None, :]], axis=0)
    cond = jnp.pad(cond, ((0, (-cond.shape[0]) % 8), (0, 0)))
    for i in range(depth):
        last = i == depth - 1
        mod = _modulation(cond, mod_w[i], mod_b[i])
        sh1, sc1, g1, sh2, sc2, g2 = [mod[:bsz, k * d:(k + 1) * d] for k in range(6)]
        bc = lambda v: jnp.broadcast_to(v[None, :], (bsz, d))
        csh1, csc1, cg1, csh2, csc2, cg2 = [bc(mod[bsz, k * d:(k + 1) * d]) for k in range(6)]
        if last:
            cg1 = None
        kind, j = i % n_mixers, i // n_mixers
        gm = norm_mix_g[i]
        if kind == 0:
            x, ctx_new = _na_mixer(x, ctx, gm, sh1, sc1, csh1, csc1, na_w_qkv[j], na_rpb[j], na_w_o[j], g1, cg1)
        elif kind == 1:
            x, ctx_new = _mla_mixer(x, ctx, gm, sh1, sc1, csh1, csc1, mla_w_in[j], mla_q_norm_g[j],
                                    mla_w_q_b[j], mla_kv_norm_g[j], mla_w_kv_b[j], mla_w_o[j], g1, cg1)
        elif kind == 2:
            x, ctx_new = _hyena_mixer(x, ctx, gm, sh1, sc1, csh1, csc1, hy_w_in[j], hy_conv_w[j], hy_conv_b[j],
                                      hy_f_w1[j], hy_f_b1[j], hy_f_w2[j], hy_f_b2[j], hy_f_w3[j], hy_f_b3[j],
                                      hy_sin_freq[j], hy_skip[j], hy_w_o[j], g1, cg1)
        else:
            x, ctx_new = _mlstm_mixer(x, ctx, gm, sh1, sc1, csh1, csc1, ml_w_in[j], ml_conv_w[j], ml_conv_b[j],
                                      ml_gate_b[j], ml_out_norm_g[j], ml_w_o[j], g1, cg1)
        if last:
            x, _ = _moe(x, None, norm_ffn_g[i], sh2, sc2, g2, None, None, None,
                        router_w[i], moe_w_gate[i], moe_w_up[i], moe_w_down[i])
        else:
            x, ctx = _moe(x, ctx_new, norm_ffn_g[i], sh2, sc2, g2, csh2, csc2, cg2,
                          router_w[i], moe_w_gate[i], moe_w_up[i], moe_w_down[i])
    return _final_norm(x, final_norm_g)
```

```python
import functools
import math

import numpy as np
import jax
import jax.numpy as jnp
from jax import lax
from jax.experimental import pallas as pl
from jax.experimental.pallas import tpu as pltpu

F32 = jnp.float32
BF16 = jnp.bfloat16
HIGHEST = lax.Precision.HIGHEST

V7X_VMEM_LIMIT_BYTES = 56 * 1024 * 1024
LANES = 128

GRID_W = 64
NORM_EPS = 1e-6
ROPE_BASE = 10000.0

NA_HEADS = 16
NA_HEAD_DIM = 64
NA_WIN_H = 8
NA_WIN_W = 16
NA_QROWS = 4
NA_KROWS = NA_QROWS - 1 + NA_WIN_H

MLA_HEADS = 16
MLA_Q_RANK = 384
MLA_KV_RANK = 256
MLA_NOPE_DIM = 64
MLA_ROPE_DIM = 32
MLA_V_DIM = 64

HY_EMB_DIM = 33
HY_DECAY_TARGET = 1e-2
HY_FAST_DECAY = 0.3
HY_SLOW_DECAY = 1.5
HY_MOD_SHIFT = 0.05

ML_HEADS = 8
ML_V_DIM = 128
ML_QK_DIM = 64
ML_CHUNK = 256

N_EXPERTS = 16
EC_CAPACITY_FACTOR = 2
SELECT_BISECTIONS = 64


def _cparams(*sem):
    return pltpu.CompilerParams(dimension_semantics=sem, vmem_limit_bytes=V7X_VMEM_LIMIT_BYTES)


def _pick(n, prefs):
    for p in prefs:
        if n % p == 0:
            return p
    return n


def _proj_kernel(x_ref, g_ref, sh_ref, sc_ref, w_ref, o_ref, h_ref, *, norm):
    @pl.when(pl.program_id(2) == 0)
    def _():
        x = x_ref[0].astype(F32)
        if norm:
            ms = jnp.mean(x * x, axis=-1, keepdims=True)
            x = x * lax.rsqrt(ms + NORM_EPS) * g_ref[...]
            x = x * (1.0 + sc_ref[0]) + sh_ref[0]
        h_ref[...] = x.astype(BF16)

    o_ref[0] = jnp.dot(h_ref[...], w_ref[...].astype(BF16),
                       preferred_element_type=F32).astype(o_ref.dtype)


def _proj(x, w, g=None, shift=None, scale=None, out_dtype=F32):
    bsz, length, kdim = x.shape
    n = w.shape[1]
    norm = g is not None
    if not norm:
        g = jnp.ones((kdim,), F32)
    if shift is None:
        shift = jnp.zeros((bsz, kdim), F32)
        scale = jnp.zeros((bsz, kdim), F32)
    tm = _pick(length, (1024, 512, 256))
    tn = _pick(n, (512, 384, 256, 128))
    return pl.pallas_call(
        functools.partial(_proj_kernel, norm=norm),
        grid=(bsz, length // tm, n // tn),
        in_specs=[
            pl.BlockSpec((1, tm, kdim), lambda b, i, j: (b, i, 0)),
            pl.BlockSpec((1, kdim), lambda b, i, j: (0, 0)),
            pl.BlockSpec((1, 1, kdim), lambda b, i, j: (b, 0, 0)),
            pl.BlockSpec((1, 1, kdim), lambda b, i, j: (b, 0, 0)),
            pl.BlockSpec((kdim, tn), lambda b, i, j: (0, j)),
        ],
        out_specs=pl.BlockSpec((1, tm, tn), lambda b, i, j: (b, i, j)),
        out_shape=jax.ShapeDtypeStruct((bsz, length, n), out_dtype),
        scratch_shapes=[pltpu.VMEM((tm, kdim), BF16)],
        compiler_params=_cparams("parallel", "parallel", "arbitrary"),
        name="proj",
    )(x, g.reshape(1, kdim).astype(F32), shift.reshape(bsz, 1, kdim), scale.reshape(bsz, 1, kdim),
      w.astype(BF16))


def _proj_res_kernel(y_ref, w_ref, r_ref, gt_ref, o_ref):
    acc = jnp.dot(y_ref[0].astype(BF16), w_ref[...].astype(BF16), preferred_element_type=F32)
    o_ref[0] = r_ref[0] + gt_ref[0] * acc


def _proj_res(y, w, res, gate):
    bsz, length, kdim = y.shape
    n = w.shape[1]
    tm = _pick(length, (1024, 512, 256))
    tn = _pick(n, (512, 256, 128))
    return pl.pallas_call(
        _proj_res_kernel,
        grid=(bsz, length // tm, n // tn),
        in_specs=[
            pl.BlockSpec((1, tm, kdim), lambda b, i, j: (b, i, 0)),
            pl.BlockSpec((kdim, tn), lambda b, i, j: (0, j)),
            pl.BlockSpec((1, tm, tn), lambda b, i, j: (b, i, j)),
            pl.BlockSpec((1, 1, tn), lambda b, i, j: (b, 0, j)),
        ],
        out_specs=pl.BlockSpec((1, tm, tn), lambda b, i, j: (b, i, j)),
        out_shape=jax.ShapeDtypeStruct((bsz, length, n), F32),
        compiler_params=_cparams("parallel", "parallel", "arbitrary"),
        name="proj_res",
    )(y, w.astype(BF16), res, gate.reshape(bsz, 1, n))


def _mod_kernel(c_ref, w_ref, b_ref, o_ref):
    c = c_ref[...]
    s = c * jax.nn.sigmoid(c)
    o_ref[...] = jnp.dot(s.astype(BF16), w_ref[...].astype(BF16), preferred_element_type=F32) + b_ref[...]


def _modulation(cc, w, b):
    rows, d = cc.shape
    n = w.shape[1]
    tn = 512
    return pl.pallas_call(
        _mod_kernel,
        grid=(n // tn,),
        in_specs=[
            pl.BlockSpec((rows, d), lambda j: (0, 0)),
            pl.BlockSpec((d, tn), lambda j: (0, j)),
            pl.BlockSpec((1, tn), lambda j: (0, j)),
        ],
        out_specs=pl.BlockSpec((rows, tn), lambda j: (0, j)),
        out_shape=jax.ShapeDtypeStruct((rows, n), F32),
        compiler_params=_cparams("parallel"),
        name="modulation",
    )(cc, w, b.reshape(1, n))


def _final_norm_kernel(x_ref, g_ref, o_ref):
    x = x_ref[0]
    ms = jnp.mean(x * x, axis=-1, keepdims=True)
    o_ref[0] = x * lax.rsqrt(ms + NORM_EPS) * g_ref[...]


def _final_norm(x, g):
    bsz, length, d = x.shape
    tm = 1024
    return pl.pallas_call(
        _final_norm_kernel,
        grid=(bsz, length // tm),
        in_specs=[pl.BlockSpec((1, tm, d), lambda b, i: (b, i, 0)),
                  pl.BlockSpec((1, d), lambda b, i: (0, 0))],
        out_specs=pl.BlockSpec((1, tm, d), lambda b, i: (b, i, 0)),
        out_shape=jax.ShapeDtypeStruct(x.shape, F32),
        compiler_params=_cparams("parallel", "parallel"),
        name="final_norm",
    )(x, g.reshape(1, d))


def _softmax_pv(s_parts, v_parts):
    m = s_parts[0].max(axis=-1, keepdims=True)
    for s in s_parts[1:]:
        m = jnp.maximum(m, s.max(axis=-1, keepdims=True))
    den = None
    acc = None
    for s, v in zip(s_parts, v_parts):
        p = jnp.exp(s - m)
        l = p.sum(axis=-1, keepdims=True)
        o = jnp.dot(p.astype(BF16), v, preferred_element_type=F32)
        den = l if den is None else den + l
        acc = o if acc is None else acc + o
    return acc / den


HEAD_V = 64


def _pair_queries(q2, hw):
    if hw == LANES:
        return [q2[:, :LANES], q2[:, LANES:]]
    lane = lax.broadcasted_iota(jnp.int32, (1, LANES), 1)
    return [jnp.where(lane < hw, q2, jnp.zeros_like(q2)), jnp.where(lane >= hw, q2, jnp.zeros_like(q2))]


def _pair_keys(k_ref, hw, j, rows=None):
    rows = slice(None) if rows is None else rows
    return k_ref[0, rows, j * LANES:(j + 1) * LANES] if hw == LANES else k_ref[0, rows, :]


def _pair_select(o0, o1):
    lane = lax.broadcasted_iota(jnp.int32, (1, LANES), 1)
    return jnp.where(lane < HEAD_V, o0, o1)


def _pair_attn_kernel(*refs, nparts, hw):
    q_ref, o_ref = refs[0], refs[-1]
    k_refs = refs[1:1 + nparts]
    v_refs = refs[1 + nparts:1 + 2 * nparts]
    nt = (((1,), (1,)), ((), ()))
    outs = []
    for j, q in enumerate(_pair_queries(q_ref[0], hw)):
        s_parts = [lax.dot_general(q, _pair_keys(k_ref, hw, j), nt, preferred_element_type=F32)
                   for k_ref in k_refs]
        outs.append(_softmax_pv(s_parts, [v_ref[0] for v_ref in v_refs]))
    o_ref[0] = _pair_select(*outs).astype(o_ref.dtype)


def _pair_attention(q_src, kv_srcs, hw, nh):
    q, q_off = q_src
    bsz, lq = q.shape[0], q.shape[1]
    tq = _pick(lq, (512, 256))
    qw = 2 * hw
    in_specs = [pl.BlockSpec((1, tq, qw), lambda b, h, i: (b, i, q_off // qw + h))]
    args = [q]
    for k_arr, k_off, _, _ in kv_srcs:
        in_specs.append(pl.BlockSpec((1, k_arr.shape[1], qw), lambda b, h, i, o=k_off // qw: (b, 0, o + h)))
        args.append(k_arr)
    for _, _, v_arr, v_off in kv_srcs:
        in_specs.append(pl.BlockSpec((1, v_arr.shape[1], LANES), lambda b, h, i, o=v_off // LANES: (b, 0, o + h)))
        args.append(v_arr)
    return pl.pallas_call(
        functools.partial(_pair_attn_kernel, nparts=len(kv_srcs), hw=hw),
        grid=(bsz, nh // 2, lq // tq),
        in_specs=in_specs,
        out_specs=pl.BlockSpec((1, tq, LANES), lambda b, h, i: (b, i, h)),
        out_shape=jax.ShapeDtypeStruct((bsz, lq, nh * HEAD_V), BF16),
        compiler_params=_cparams("parallel", "parallel", "arbitrary"),
        name="attention",
    )(*args)


def _na_bias_table(rpb, rows):
    nh, n_dr, n_dc = rpb.shape
    kh = min(NA_WIN_H, rows)
    nblk = rows // NA_QROWS
    w = GRID_W
    qc = np.arange(w)[:, None]
    kc = np.arange(w)[None, :]
    cs = np.clip(qc - NA_WIN_W // 2, 0, w - NA_WIN_W)
    col_ok = (kc >= cs) & (kc < cs + NA_WIN_W)
    dc = np.clip(kc - qc, -(NA_WIN_W - 1), NA_WIN_W - 1) + NA_WIN_W - 1
    pick_dc = (dc.reshape(1, w * w) == np.arange(n_dc)[:, None]).astype(np.float32)
    tile = jnp.dot(rpb.reshape(nh * n_dr, n_dc).astype(F32), jnp.asarray(pick_dc), precision=HIGHEST)
    tile = jnp.where(col_ok, tile.reshape(nh, n_dr, w, w), -1e30)
    masked = jnp.full((nh, w, w), -1e30, F32)
    tables = []
    for blk in (0, 1, nblk - 1):
        start = int(np.clip(blk * NA_QROWS - NA_WIN_H // 2, 0, rows - NA_KROWS))
        q_rows = []
        for qr in range(NA_QROWS):
            r = blk * NA_QROWS + qr
            rs = int(np.clip(r - kh // 2, 0, rows - kh))
            tiles = [tile[:, kr - r + NA_WIN_H - 1] if rs <= kr < rs + kh else masked
                     for kr in range(start, start + NA_KROWS)]
            q_rows.append(jnp.concatenate(tiles, axis=-1))
        tables.append(jnp.concatenate(q_rows, axis=1))
    return jnp.stack(tables, axis=1)


def _na_kernel(q_ref, kl_ref, vl_ref, kc_ref, vc_ref, bias_ref, o_ref, *, nblk, rows):
    i = pl.program_id(2)
    start = jnp.clip(i * NA_QROWS - NA_WIN_H // 2, 0, rows - NA_KROWS) * GRID_W
    start = pl.multiple_of(start, GRID_W)
    pat = jnp.where(i == 0, 0, jnp.where(i == nblk - 1, 2, 1))
    local = pl.ds(start, NA_KROWS * GRID_W)
    nt = (((1,), (1,)), ((), ()))
    hw = NA_HEAD_DIM
    outs = []
    for j, q in enumerate(_pair_queries(q_ref[0], hw)):
        s_ctx = lax.dot_general(q, _pair_keys(kc_ref, hw, j), nt, preferred_element_type=F32)
        s_loc = (lax.dot_general(q, _pair_keys(kl_ref, hw, j, local), nt, preferred_element_type=F32)
                 + bias_ref[j, pat])
        outs.append(_softmax_pv([s_ctx, s_loc], [vc_ref[0], vl_ref[0, local, :]]))
    o_ref[0] = _pair_select(*outs).astype(o_ref.dtype)


def _na_mixer(x, ctx, g, sh, sc, csh, csc, w_qkv, rpb, w_o, gate, cgate):
    bsz, length, d = x.shape
    lc = ctx.shape[1]
    rows = length // GRID_W
    nblk = rows // NA_QROWS
    nh, hd = NA_HEADS, NA_HEAD_DIM
    w = jnp.concatenate([w_qkv[:, :d] * hd ** -0.5, w_qkv[:, d:]], axis=1).astype(BF16)
    qkv = _proj(x, w, g, sh, sc, out_dtype=BF16)
    qkv_c = _proj(ctx, w, g, csh, csc, out_dtype=BF16)
    bias = _na_bias_table(rpb, rows)
    tq = NA_QROWS * GRID_W
    npair = nh // 2
    pw = 2 * hd
    y = pl.pallas_call(
        functools.partial(_na_kernel, nblk=nblk, rows=rows),
        grid=(npair, bsz, nblk),
        in_specs=[
            pl.BlockSpec((1, tq, pw), lambda h, b, i: (b, i, h)),
            pl.BlockSpec((1, length, pw), lambda h, b, i: (b, 0, npair + h)),
            pl.BlockSpec((1, length, pw), lambda h, b, i: (b, 0, 2 * npair + h)),
            pl.BlockSpec((1, lc, pw), lambda h, b, i: (b, 0, npair + h)),
            pl.BlockSpec((1, lc, pw), lambda h, b, i: (b, 0, 2 * npair + h)),
            pl.BlockSpec((2, 3, tq, NA_KROWS * GRID_W), lambda h, b, i: (h, 0, 0, 0)),
        ],
        out_specs=pl.BlockSpec((1, tq, pw), lambda h, b, i: (b, i, h)),
        out_shape=jax.ShapeDtypeStruct((bsz, length, d), BF16),
        compiler_params=_cparams("parallel", "parallel", "arbitrary"),
        name="na_local",
    )(qkv, qkv, qkv, qkv_c, qkv_c, bias)
    if cgate is None:
        return _proj_res(y, w_o, x, gate), None
    yc = _pair_attention((qkv_c, 0), [(qkv_c, d, qkv_c, 2 * d)], hd, nh)
    return _proj_res(y, w_o, x, gate), _proj_res(yc, w_o, ctx, cgate)


def _axial_rope(length):
    t = jnp.arange(length)
    row = (t // GRID_W).astype(F32)
    col = (t % GRID_W).astype(F32)
    n_freq = MLA_ROPE_DIM // 4
    inv = ROPE_BASE ** (-jnp.arange(n_freq, dtype=F32) / n_freq)
    ang = jnp.concatenate([row[:, None] * inv, col[:, None] * inv], axis=-1)
    return jnp.cos(ang), jnp.sin(ang)


def _rot_half_cols(w):
    half = w.shape[-1] // 2
    return jnp.concatenate([-w[..., half:], w[..., :half]], axis=-1)


def _mla_weights(w_in, w_q_b, w_kv_b):
    nh, nope, rope = MLA_HEADS, MLA_NOPE_DIM, MLA_ROPE_DIM
    off = MLA_Q_RANK + MLA_KV_RANK
    w_in_ext = jnp.concatenate([w_in, _rot_half_cols(w_in[:, off:])], axis=1)
    rq, rkv = w_q_b.shape[0], w_kv_b.shape[0]
    wq = w_q_b.reshape(rq, nh, nope + rope)
    pad = jnp.zeros((rq, nh, LANES - nope - rope), F32)
    wq_pad = jnp.concatenate([wq, pad], axis=-1).reshape(rq, nh * LANES)
    wq_rot = jnp.concatenate([jnp.zeros((rq, nh, nope), F32), _rot_half_cols(wq[..., nope:]), pad],
                             axis=-1).reshape(rq, nh * LANES)
    wkv = w_kv_b.reshape(rkv, nh, nope + MLA_V_DIM)
    wk_pad = jnp.concatenate([wkv[..., :nope], jnp.zeros((rkv, nh, LANES - nope), F32)],
                             axis=-1).reshape(rkv, nh * LANES)
    wv = wkv[..., nope:].reshape(rkv, nh * MLA_V_DIM)
    place = np.zeros((rope, nh, LANES), np.float32)
    place[np.arange(rope), :, nope + np.arange(rope)] = 1.0
    place = jnp.asarray(place.reshape(rope, nh * LANES))
    return (w_in_ext,) + tuple(a.astype(BF16) for a in (wq_pad, wq_rot, wk_pad, wv, place))


def _mla_rope_tables(length, rope_on):
    if rope_on:
        cos, sin = _axial_rope(length)
    else:
        cos = jnp.ones((length, MLA_ROPE_DIM // 2), F32)
        sin = jnp.zeros((length, MLA_ROPE_DIM // 2), F32)
    ck = jnp.concatenate([cos, cos], axis=-1)
    sk = jnp.concatenate([sin, sin], axis=-1)
    one = jnp.ones((length, MLA_NOPE_DIM), F32)
    tail = LANES - MLA_NOPE_DIM - MLA_ROPE_DIM
    cq = jnp.concatenate([one, ck, one[:, :tail]], axis=-1)
    sq = jnp.concatenate([0.0 * one, sk, 0.0 * one[:, :tail]], axis=-1)
    return cq, sq, ck, sk


def _mla_qkv_kernel(z_ref, gq_ref, gkv_ref, wq_ref, wqr_ref, wk_ref, wv_ref, place_ref,
                    cq_ref, sq_ref, ck_ref, sk_ref, q_ref, k_ref, v_ref):
    z = z_ref[0]
    off = MLA_Q_RANK + MLA_KV_RANK
    rope = MLA_ROPE_DIM

    def norm(a, g_ref):
        ms = jnp.mean(a * a, axis=-1, keepdims=True)
        return (a * lax.rsqrt(ms + NORM_EPS) * g_ref[...]).astype(BF16)

    cq = norm(z[:, :MLA_Q_RANK], gq_ref)
    ckv = norm(z[:, MLA_Q_RANK:off], gkv_ref)
    k_rope = (z[:, off:off + rope] * ck_ref[...] + z[:, off + rope:off + 2 * rope] * sk_ref[...]).astype(BF16)
    qa = jnp.dot(cq, wq_ref[...], preferred_element_type=F32)
    qb = jnp.dot(cq, wqr_ref[...], preferred_element_type=F32)
    cos_q = cq_ref[...]
    sin_q = sq_ref[...]
    for h in range(MLA_HEADS):
        cols = slice(h * LANES, (h + 1) * LANES)
        q_ref[0, :, cols] = (qa[:, cols] * cos_q + qb[:, cols] * sin_q).astype(BF16)
    k_ref[0] = (jnp.dot(ckv, wk_ref[...], preferred_element_type=F32)
                + jnp.dot(k_rope, place_ref[...], preferred_element_type=F32)).astype(BF16)
    v_ref[0] = jnp.dot(ckv, wv_ref[...], preferred_element_type=F32).astype(BF16)


def _mla_qkv(z, q_norm_g, kv_norm_g, weights, tables):
    bsz, length, zw = z.shape
    wq, wqr, wk, wv, place = weights
    nh = MLA_HEADS
    tm = _pick(length, (512, 256))
    full = lambda a: pl.BlockSpec(a.shape, lambda b, i: (0, 0))
    rows = lambda a: pl.BlockSpec((tm, a.shape[1]), lambda b, i: (i, 0))
    gq = q_norm_g.reshape(1, -1)
    gkv = kv_norm_g.reshape(1, -1)
    out = lambda w: pl.BlockSpec((1, tm, w), lambda b, i: (b, i, 0))
    return pl.pallas_call(
        _mla_qkv_kernel,
        grid=(bsz, length // tm),
        in_specs=[pl.BlockSpec((1, tm, zw), lambda b, i: (b, i, 0)), full(gq), full(gkv),
                  full(wq), full(wqr), full(wk), full(wv), full(place)] + [rows(t) for t in tables],
        out_specs=[out(nh * LANES), out(nh * LANES), out(nh * MLA_V_DIM)],
        out_shape=[jax.ShapeDtypeStruct((bsz, length, nh * LANES), BF16),
                   jax.ShapeDtypeStruct((bsz, length, nh * LANES), BF16),
                   jax.ShapeDtypeStruct((bsz, length, nh * MLA_V_DIM), BF16)],
        compiler_params=_cparams("parallel", "parallel"),
        name="mla_qkv",
    )(z, gq, gkv, wq, wqr, wk, wv, place, *tables)


def _mla_mixer(x, ctx, g, sh, sc, csh, csc, w_in, q_norm_g, w_q_b, kv_norm_g, w_kv_b, w_o, gate, cgate):
    nh = MLA_HEADS
    scale = (MLA_NOPE_DIM + MLA_ROPE_DIM) ** -0.5
    w_in_ext, *weights = _mla_weights(w_in, w_q_b, w_kv_b)

    def project(u, shift, scl, rope_on):
        z = _proj(u, w_in_ext, g, shift, scl)
        cos_q, sin_q, cos_k, sin_k = _mla_rope_tables(u.shape[1], rope_on)
        return _mla_qkv(z, q_norm_g, kv_norm_g, weights, (cos_q * scale, sin_q * scale, cos_k, sin_k))

    ql, kl, vl = project(x, sh, sc, True)
    qc, kc, vc = project(ctx, csh, csc, False)
    y = _pair_attention((ql, 0), [(kc, 0, vc, 0), (kl, 0, vl, 0)], LANES, nh)
    if cgate is None:
        return _proj_res(y, w_o, x, gate), None
    yc = _pair_attention((qc, 0), [(kc, 0, vc, 0)], LANES, nh)
    return _proj_res(y, w_o, x, gate), _proj_res(yc, w_o, ctx, cgate)


def _conv3(z, w_ref, b_ref):
    length = z.shape[0]
    t = lax.broadcasted_iota(jnp.int32, z.shape, 0)
    prev = jnp.where(t == 0, 0.0, pltpu.roll(z, 1, axis=0))
    nxt = jnp.where(t == length - 1, 0.0, pltpu.roll(z, length - 1, axis=0))
    return b_ref[...] + prev * w_ref[0:1, :] + z * w_ref[1:2, :] + nxt * w_ref[2:3, :]


def _hy_conv_kernel(z0_ref, z1_ref, z2_ref, w0_ref, w1_ref, w2_ref, b0_ref, b1_ref, b2_ref,
                    x0_ref, gg_ref):
    x0_ref[0] = _conv3(z0_ref[0], w0_ref, b0_ref)
    x1 = _conv3(z1_ref[0], w1_ref, b1_ref)
    v = _conv3(z2_ref[0], w2_ref, b2_ref)
    gg_ref[0] = v * x1


def _hy_conv(z, conv_w, conv_b, d):
    bsz, length, _ = z.shape
    tc = 256
    nb = d // tc
    zspec = lambda k: pl.BlockSpec((1, length, tc), lambda b, j: (b, 0, k * nb + j))
    wspec = lambda k: pl.BlockSpec((3, tc), lambda b, j: (0, k * nb + j))
    bspec = lambda k: pl.BlockSpec((1, tc), lambda b, j: (0, k * nb + j))
    ospec = pl.BlockSpec((1, length, tc), lambda b, j: (b, 0, j))
    cb = conv_b.reshape(1, 3 * d)
    return pl.pallas_call(
        _hy_conv_kernel,
        grid=(bsz, nb),
        in_specs=[zspec(0), zspec(1), zspec(2), wspec(0), wspec(1), wspec(2), bspec(0), bspec(1), bspec(2)],
        out_specs=[ospec, ospec],
        out_shape=[jax.ShapeDtypeStruct((bsz, length, d), F32)] * 2,
        compiler_params=_cparams("parallel", "parallel"),
        name="hy_conv",
    )(z, z, z, conv_w, conv_w, conv_w, cb, cb, cb)


def _dft_tables(length):
    n2 = 2 * length
    f = jnp.arange(length, dtype=jnp.int32)[:, None]
    n = jnp.arange(length, dtype=jnp.int32)[None, :]
    ang = ((f * n) % n2).astype(F32) * (2.0 * math.pi / n2)
    cf = jnp.cos(ang)
    sgn_n = jnp.where(n % 2 == 0, 1.0, -1.0).astype(F32)
    sf = jnp.where(f == 0, sgn_n, -jnp.sin(ang))
    return cf.astype(BF16), sf.astype(BF16), sf.T.astype(BF16)


def _hy_fwd_kernel(cf_ref, sf_ref, gg_ref, kre_ref, kim_ref, y_ref, gb_ref):
    j = pl.program_id(1)

    @pl.when(j == 0)
    def _():
        gb_ref[...] = gg_ref[0].astype(BF16)

    ure = jnp.dot(cf_ref[...], gb_ref[...], preferred_element_type=F32)
    uim = jnp.dot(sf_ref[...], gb_ref[...], preferred_element_type=F32)
    kre = kre_ref[...]
    kim = kim_ref[...]
    tf = ure.shape[0]
    row = lax.broadcasted_iota(jnp.int32, ure.shape, 0) + j * tf
    packed = row == 0
    yre = ure * kre - jnp.where(packed, 0.0, uim * kim)
    yim = uim * jnp.where(packed, kim, kre) + jnp.where(packed, 0.0, ure * kim)
    y_ref[0, 0] = yre.astype(BF16)
    y_ref[0, 1] = yim.astype(BF16)


def _hy_inv_kernel(ci_ref, si_ref, y_ref, gg_ref, x0_ref, skip_ref, o_ref):
    y = (jnp.dot(ci_ref[...], y_ref[0, 0], preferred_element_type=F32)
         + jnp.dot(si_ref[...], y_ref[0, 1], preferred_element_type=F32))
    o_ref[0] = ((y + gg_ref[0] * skip_ref[...]) * x0_ref[0]).astype(o_ref.dtype)


def _hy_long_conv_gate(gg, x0, kre, kim, skip, tables):
    bsz, length, d = gg.shape
    cf, sf, sft = tables
    tf = 256
    y = pl.pallas_call(
        _hy_fwd_kernel,
        grid=(bsz, length // tf),
        in_specs=[
            pl.BlockSpec((tf, length), lambda b, j: (j, 0)),
            pl.BlockSpec((tf, length), lambda b, j: (j, 0)),
            pl.BlockSpec((1, length, d), lambda b, j: (b, 0, 0)),
            pl.BlockSpec((tf, d), lambda b, j: (j, 0)),
            pl.BlockSpec((tf, d), lambda b, j: (j, 0)),
        ],
        out_specs=pl.BlockSpec((1, 2, tf, d), lambda b, j: (b, 0, j, 0)),
        out_shape=jax.ShapeDtypeStruct((bsz, 2, length, d), BF16),
        scratch_shapes=[pltpu.VMEM((length, d), BF16)],
        compiler_params=_cparams("parallel", "arbitrary"),
        name="hy_dft_fwd",
    )(cf, sf, gg, kre, kim)
    tt = 256
    return pl.pallas_call(
        _hy_inv_kernel,
        grid=(bsz, length // tt),
        in_specs=[
            pl.BlockSpec((tt, length), lambda b, i: (i, 0)),
            pl.BlockSpec((tt, length), lambda b, i: (i, 0)),
            pl.BlockSpec((1, 2, length, d), lambda b, i: (b, 0, 0, 0)),
            pl.BlockSpec((1, tt, d), lambda b, i: (b, i, 0)),
            pl.BlockSpec((1, tt, d), lambda b, i: (b, i, 0)),
            pl.BlockSpec((1, d), lambda b, i: (0, 0)),
        ],
        out_specs=pl.BlockSpec((1, tt, d), lambda b, i: (b, i, 0)),
        out_shape=jax.ShapeDtypeStruct((bsz, length, d), BF16),
        compiler_params=_cparams("parallel", "arbitrary"),
        name="hy_dft_inv",
    )(cf, sft, y, gg, x0, skip.reshape(1, d))


def _hyena_filter(length, w1, b1, w2, b2, w3, b3, sin_freq):
    t = jnp.linspace(0.0, 1.0, length, dtype=F32)[:, None]
    bands = (HY_EMB_DIM - 1) // 2
    w = (2.0 * math.pi / length) * jnp.arange(length, dtype=F32)[:, None]
    f = jnp.linspace(1e-4, bands - 1, bands, dtype=F32)[None, :]
    z = jnp.concatenate([t, jnp.cos(f * w), -jnp.sin(f * w)], axis=-1)
    hp = lax.Precision.HIGHEST
    h = jnp.sin(sin_freq[0] * (jnp.dot(z, w1, precision=hp) + b1))
    h = jnp.sin(sin_freq[1] * (jnp.dot(h, w2, precision=hp) + b2))
    h = jnp.dot(h, w3, precision=hp) + b3
    max_decay = math.log(HY_DECAY_TARGET) / HY_FAST_DECAY
    min_decay = math.log(HY_DECAY_TARGET) / HY_SLOW_DECAY
    deltas = jnp.abs(jnp.linspace(min_decay, max_decay, h.shape[-1] // 2, dtype=F32))
    deltas = jnp.tile(deltas, 2)
    return h * (jnp.exp(-t * deltas) + HY_MOD_SHIFT)


def _hy_filter_spectrum(filt, d, tables):
    length = filt.shape[0]
    cf, sf, _ = tables
    hf, hb = filt[:, :d], filt[:, d:]
    ka = hf
    kb = jnp.concatenate([jnp.zeros((1, d), F32), hb[1:][::-1]], axis=0)
    tab = jnp.concatenate([cf, sf], axis=0)[None]
    spec = _proj(tab, jnp.concatenate([ka, kb], axis=1))[0]
    sgn_f = jnp.where(jnp.arange(length) % 2 == 0, 1.0, -1.0).astype(F32)[:, None]
    wgt = jnp.where(jnp.arange(length) == 0, 1.0, 2.0).astype(F32)[:, None] / (2 * length)
    kre = (spec[:length, :d] + sgn_f * spec[:length, d:]) * wgt
    kim = (spec[length:, :d] + sgn_f * spec[length:, d:]) * wgt
    return kre, kim


def _hyena_mixer(x, ctx, g, sh, sc, csh, csc, w_in, conv_w, conv_b, f_w1, f_b1, f_w2, f_b2, f_w3, f_b3,
                 sin_freq, skip, w_o, gate, cgate):
    d = x.shape[-1]

    def one(u, shift, scale, res, gt):
        length = u.shape[1]
        z = _proj(u, w_in, g, shift, scale)
        x0, gg = _hy_conv(z, conv_w, conv_b, d)
        tables = _dft_tables(length)
        filt = _hyena_filter(length, f_w1, f_b1, f_w2, f_b2, f_w3, f_b3, sin_freq)
        kre, kim = _hy_filter_spectrum(filt, d, tables)
        y = _hy_long_conv_gate(gg, x0, kre, kim, skip, tables)
        return _proj_res(y, w_o, res, gt)

    return one(x, sh, sc, x, gate), (one(ctx, csh, csc, ctx, cgate) if cgate is not None else None)


def _ml_conv_kernel(z_ref, w_ref, b_ref, o_ref, *t_refs, out_scale):
    y = _conv3(z_ref[0], w_ref, b_ref)
    y = y * jax.nn.sigmoid(y) * out_scale
    o_ref[0] = y
    for t_ref in t_refs:
        t_ref[0] = y.T


def _ml_conv(z, conv_w, conv_b, col0, width, out_scale, transposed):
    bsz, length, _ = z.shape
    tc = 256
    j0 = col0 // tc
    out_specs = [pl.BlockSpec((1, length, tc), lambda b, j: (b, 0, j))]
    out_shape = [jax.ShapeDtypeStruct((bsz, length, width), F32)]
    if transposed:
        out_specs.append(pl.BlockSpec((1, tc, length), lambda b, j: (b, j, 0)))
        out_shape.append(jax.ShapeDtypeStruct((bsz, width, length), F32))
    return pl.pallas_call(
        functools.partial(_ml_conv_kernel, out_scale=out_scale),
        grid=(bsz, width // tc),
        in_specs=[pl.BlockSpec((1, length, tc), lambda b, j: (b, 0, j0 + j)),
                  pl.BlockSpec((3, tc), lambda b, j: (0, j0 + j)),
                  pl.BlockSpec((1, tc), lambda b, j: (0, j0 + j))],
        out_specs=out_specs,
        out_shape=out_shape,
        compiler_params=_cparams("parallel", "parallel"),
        name="ml_conv",
    )(z, conv_w, conv_b.reshape(1, -1))


def _ml_gates_kernel(x_ref, g_ref, sh_ref, sc_ref, w_ref, b_ref, o_ref, t_ref):
    x = x_ref[0]
    ms = jnp.mean(x * x, axis=-1, keepdims=True)
    h = x * lax.rsqrt(ms + NORM_EPS) * g_ref[...]
    h = h * (1.0 + sc_ref[0]) + sh_ref[0]
    gates = jnp.dot(h.astype(BF16), w_ref[...], preferred_element_type=F32) + b_ref[...]
    o_ref[0] = gates
    t_ref[0] = gates.T


def _ml_gates(x, g, shift, scale, w_gate, gate_b):
    bsz, length, d = x.shape
    pad = LANES - w_gate.shape[1]
    w = jnp.pad(w_gate, ((0, 0), (0, pad))).astype(BF16)
    b = jnp.pad(gate_b, (0, pad)).reshape(1, LANES)
    tm = _pick(length, (1024, 256))
    return pl.pallas_call(
        _ml_gates_kernel,
        grid=(bsz, length // tm),
        in_specs=[
            pl.BlockSpec((1, tm, d), lambda b_, i: (b_, i, 0)),
            pl.BlockSpec((1, d), lambda b_, i: (0, 0)),
            pl.BlockSpec((1, 1, d), lambda b_, i: (b_, 0, 0)),
            pl.BlockSpec((1, 1, d), lambda b_, i: (b_, 0, 0)),
            pl.BlockSpec((d, LANES), lambda b_, i: (0, 0)),
            pl.BlockSpec((1, LANES), lambda b_, i: (0, 0)),
        ],
        out_specs=[pl.BlockSpec((1, tm, LANES), lambda b_, i: (b_, i, 0)),
                   pl.BlockSpec((1, LANES, tm), lambda b_, i: (b_, 0, i))],
        out_shape=[jax.ShapeDtypeStruct((bsz, length, LANES), F32),
                   jax.ShapeDtypeStruct((bsz, LANES, length), F32)],
        compiler_params=_cparams("parallel", "parallel"),
        name="ml_gates",
    )(x, g.reshape(1, d), shift.reshape(bsz, 1, d), scale.reshape(bsz, 1, d), w, b)


def _log_sigmoid(x):
    return jnp.minimum(x, 0.0) - jnp.log1p(jnp.exp(-jnp.abs(x)))


def _ml_chunk_pair(q2, kt2, k2, v2, gates, i_rows, f_rows, f_lanes, state, reverse):
    ct2, n2, ms = state
    tlen = q2.shape[0]
    dk, dv = ML_QK_DIM, ML_V_DIM
    ti = lax.broadcasted_iota(jnp.int32, (tlen, tlen), 0)
    si = lax.broadcasted_iota(jnp.int32, (tlen, tlen), 1)
    causal = (si >= ti) if reverse else (si <= ti)
    causal_t = (ti >= si) if reverse else (ti <= si)
    lane = lax.broadcasted_iota(jnp.int32, (1, LANES), 1)
    sub = lax.broadcasted_iota(jnp.int32, (LANES, 1), 0)
    kt_b = kt2.astype(BF16)
    k_b = k2.astype(BF16)
    v_b = v2.astype(BF16)
    ct_b = ct2.astype(BF16)
    hs, ws, decays, m_news = [], [], [], []
    for j in range(2):
        own = (lane >= dk) if j else (lane < dk)
        lf_row = _log_sigmoid(f_rows[j])
        f_col = jnp.where(lane == f_lanes[j], gates, 0.0).sum(axis=-1, keepdims=True)
        lf_col = _log_sigmoid(f_col)
        b_col = jnp.where(causal, lf_row, 0.0).sum(axis=-1, keepdims=True)
        b_row = jnp.where(causal_t, lf_col, 0.0).sum(axis=0, keepdims=True)
        log_d = jnp.where(causal, b_col - b_row + i_rows[j], -jnp.inf)
        m_inter = b_col + ms[j]
        m_t = jnp.maximum(log_d.max(axis=-1, keepdims=True), m_inter)
        qj = jnp.where(own, q2, 0.0)
        qj_b = qj.astype(BF16)
        s = jnp.dot(qj_b, kt_b, preferred_element_type=F32) * jnp.exp(log_d - m_t)
        inter = jnp.exp(m_inter - m_t)
        sv = jnp.dot(s.astype(BF16), v_b, preferred_element_type=F32)[:, j * dv:(j + 1) * dv]
        num = sv + inter * jnp.dot(qj_b, ct_b, preferred_element_type=F32)
        qn = s.sum(axis=-1, keepdims=True) + inter * (qj * n2).sum(axis=-1, keepdims=True)
        hs.append(num / jnp.maximum(jnp.abs(qn), jnp.exp(-m_t)))
        b_end = b_row[:, 0:1] if reverse else b_row[:, tlen - 1:tlen]
        w_log = b_end - b_row + i_rows[j]
        m_new = jnp.maximum(b_end + ms[j], w_log.max(axis=-1, keepdims=True))
        ws.append(jnp.exp(w_log - m_new))
        decays.append(jnp.exp(b_end + ms[j] - m_new))
        m_news.append(m_new)
    head0_rows = sub < dk
    head0_lanes = lane < dk
    kw = (kt2 * jnp.where(head0_rows, ws[0], ws[1])).astype(BF16)
    upd = jnp.dot(kw, v_b, preferred_element_type=F32)
    ct_new = (jnp.where(head0_rows, decays[0], decays[1]) * ct2
              + jnp.where(head0_rows, upd[:, :dv], upd[:, dv:]))
    wk = [jnp.dot(w.astype(BF16), k_b, preferred_element_type=F32) for w in ws]
    n_new = jnp.where(head0_lanes, decays[0], decays[1]) * n2 + jnp.where(head0_lanes, wk[0], wk[1])
    return jnp.concatenate(hs, axis=-1), (ct_new, n_new, tuple(m_news))


def _mlstm_kernel(ql_ref, kl_ref, ktl_ref, vl_ref, ol_ref, gl_ref, gtl_ref,
                  qc_ref, kc_ref, ktc_ref, vc_ref, oc_ref, gc_ref, gtc_ref, ng_ref,
                  yl_ref, yc_ref, hfl_ref, hbl_ref, hfc_ref, hbc_ref):
    tlen = ML_CHUNK
    dv, nh = ML_V_DIM, ML_HEADS
    n_lat = ql_ref.shape[1] // tlen
    n_ctx = qc_ref.shape[1] // tlen
    head0 = 2 * pl.program_id(1)

    def run(refs, c, state, direction):
        q_ref, k_ref, kt_ref, v_ref, g_ref, gt_ref, hf_ref, hb_ref = refs
        rows = pl.ds(pl.multiple_of(c * tlen, tlen), tlen)
        gate_i, gate_f = 2 * direction * nh, (2 * direction + 1) * nh
        i_rows = [gt_ref[0, pl.ds(gate_i + head0 + j, 1), rows] for j in range(2)]
        f_rows = [gt_ref[0, pl.ds(gate_f + head0 + j, 1), rows] for j in range(2)]
        f_lanes = [gate_f + head0 + j for j in range(2)]
        h, state = _ml_chunk_pair(q_ref[0, rows, :], kt_ref[0, :, rows], k_ref[0, rows, :], v_ref[0, rows, :],
                                  g_ref[0, rows, :], i_rows, f_rows, f_lanes, state, direction == 1)
        (hb_ref if direction else hf_ref)[rows, :] = h
        return state

    def sweep(refs, n_chunks, states):
        def body(c, sts):
            return run(refs, c, sts[0], 0), run(refs, n_chunks - 1 - c, sts[1], 1)
        return lax.fori_loop(0, n_chunks, body, states)

    zero = (jnp.zeros((LANES, dv), F32), jnp.zeros((1, LANES), F32),
            (jnp.zeros((1, 1), F32), jnp.zeros((1, 1), F32)))
    lat = (ql_ref, kl_ref, ktl_ref, vl_ref, gl_ref, gtl_ref, hfl_ref, hbl_ref)
    ctx = (qc_ref, kc_ref, ktc_ref, vc_ref, gc_ref, gtc_ref, hfc_ref, hbc_ref)
    sweep(lat, n_lat, sweep(ctx, n_ctx, (zero, zero)))

    def finish(hf_ref, hb_ref, o_ref, y_ref):
        for j in range(2):
            vs = slice(j * dv, (j + 1) * dv)
            h = hf_ref[:, vs] + hb_ref[:, vs]
            h = h * lax.rsqrt(jnp.mean(h * h, axis=-1, keepdims=True) + NORM_EPS) * ng_ref[:, vs]
            y_ref[0, :, vs] = (h * jax.nn.sigmoid(o_ref[0, :, vs])).astype(y_ref.dtype)

    finish(hfl_ref, hbl_ref, ol_ref, yl_ref)
    finish(hfc_ref, hbc_ref, oc_ref, yc_ref)


def _mlstm_mixer(x, ctx, g, sh, sc, csh, csc, w_in, conv_w, conv_b, gate_b, out_norm_g, w_o, gate, cgate):
    nh, dk, dv = ML_HEADS, ML_QK_DIM, ML_V_DIM
    nqk = 2 * nh * dk
    wide = nqk + 2 * nh * dv
    w_main = w_in[:, :wide].astype(BF16)
    w_gate = w_in[:, wide:]

    def project(u, shift, scale):
        z = _proj(u, w_main, g, shift, scale)
        gates, gates_t = _ml_gates(u, g, shift, scale, w_gate, gate_b)
        (q,) = _ml_conv(z, conv_w, conv_b, 0, nh * dk, 1.0, False)
        k, kt = _ml_conv(z, conv_w, conv_b, nh * dk, nh * dk, dk ** -0.5, True)
        return z, q, k, kt, gates, gates_t

    zl, ql, kl, ktl, gl, gtl = project(x, sh, sc)
    zc, qc, kc, ktc, gc, gtc = project(ctx, csh, csc)
    bsz, length, d = x.shape
    lc = ctx.shape[1]
    npair = nh // 2
    qw, vw = 2 * dk, 2 * dv
    v_off, o_off = nqk // vw, (nqk + nh * dv) // vw

    def specs(n):
        return [
            pl.BlockSpec((1, n, qw), lambda b, h: (b, 0, h)),
            pl.BlockSpec((1, n, qw), lambda b, h: (b, 0, h)),
            pl.BlockSpec((1, qw, n), lambda b, h: (b, h, 0)),
            pl.BlockSpec((1, n, vw), lambda b, h: (b, 0, v_off + h)),
            pl.BlockSpec((1, n, vw), lambda b, h: (b, 0, o_off + h)),
            pl.BlockSpec((1, n, LANES), lambda b, h: (b, 0, 0)),
            pl.BlockSpec((1, LANES, n), lambda b, h: (b, 0, 0)),
        ]

    yl, yc = pl.pallas_call(
        _mlstm_kernel,
        grid=(bsz, npair),
        in_specs=specs(length) + specs(lc) + [pl.BlockSpec((1, vw), lambda b, h: (0, h))],
        out_specs=[pl.BlockSpec((1, length, vw), lambda b, h: (b, 0, h)),
                   pl.BlockSpec((1, lc, vw), lambda b, h: (b, 0, h))],
        out_shape=[jax.ShapeDtypeStruct((bsz, length, nh * dv), BF16),
                   jax.ShapeDtypeStruct((bsz, lc, nh * dv), BF16)],
        scratch_shapes=[pltpu.VMEM((length, vw), F32), pltpu.VMEM((length, vw), F32),
                        pltpu.VMEM((lc, vw), F32), pltpu.VMEM((lc, vw), F32)],
        compiler_params=_cparams("parallel", "parallel"),
        name="mlstm",
    )(ql, kl, ktl, zl, zl, gl, gtl, qc, kc, ktc, zc, zc, gc, gtc, out_norm_g.reshape(1, nh * dv))
    return _proj_res(yl, w_o, x, gate), (_proj_res(yc, w_o, ctx, cgate) if cgate is not None else None)


def _router_kernel(x_ref, g_ref, sh_ref, sc_ref, rw_ref, h_ref, aff_ref):
    x = x_ref[0]
    ms = jnp.mean(x * x, axis=-1, keepdims=True)
    h = x * lax.rsqrt(ms + NORM_EPS) * g_ref[...]
    h = h * (1.0 + sc_ref[0]) + sh_ref[0]
    h_ref[0] = h.astype(BF16)
    logits = lax.dot_general(rw_ref[...], h, (((1,), (1,)), ((), ())), precision=HIGHEST,
                             preferred_element_type=F32)
    mx = logits.max(axis=0, keepdims=True)
    p = jnp.exp(logits - mx)
    aff_ref[0] = p / p.sum(axis=0, keepdims=True)


def _router(x, g, shift, scale, router_w):
    bsz, length, d = x.shape
    ne = router_w.shape[1]
    tm = _pick(length, (1024, 256))
    return pl.pallas_call(
        _router_kernel,
        grid=(bsz, length // tm),
        in_specs=[
            pl.BlockSpec((1, tm, d), lambda b, i: (b, i, 0)),
            pl.BlockSpec((1, d), lambda b, i: (0, 0)),
            pl.BlockSpec((1, 1, d), lambda b, i: (b, 0, 0)),
            pl.BlockSpec((1, 1, d), lambda b, i: (b, 0, 0)),
            pl.BlockSpec((ne, d), lambda b, i: (0, 0)),
        ],
        out_specs=[pl.BlockSpec((1, tm, d), lambda b, i: (b, i, 0)),
                   pl.BlockSpec((1, ne, tm), lambda b, i: (b, 0, i))],
        out_shape=[jax.ShapeDtypeStruct((bsz, length, d), BF16),
                   jax.ShapeDtypeStruct((bsz, ne, length), F32)],
        compiler_params=_cparams("parallel", "parallel"),
        name="moe_router",
    )(x, g.reshape(1, d), shift.reshape(bsz, 1, d), scale.reshape(bsz, 1, d), router_w.T)


def _excl_cumsum_lanes(flags):
    rows, length = flags.shape
    cw = min(length, 256)
    si = lax.broadcasted_iota(jnp.int32, (cw, cw), 0)
    ti = lax.broadcasted_iota(jnp.int32, (cw, cw), 1)
    upper = jnp.where(si < ti, 1.0, 0.0).astype(BF16)
    carry = jnp.zeros((rows, 1), F32)
    parts = []
    for c in range(length // cw):
        blk = flags[:, c * cw:(c + 1) * cw]
        parts.append(jnp.dot(blk.astype(BF16), upper, preferred_element_type=F32) + carry)
        carry = carry + blk.sum(axis=-1, keepdims=True)
    return parts[0] if len(parts) == 1 else jnp.concatenate(parts, axis=-1)


def _select_kernel(aff_ref, pos_ref, *, cap):
    a = aff_ref[...]
    rows = a.shape[0]
    capf = float(cap)

    def bisect(_, lohi):
        lo, hi = lohi
        mid = 0.5 * (lo + hi)
        cnt = jnp.where(a >= mid, 1.0, 0.0).sum(axis=-1, keepdims=True)
        ge = cnt >= capf
        return jnp.where(ge, mid, lo), jnp.where(ge, hi, mid)

    lo, hi = lax.fori_loop(0, SELECT_BISECTIONS, bisect,
                           (jnp.zeros((rows, 1), F32), jnp.full((rows, 1), 2.0, F32)))
    above = a >= hi
    tie = jnp.logical_and(a >= lo, jnp.logical_not(above))
    n_above = jnp.where(above, 1.0, 0.0).sum(axis=-1, keepdims=True)
    tie_rank = _excl_cumsum_lanes(jnp.where(tie, 1.0, 0.0))
    sel = jnp.logical_or(above, jnp.logical_and(tie, tie_rank < capf - n_above))
    slot = _excl_cumsum_lanes(jnp.where(sel, 1.0, 0.0))
    pos_ref[...] = jnp.where(sel, slot, -1.0).astype(jnp.int32)


def _select(aff2d, cap):
    return pl.pallas_call(
        functools.partial(_select_kernel, cap=cap),
        out_shape=jax.ShapeDtypeStruct(aff2d.shape, jnp.int32),
        compiler_params=pltpu.CompilerParams(vmem_limit_bytes=V7X_VMEM_LIMIT_BYTES),
        name="moe_select",
    )(aff2d)


def _gather_kernel(h_ref, pos_ref, aff_ref, xg_ref, gs_ref, *, cap):
    pos = pos_ref[0, 0]
    length = pos.shape[1]
    slot = lax.broadcasted_iota(jnp.int32, (cap, length), 0)
    hit = slot == pos
    onehot = jnp.where(hit, 1.0, 0.0).astype(BF16)
    xg_ref[0] = jnp.dot(onehot, h_ref[0], preferred_element_type=F32).astype(BF16)
    gs_ref[0] = jnp.where(hit, aff_ref[0, 0], 0.0).sum(axis=-1, keepdims=True)


def _gather(h, pos, aff, cap):
    bsz, length, d = h.shape
    ne = pos.shape[1]
    return pl.pallas_call(
        functools.partial(_gather_kernel, cap=cap),
        grid=(bsz, ne),
        in_specs=[
            pl.BlockSpec((1, length, d), lambda b, e: (b, 0, 0)),
            pl.BlockSpec((1, 1, 1, length), lambda b, e: (b, e, 0, 0)),
            pl.BlockSpec((1, 1, 1, length), lambda b, e: (b, e, 0, 0)),
        ],
        out_specs=[pl.BlockSpec((1, cap, d), lambda b, e: (e, b, 0)),
                   pl.BlockSpec((1, cap, 1), lambda b, e: (e, b, 0))],
        out_shape=[jax.ShapeDtypeStruct((ne, bsz * cap, d), BF16),
                   jax.ShapeDtypeStruct((ne, bsz * cap, 1), F32)],
        compiler_params=_cparams("parallel", "arbitrary"),
        name="moe_gather",
    )(h, pos.reshape(bsz, ne, 1, length), aff.reshape(bsz, ne, 1, length))


def _ffn_kernel(*refs, n_groups):
    xg_refs = refs[:n_groups]
    gs_refs = refs[n_groups:2 * n_groups]
    wg_ref, wu_ref, wd_ref = refs[2 * n_groups:2 * n_groups + 3]
    y_refs = refs[2 * n_groups + 3:3 * n_groups + 3]
    acc_refs = refs[3 * n_groups + 3:]
    f = pl.program_id(1)
    nf = pl.num_programs(1)
    wg = wg_ref[0].astype(BF16)
    wu = wu_ref[0].astype(BF16)
    wd = wd_ref[0].astype(BF16)
    for xg_ref, gs_ref, y_ref, acc_ref in zip(xg_refs, gs_refs, y_refs, acc_refs):
        @pl.when(f == 0)
        def _():
            acc_ref[...] = jnp.zeros_like(acc_ref)

        xg = xg_ref[0]
        a = jnp.dot(xg, wg, preferred_element_type=F32)
        u = jnp.dot(xg, wu, preferred_element_type=F32)
        z = (a * jax.nn.sigmoid(a) * u).astype(BF16)
        acc_ref[...] += jnp.dot(z, wd, preferred_element_type=F32)

        @pl.when(f == nf - 1)
        def _():
            y_ref[0] = (acc_ref[...] * gs_ref[0]).astype(BF16)


def _expert_ffn(xgs, gss, w_gate, w_up, w_down):
    ne, _, d = xgs[0].shape
    ff = w_gate.shape[2]
    tf = 256
    n = len(xgs)
    tok = lambda a: pl.BlockSpec((1,) + a.shape[1:], lambda e, f: (e, 0, 0))
    return pl.pallas_call(
        functools.partial(_ffn_kernel, n_groups=n),
        grid=(ne, ff // tf),
        in_specs=[tok(a) for a in xgs] + [tok(a) for a in gss] + [
            pl.BlockSpec((1, d, tf), lambda e, f: (e, 0, f)),
            pl.BlockSpec((1, d, tf), lambda e, f: (e, 0, f)),
            pl.BlockSpec((1, tf, d), lambda e, f: (e, f, 0)),
        ],
        out_specs=[tok(a) for a in xgs],
        out_shape=[jax.ShapeDtypeStruct(a.shape, BF16) for a in xgs],
        scratch_shapes=[pltpu.VMEM(a.shape[1:], F32) for a in xgs],
        compiler_params=_cparams("parallel", "arbitrary"),
        name="moe_ffn",
    )(*xgs, *gss, w_gate, w_up, w_down)


def _combine_kernel(post_ref, y_ref, x_ref, gt_ref, o_ref, *, cap):
    post = post_ref[0]
    tl, ne = post.shape
    slot = lax.broadcasted_iota(jnp.int32, (tl, cap), 1)
    acc = jnp.zeros(o_ref.shape[1:], F32)
    for e in range(ne):
        onehot = jnp.where(post[:, e:e + 1] == slot, 1.0, 0.0).astype(BF16)
        acc = acc + jnp.dot(onehot, y_ref[e], preferred_element_type=F32)
    o_ref[0] = x_ref[0] + gt_ref[0] * acc


def _combine(pos_t, y, x, gate, cap):
    bsz, length, d = x.shape
    ne = y.shape[0]
    tl = _pick(length, (512, 256))
    return pl.pallas_call(
        functools.partial(_combine_kernel, cap=cap),
        grid=(bsz, length // tl),
        in_specs=[
            pl.BlockSpec((1, tl, ne), lambda b, i: (b, i, 0)),
            pl.BlockSpec((ne, cap, d), lambda b, i: (0, b, 0)),
            pl.BlockSpec((1, tl, d), lambda b, i: (b, i, 0)),
            pl.BlockSpec((1, 1, d), lambda b, i: (b, 0, 0)),
        ],
        out_specs=pl.BlockSpec((1, tl, d), lambda b, i: (b, i, 0)),
        out_shape=jax.ShapeDtypeStruct(x.shape, F32),
        compiler_params=_cparams("parallel", "arbitrary"),
        name="moe_combine",
    )(pos_t, y, x, gate.reshape(bsz, 1, d))


def _route(x, g, shift, scale, router_w):
    bsz, length, _ = x.shape
    ne = router_w.shape[1]
    cap = max(1, EC_CAPACITY_FACTOR * length // ne)
    h, aff = _router(x, g, shift, scale, router_w)
    pos = _select(aff.reshape(bsz * ne, length), cap).reshape(bsz, ne, length)
    xg, gs = _gather(h, pos, aff, cap)
    return xg, gs, pos.transpose(0, 2, 1), cap


def _moe(x, ctx, g, sh, sc, gate, csh, csc, cgate, router_w, w_gate, w_up, w_down):
    xg, gs, pos_t, cap = _route(x, g, sh, sc, router_w)
    if ctx is None:
        (y,) = _expert_ffn([xg], [gs], w_gate, w_up, w_down)
        return _combine(pos_t, y, x, gate, cap), None
    xg_c, gs_c, pos_tc, cap_c = _route(ctx, g, csh, csc, router_w)
    y, y_c = _expert_ffn([xg, xg_c], [gs, gs_c], w_gate, w_up, w_down)
    return _combine(pos_t, y, x, gate, cap), _combine(pos_tc, y_c, ctx, cgate, cap_c)


def kernel(x, c, ctx, c_ctx, mod_w, mod_b, norm_mix_g, norm_ffn_g, router_w, moe_w_gate, moe_w_up, moe_w_down, na_w_qkv, na_rpb, na_w_o, mla_w_in, mla_q_norm_g, mla_w_q_b, mla_kv_norm_g, mla_w_kv_b, mla_w_o, hy_w_in, hy_conv_w, hy_conv_b, hy_f_w1, hy_f_b1, hy_f_w2, hy_f_b2, hy_f_w3, hy_f_b3, hy_sin_freq, hy_skip, hy_w_o, ml_w_in, ml_conv_w, ml_conv_b, ml_gate_b, ml_out_norm_g, ml_w_o, final_norm_g):
    bsz, _, d = x.shape
    depth = mod_w.shape[0]
    n_mixers = 4
    cond = jnp.concatenate([c, c_ctx[None, :]], axis=0)
    cond = jnp.pad(cond, ((0, (-cond.shape[0]) % 8), (0, 0)))
    for i in range(depth):
        last = i == depth - 1
        mod = _modulation(cond, mod_w[i], mod_b[i])
        sh1, sc1, g1, sh2, sc2, g2 = [mod[:bsz, k * d:(k + 1) * d] for k in range(6)]
        bc = lambda v: jnp.broadcast_to(v[None, :], (bsz, d))
        csh1, csc1, cg1, csh2, csc2, cg2 = [bc(mod[bsz, k * d:(k + 1) * d]) for k in range(6)]
        if last:
            cg1 = None
        kind, j = i % n_mixers, i // n_mixers
        gm = norm_mix_g[i]
        if kind == 0:
            x, ctx_new = _na_mixer(x, ctx, gm, sh1, sc1, csh1, csc1, na_w_qkv[j], na_rpb[j], na_w_o[j], g1, cg1)
        elif kind == 1:
            x, ctx_new = _mla_mixer(x, ctx, gm, sh1, sc1, csh1, csc1, mla_w_in[j], mla_q_norm_g[j],
                                    mla_w_q_b[j], mla_kv_norm_g[j], mla_w_kv_b[j], mla_w_o[j], g1, cg1)
        elif kind == 2:
            x, ctx_new = _hyena_mixer(x, ctx, gm, sh1, sc1, csh1, csc1, hy_w_in[j], hy_conv_w[j], hy_conv_b[j],
                                      hy_f_w1[j], hy_f_b1[j], hy_f_w2[j], hy_f_b2[j], hy_f_w3[j], hy_f_b3[j],
                                      hy_sin_freq[j], hy_skip[j], hy_w_o[j], g1, cg1)
        else:
            x, ctx_new = _mlstm_mixer(x, ctx, gm, sh1, sc1, csh1, csc1, ml_w_in[j], ml_conv_w[j], ml_conv_b[j],
                                      ml_gate_b[j], ml_out_norm_g[j], ml_w_o[j], g1, cg1)
        if last:
            x, _ = _moe(x, None, norm_ffn_g[i], sh2, sc2, g2, None, None, None,
                        router_w[i], moe_w_gate[i], moe_w_up[i], moe_w_down[i])
        else:
            x, ctx = _moe(x, ctx_new, norm_ffn_g[i], sh2, sc2, g2, csh2, csc2, cg2,
                          router_w[i], moe_w_gate[i], moe_w_up[i], moe_w_down[i])
    return _final_norm(x, final_norm_g)
```

```python
import functools
import math

import numpy as np
import jax
import jax.numpy as jnp
from jax import lax
from jax.experimental import pallas as pl
from jax.experimental.pallas import tpu as pltpu

F32 = jnp.float32
BF16 = jnp.bfloat16
HIGHEST = lax.Precision.HIGHEST

V7X_VMEM_LIMIT_BYTES = 56 * 1024 * 1024
LANES = 128

GRID_W = 64
NORM_EPS = 1e-6
ROPE_BASE = 10000.0

NA_HEADS = 16
NA_HEAD_DIM = 64
NA_WIN_H = 8
NA_WIN_W = 16
NA_QROWS = 4
NA_KROWS = NA_QROWS - 1 + NA_WIN_H

MLA_HEADS = 16
MLA_Q_RANK = 384
MLA_KV_RANK = 256
MLA_NOPE_DIM = 64
MLA_ROPE_DIM = 32
MLA_V_DIM = 64

HY_EMB_DIM = 33
HY_DECAY_TARGET = 1e-2
HY_FAST_DECAY = 0.3
HY_SLOW_DECAY = 1.5
HY_MOD_SHIFT = 0.05

ML_HEADS = 8
ML_V_DIM = 128
ML_QK_DIM = 64
ML_CHUNK = 256

N_EXPERTS = 16
EC_CAPACITY_FACTOR = 2
SELECT_BISECTIONS = 64


def _cparams(*sem):
    return pltpu.CompilerParams(dimension_semantics=sem, vmem_limit_bytes=V7X_VMEM_LIMIT_BYTES)


def _pick(n, prefs):
    for p in prefs:
        if n % p == 0:
            return p
    return n


def _proj_kernel(x_ref, g_ref, sh_ref, sc_ref, w_ref, o_ref, h_ref, *, norm):
    @pl.when(pl.program_id(2) == 0)
    def _():
        x = x_ref[0].astype(F32)
        if norm:
            ms = jnp.mean(x * x, axis=-1, keepdims=True)
            x = x * lax.rsqrt(ms + NORM_EPS) * g_ref[...]
            x = x * (1.0 + sc_ref[0]) + sh_ref[0]
        h_ref[...] = x.astype(BF16)

    o_ref[0] = jnp.dot(h_ref[...], w_ref[...].astype(BF16),
                       preferred_element_type=F32).astype(o_ref.dtype)


def _proj(x, w, g=None, shift=None, scale=None, out_dtype=F32):
    bsz, length, kdim = x.shape
    n = w.shape[1]
    norm = g is not None
    if not norm:
        g = jnp.ones((kdim,), F32)
    if shift is None:
        shift = jnp.zeros((bsz, kdim), F32)
        scale = jnp.zeros((bsz, kdim), F32)
    tm = _pick(length, (1024, 512, 256))
    tn = _pick(n, (512, 384, 256, 128))
    return pl.pallas_call(
        functools.partial(_proj_kernel, norm=norm),
        grid=(bsz, length // tm, n // tn),
        in_specs=[
            pl.BlockSpec((1, tm, kdim), lambda b, i, j: (b, i, 0)),
            pl.BlockSpec((1, kdim), lambda b, i, j: (0, 0)),
            pl.BlockSpec((1, 1, kdim), lambda b, i, j: (b, 0, 0)),
            pl.BlockSpec((1, 1, kdim), lambda b, i, j: (b, 0, 0)),
            pl.BlockSpec((kdim, tn), lambda b, i, j: (0, j)),
        ],
        out_specs=pl.BlockSpec((1, tm, tn), lambda b, i, j: (b, i, j)),
        out_shape=jax.ShapeDtypeStruct((bsz, length, n), out_dtype),
        scratch_shapes=[pltpu.VMEM((tm, kdim), BF16)],
        compiler_params=_cparams("parallel", "parallel", "arbitrary"),
        name="proj",
    )(x, g.reshape(1, kdim).astype(F32), shift.reshape(bsz, 1, kdim), scale.reshape(bsz, 1, kdim),
      w.astype(BF16))


def _proj_res_kernel(y_ref, w_ref, r_ref, gt_ref, o_ref):
    acc = jnp.dot(y_ref[0].astype(BF16), w_ref[...].astype(BF16), preferred_element_type=F32)
    o_ref[0] = r_ref[0] + gt_ref[0] * acc


def _proj_res(y, w, res, gate):
    bsz, length, kdim = y.shape
    n = w.shape[1]
    tm = _pick(length, (1024, 512, 256))
    tn = _pick(n, (512, 256, 128))
    return pl.pallas_call(
        _proj_res_kernel,
        grid=(bsz, length // tm, n // tn),
        in_specs=[
            pl.BlockSpec((1, tm, kdim), lambda b, i, j: (b, i, 0)),
            pl.BlockSpec((kdim, tn), lambda b, i, j: (0, j)),
            pl.BlockSpec((1, tm, tn), lambda b, i, j: (b, i, j)),
            pl.BlockSpec((1, 1, tn), lambda b, i, j: (b, 0, j)),
        ],
        out_specs=pl.BlockSpec((1, tm, tn), lambda b, i, j: (b, i, j)),
        out_shape=jax.ShapeDtypeStruct((bsz, length, n), F32),
        compiler_params=_cparams("parallel", "parallel", "arbitrary"),
        name="proj_res",
    )(y, w.astype(BF16), res, gate.reshape(bsz, 1, n))


def _mod_kernel(c_ref, w_ref, b_ref, o_ref):
    c = c_ref[...]
    s = c * jax.nn.sigmoid(c)
    o_ref[0] = jnp.dot(s.astype(BF16), w_ref[0].astype(BF16), preferred_element_type=F32) + b_ref[0]


def _modulation(cc, w, b):
    rows, d = cc.shape
    depth, _, n = w.shape
    tn = 512
    return pl.pallas_call(
        _mod_kernel,
        grid=(depth, n // tn),
        in_specs=[
            pl.BlockSpec((rows, d), lambda l, j: (0, 0)),
            pl.BlockSpec((1, d, tn), lambda l, j: (l, 0, j)),
            pl.BlockSpec((1, 1, tn), lambda l, j: (l, 0, j)),
        ],
        out_specs=pl.BlockSpec((1, rows, tn), lambda l, j: (l, 0, j)),
        out_shape=jax.ShapeDtypeStruct((depth, rows, n), F32),
        compiler_params=_cparams("parallel", "parallel"),
        name="modulation",
    )(cc, w, b.reshape(depth, 1, n))


def _final_norm_kernel(x_ref, g_ref, o_ref):
    x = x_ref[0]
    ms = jnp.mean(x * x, axis=-1, keepdims=True)
    o_ref[0] = x * lax.rsqrt(ms + NORM_EPS) * g_ref[...]


def _final_norm(x, g):
    bsz, length, d = x.shape
    tm = 1024
    return pl.pallas_call(
        _final_norm_kernel,
        grid=(bsz, length // tm),
        in_specs=[pl.BlockSpec((1, tm, d), lambda b, i: (b, i, 0)),
                  pl.BlockSpec((1, d), lambda b, i: (0, 0))],
        out_specs=pl.BlockSpec((1, tm, d), lambda b, i: (b, i, 0)),
        out_shape=jax.ShapeDtypeStruct(x.shape, F32),
        compiler_params=_cparams("parallel", "parallel"),
        name="final_norm",
    )(x, g.reshape(1, d))


ATTN_KEY_CHUNK = 512


def _softmax_pv(s_parts, v_parts):
    m = den = acc = None
    for s, v in zip(s_parts, v_parts):
        m_part = s.max(axis=-1, keepdims=True)
        m_new = m_part if m is None else jnp.maximum(m, m_part)
        p = jnp.exp(s - m_new)
        l = p.sum(axis=-1, keepdims=True)
        o = jnp.dot(p.astype(BF16), v, preferred_element_type=F32)
        if m is None:
            den, acc = l, o
        else:
            alpha = jnp.exp(m - m_new)
            den, acc = alpha * den + l, alpha * acc + o
        m = m_new
    return acc / den


def _key_chunks(n):
    step = ATTN_KEY_CHUNK if n % ATTN_KEY_CHUNK == 0 else n
    return [slice(c, c + step) for c in range(0, n, step)]


HEAD_V = 64


def _pair_queries(q2, hw):
    if hw == LANES:
        return [q2[:, :LANES], q2[:, LANES:]]
    lane = lax.broadcasted_iota(jnp.int32, (1, LANES), 1)
    return [jnp.where(lane < hw, q2, jnp.zeros_like(q2)), jnp.where(lane >= hw, q2, jnp.zeros_like(q2))]


def _pair_keys(k_ref, hw, j, rows=None):
    rows = slice(None) if rows is None else rows
    return k_ref[0, rows, j * LANES:(j + 1) * LANES] if hw == LANES else k_ref[0, rows, :]


def _pair_select(o0, o1):
    lane = lax.broadcasted_iota(jnp.int32, (1, LANES), 1)
    return jnp.where(lane < HEAD_V, o0, o1)


def _pair_attn_kernel(*refs, nparts, hw):
    q_ref, o_ref = refs[0], refs[-1]
    k_refs = refs[1:1 + nparts]
    v_refs = refs[1 + nparts:1 + 2 * nparts]
    nt = (((1,), (1,)), ((), ()))
    outs = []
    for j, q in enumerate(_pair_queries(q_ref[0], hw)):
        s_parts, v_parts = [], []
        for k_ref, v_ref in zip(k_refs, v_refs):
            for rows in _key_chunks(k_ref.shape[1]):
                s_parts.append(lax.dot_general(q, _pair_keys(k_ref, hw, j, rows), nt,
                                               preferred_element_type=F32))
                v_parts.append(v_ref[0, rows, :])
        outs.append(_softmax_pv(s_parts, v_parts))
    o_ref[0] = _pair_select(*outs).astype(o_ref.dtype)


def _pair_attention(q_src, kv_srcs, hw, nh):
    q, q_off = q_src
    bsz, lq = q.shape[0], q.shape[1]
    tq = _pick(lq, (512, 256))
    qw = 2 * hw
    in_specs = [pl.BlockSpec((1, tq, qw), lambda b, h, i: (b, i, q_off // qw + h))]
    args = [q]
    for k_arr, k_off, _, _ in kv_srcs:
        in_specs.append(pl.BlockSpec((1, k_arr.shape[1], qw), lambda b, h, i, o=k_off // qw: (b, 0, o + h)))
        args.append(k_arr)
    for _, _, v_arr, v_off in kv_srcs:
        in_specs.append(pl.BlockSpec((1, v_arr.shape[1], LANES), lambda b, h, i, o=v_off // LANES: (b, 0, o + h)))
        args.append(v_arr)
    return pl.pallas_call(
        functools.partial(_pair_attn_kernel, nparts=len(kv_srcs), hw=hw),
        grid=(bsz, nh // 2, lq // tq),
        in_specs=in_specs,
        out_specs=pl.BlockSpec((1, tq, LANES), lambda b, h, i: (b, i, h)),
        out_shape=jax.ShapeDtypeStruct((bsz, lq, nh * HEAD_V), BF16),
        compiler_params=_cparams("parallel", "parallel", "arbitrary"),
        name="attention",
    )(*args)


def _na_bias_table(rpb, rows):
    nh, n_dr, n_dc = rpb.shape
    kh = min(NA_WIN_H, rows)
    nblk = rows // NA_QROWS
    w = GRID_W
    qc = np.arange(w)[:, None]
    kc = np.arange(w)[None, :]
    cs = np.clip(qc - NA_WIN_W // 2, 0, w - NA_WIN_W)
    col_ok = (kc >= cs) & (kc < cs + NA_WIN_W)
    dc = np.clip(kc - qc, -(NA_WIN_W - 1), NA_WIN_W - 1) + NA_WIN_W - 1
    pick_dc = (dc.reshape(1, w * w) == np.arange(n_dc)[:, None]).astype(np.float32)
    tile = jnp.dot(rpb.reshape(nh * n_dr, n_dc).astype(F32), jnp.asarray(pick_dc), precision=HIGHEST)
    tile = jnp.where(col_ok, tile.reshape(nh, n_dr, w, w), -1e30)
    masked = jnp.full((nh, w, w), -1e30, F32)
    tables = []
    for blk in (0, 1, nblk - 1):
        start = int(np.clip(blk * NA_QROWS - NA_WIN_H // 2, 0, rows - NA_KROWS))
        q_rows = []
        for qr in range(NA_QROWS):
            r = blk * NA_QROWS + qr
            rs = int(np.clip(r - kh // 2, 0, rows - kh))
            tiles = [tile[:, kr - r + NA_WIN_H - 1] if rs <= kr < rs + kh else masked
                     for kr in range(start, start + NA_KROWS)]
            q_rows.append(jnp.concatenate(tiles, axis=-1))
        tables.append(jnp.concatenate(q_rows, axis=1))
    return jnp.stack(tables, axis=1)


def _na_kernel(q_ref, kl_ref, vl_ref, kc_ref, vc_ref, bias_ref, o_ref, *, nblk, rows):
    i = pl.program_id(2)
    start = jnp.clip(i * NA_QROWS - NA_WIN_H // 2, 0, rows - NA_KROWS) * GRID_W
    start = pl.multiple_of(start, GRID_W)
    pat = jnp.where(i == 0, 0, jnp.where(i == nblk - 1, 2, 1))
    local = pl.ds(start, NA_KROWS * GRID_W)
    nt = (((1,), (1,)), ((), ()))
    hw = NA_HEAD_DIM
    outs = []
    for j, q in enumerate(_pair_queries(q_ref[0], hw)):
        s_ctx = lax.dot_general(q, _pair_keys(kc_ref, hw, j), nt, preferred_element_type=F32)
        s_loc = (lax.dot_general(q, _pair_keys(kl_ref, hw, j, local), nt, preferred_element_type=F32)
                 + bias_ref[j, pat])
        outs.append(_softmax_pv([s_ctx, s_loc], [vc_ref[0], vl_ref[0, local, :]]))
    o_ref[0] = _pair_select(*outs).astype(o_ref.dtype)


def _na_mixer(x, ctx, g, sh, sc, csh, csc, w_qkv, rpb, w_o, gate, cgate):
    bsz, length, d = x.shape
    lc = ctx.shape[1]
    rows = length // GRID_W
    nblk = rows // NA_QROWS
    nh, hd = NA_HEADS, NA_HEAD_DIM
    w = jnp.concatenate([w_qkv[:, :d] * hd ** -0.5, w_qkv[:, d:]], axis=1).astype(BF16)
    qkv = _proj(x, w, g, sh, sc, out_dtype=BF16)
    qkv_c = _proj(ctx, w, g, csh, csc, out_dtype=BF16)
    bias = _na_bias_table(rpb, rows)
    tq = NA_QROWS * GRID_W
    npair = nh // 2
    pw = 2 * hd
    y = pl.pallas_call(
        functools.partial(_na_kernel, nblk=nblk, rows=rows),
        grid=(npair, bsz, nblk),
        in_specs=[
            pl.BlockSpec((1, tq, pw), lambda h, b, i: (b, i, h)),
            pl.BlockSpec((1, length, pw), lambda h, b, i: (b, 0, npair + h)),
            pl.BlockSpec((1, length, pw), lambda h, b, i: (b, 0, 2 * npair + h)),
            pl.BlockSpec((1, lc, pw), lambda h, b, i: (b, 0, npair + h)),
            pl.BlockSpec((1, lc, pw), lambda h, b, i: (b, 0, 2 * npair + h)),
            pl.BlockSpec((2, 3, tq, NA_KROWS * GRID_W), lambda h, b, i: (h, 0, 0, 0)),
        ],
        out_specs=pl.BlockSpec((1, tq, pw), lambda h, b, i: (b, i, h)),
        out_shape=jax.ShapeDtypeStruct((bsz, length, d), BF16),
        compiler_params=_cparams("parallel", "parallel", "arbitrary"),
        name="na_local",
    )(qkv, qkv, qkv, qkv_c, qkv_c, bias)
    if cgate is None:
        return _proj_res(y, w_o, x, gate), None
    yc = _pair_attention((qkv_c, 0), [(qkv_c, d, qkv_c, 2 * d)], hd, nh)
    return _proj_res(y, w_o, x, gate), _proj_res(yc, w_o, ctx, cgate)


def _axial_rope(length):
    t = jnp.arange(length)
    row = (t // GRID_W).astype(F32)
    col = (t % GRID_W).astype(F32)
    n_freq = MLA_ROPE_DIM // 4
    inv = ROPE_BASE ** (-jnp.arange(n_freq, dtype=F32) / n_freq)
    ang = jnp.concatenate([row[:, None] * inv, col[:, None] * inv], axis=-1)
    return jnp.cos(ang), jnp.sin(ang)


def _rot_half_cols(w):
    half = w.shape[-1] // 2
    return jnp.concatenate([-w[..., half:], w[..., :half]], axis=-1)


def _mla_weights(w_in, w_q_b, w_kv_b):
    nh, nope, rope = MLA_HEADS, MLA_NOPE_DIM, MLA_ROPE_DIM
    off = MLA_Q_RANK + MLA_KV_RANK
    w_in_ext = jnp.concatenate([w_in, _rot_half_cols(w_in[:, off:])], axis=1)
    rq, rkv = w_q_b.shape[0], w_kv_b.shape[0]
    wq = w_q_b.reshape(rq, nh, nope + rope)
    pad = jnp.zeros((rq, nh, LANES - nope - rope), F32)
    wq_pad = jnp.concatenate([wq, pad], axis=-1).reshape(rq, nh * LANES)
    wq_rot = jnp.concatenate([jnp.zeros((rq, nh, nope), F32), _rot_half_cols(wq[..., nope:]), pad],
                             axis=-1).reshape(rq, nh * LANES)
    wkv = w_kv_b.reshape(rkv, nh, nope + MLA_V_DIM)
    wk_pad = jnp.concatenate([wkv[..., :nope], jnp.zeros((rkv, nh, LANES - nope), F32)],
                             axis=-1).reshape(rkv, nh * LANES)
    wv = wkv[..., nope:].reshape(rkv, nh * MLA_V_DIM)
    place = np.zeros((rope, nh, LANES), np.float32)
    place[np.arange(rope), :, nope + np.arange(rope)] = 1.0
    place = jnp.asarray(place.reshape(rope, nh * LANES))
    return (w_in_ext,) + tuple(a.astype(BF16) for a in (wq_pad, wq_rot, wk_pad, wv, place))


def _mla_rope_tables(length, rope_on):
    if rope_on:
        cos, sin = _axial_rope(length)
    else:
        cos = jnp.ones((length, MLA_ROPE_DIM // 2), F32)
        sin = jnp.zeros((length, MLA_ROPE_DIM // 2), F32)
    ck = jnp.concatenate([cos, cos], axis=-1)
    sk = jnp.concatenate([sin, sin], axis=-1)
    one = jnp.ones((length, MLA_NOPE_DIM), F32)
    tail = LANES - MLA_NOPE_DIM - MLA_ROPE_DIM
    cq = jnp.concatenate([one, ck, one[:, :tail]], axis=-1)
    sq = jnp.concatenate([0.0 * one, sk, 0.0 * one[:, :tail]], axis=-1)
    return cq, sq, ck, sk


def _mla_qkv_kernel(z_ref, gq_ref, gkv_ref, wq_ref, wqr_ref, wk_ref, wv_ref, place_ref,
                    cq_ref, sq_ref, ck_ref, sk_ref, q_ref, k_ref, v_ref):
    z = z_ref[0]
    off = MLA_Q_RANK + MLA_KV_RANK
    rope = MLA_ROPE_DIM

    def norm(a, g_ref):
        ms = jnp.mean(a * a, axis=-1, keepdims=True)
        return (a * lax.rsqrt(ms + NORM_EPS) * g_ref[...]).astype(BF16)

    cq = norm(z[:, :MLA_Q_RANK], gq_ref)
    ckv = norm(z[:, MLA_Q_RANK:off], gkv_ref)
    k_rope = (z[:, off:off + rope] * ck_ref[...] + z[:, off + rope:off + 2 * rope] * sk_ref[...]).astype(BF16)
    qa = jnp.dot(cq, wq_ref[...], preferred_element_type=F32)
    qb = jnp.dot(cq, wqr_ref[...], preferred_element_type=F32)
    cos_q = cq_ref[...]
    sin_q = sq_ref[...]
    for h in range(MLA_HEADS):
        cols = slice(h * LANES, (h + 1) * LANES)
        q_ref[0, :, cols] = (qa[:, cols] * cos_q + qb[:, cols] * sin_q).astype(BF16)
    k_ref[0] = (jnp.dot(ckv, wk_ref[...], preferred_element_type=F32)
                + jnp.dot(k_rope, place_ref[...], preferred_element_type=F32)).astype(BF16)
    v_ref[0] = jnp.dot(ckv, wv_ref[...], preferred_element_type=F32).astype(BF16)


def _mla_qkv(z, q_norm_g, kv_norm_g, weights, tables):
    bsz, length, zw = z.shape
    wq, wqr, wk, wv, place = weights
    nh = MLA_HEADS
    tm = _pick(length, (512, 256))
    full = lambda a: pl.BlockSpec(a.shape, lambda b, i: (0, 0))
    rows = lambda a: pl.BlockSpec((tm, a.shape[1]), lambda b, i: (i, 0))
    gq = q_norm_g.reshape(1, -1)
    gkv = kv_norm_g.reshape(1, -1)
    out = lambda w: pl.BlockSpec((1, tm, w), lambda b, i: (b, i, 0))
    return pl.pallas_call(
        _mla_qkv_kernel,
        grid=(bsz, length // tm),
        in_specs=[pl.BlockSpec((1, tm, zw), lambda b, i: (b, i, 0)), full(gq), full(gkv),
                  full(wq), full(wqr), full(wk), full(wv), full(place)] + [rows(t) for t in tables],
        out_specs=[out(nh * LANES), out(nh * LANES), out(nh * MLA_V_DIM)],
        out_shape=[jax.ShapeDtypeStruct((bsz, length, nh * LANES), BF16),
                   jax.ShapeDtypeStruct((bsz, length, nh * LANES), BF16),
                   jax.ShapeDtypeStruct((bsz, length, nh * MLA_V_DIM), BF16)],
        compiler_params=_cparams("parallel", "parallel"),
        name="mla_qkv",
    )(z, gq, gkv, wq, wqr, wk, wv, place, *tables)


def _mla_mixer(x, ctx, g, sh, sc, csh, csc, w_in, q_norm_g, w_q_b, kv_norm_g, w_kv_b, w_o, gate, cgate):
    nh = MLA_HEADS
    scale = (MLA_NOPE_DIM + MLA_ROPE_DIM) ** -0.5
    w_in_ext, *weights = _mla_weights(w_in, w_q_b, w_kv_b)

    def project(u, shift, scl, rope_on):
        z = _proj(u, w_in_ext, g, shift, scl)
        cos_q, sin_q, cos_k, sin_k = _mla_rope_tables(u.shape[1], rope_on)
        return _mla_qkv(z, q_norm_g, kv_norm_g, weights, (cos_q * scale, sin_q * scale, cos_k, sin_k))

    ql, kl, vl = project(x, sh, sc, True)
    qc, kc, vc = project(ctx, csh, csc, False)
    y = _pair_attention((ql, 0), [(kc, 0, vc, 0), (kl, 0, vl, 0)], LANES, nh)
    if cgate is None:
        return _proj_res(y, w_o, x, gate), None
    yc = _pair_attention((qc, 0), [(kc, 0, vc, 0)], LANES, nh)
    return _proj_res(y, w_o, x, gate), _proj_res(yc, w_o, ctx, cgate)


def _conv3(z, w_ref, b_ref):
    length = z.shape[0]
    t = lax.broadcasted_iota(jnp.int32, z.shape, 0)
    prev = jnp.where(t == 0, 0.0, pltpu.roll(z, 1, axis=0))
    nxt = jnp.where(t == length - 1, 0.0, pltpu.roll(z, length - 1, axis=0))
    return b_ref[...] + prev * w_ref[0:1, :] + z * w_ref[1:2, :] + nxt * w_ref[2:3, :]


def _hy_conv_kernel(z0_ref, z1_ref, z2_ref, w0_ref, w1_ref, w2_ref, b0_ref, b1_ref, b2_ref,
                    x0_ref, gg_ref):
    x0_ref[0] = _conv3(z0_ref[0], w0_ref, b0_ref)
    x1 = _conv3(z1_ref[0], w1_ref, b1_ref)
    v = _conv3(z2_ref[0], w2_ref, b2_ref)
    gg_ref[0] = v * x1


def _hy_conv(z, conv_w, conv_b, d):
    bsz, length, _ = z.shape
    tc = 256
    nb = d // tc
    zspec = lambda k: pl.BlockSpec((1, length, tc), lambda b, j: (b, 0, k * nb + j))
    wspec = lambda k: pl.BlockSpec((3, tc), lambda b, j: (0, k * nb + j))
    bspec = lambda k: pl.BlockSpec((1, tc), lambda b, j: (0, k * nb + j))
    ospec = pl.BlockSpec((1, length, tc), lambda b, j: (b, 0, j))
    cb = conv_b.reshape(1, 3 * d)
    return pl.pallas_call(
        _hy_conv_kernel,
        grid=(bsz, nb),
        in_specs=[zspec(0), zspec(1), zspec(2), wspec(0), wspec(1), wspec(2), bspec(0), bspec(1), bspec(2)],
        out_specs=[ospec, ospec],
        out_shape=[jax.ShapeDtypeStruct((bsz, length, d), F32)] * 2,
        compiler_params=_cparams("parallel", "parallel"),
        name="hy_conv",
    )(z, z, z, conv_w, conv_w, conv_w, cb, cb, cb)


def _dft_tables(length):
    n2 = 2 * length
    f = jnp.arange(length, dtype=jnp.int32)[:, None]
    n = jnp.arange(length, dtype=jnp.int32)[None, :]
    ang = ((f * n) % n2).astype(F32) * (2.0 * math.pi / n2)
    cf = jnp.cos(ang)
    sgn_n = jnp.where(n % 2 == 0, 1.0, -1.0).astype(F32)
    sf = jnp.where(f == 0, sgn_n, -jnp.sin(ang))
    return cf.astype(BF16), sf.astype(BF16), sf.T.astype(BF16)


def _hy_fwd_kernel(cf_ref, sf_ref, gg_ref, kre_ref, kim_ref, y_ref, gb_ref):
    j = pl.program_id(1)

    @pl.when(j == 0)
    def _():
        gb_ref[...] = gg_ref[0].astype(BF16)

    ure = jnp.dot(cf_ref[...], gb_ref[...], preferred_element_type=F32)
    uim = jnp.dot(sf_ref[...], gb_ref[...], preferred_element_type=F32)
    kre = kre_ref[...]
    kim = kim_ref[...]
    tf = ure.shape[0]
    row = lax.broadcasted_iota(jnp.int32, ure.shape, 0) + j * tf
    packed = row == 0
    yre = ure * kre - jnp.where(packed, 0.0, uim * kim)
    yim = uim * jnp.where(packed, kim, kre) + jnp.where(packed, 0.0, ure * kim)
    y_ref[0, 0] = yre.astype(BF16)
    y_ref[0, 1] = yim.astype(BF16)


def _hy_inv_kernel(ci_ref, si_ref, y_ref, gg_ref, x0_ref, skip_ref, o_ref):
    y = (jnp.dot(ci_ref[...], y_ref[0, 0], preferred_element_type=F32)
         + jnp.dot(si_ref[...], y_ref[0, 1], preferred_element_type=F32))
    o_ref[0] = ((y + gg_ref[0] * skip_ref[...]) * x0_ref[0]).astype(o_ref.dtype)


def _hy_long_conv_gate(gg, x0, kre, kim, skip, tables):
    bsz, length, d = gg.shape
    cf, sf, sft = tables
    tf = 256
    y = pl.pallas_call(
        _hy_fwd_kernel,
        grid=(bsz, length // tf),
        in_specs=[
            pl.BlockSpec((tf, length), lambda b, j: (j, 0)),
            pl.BlockSpec((tf, length), lambda b, j: (j, 0)),
            pl.BlockSpec((1, length, d), lambda b, j: (b, 0, 0)),
            pl.BlockSpec((tf, d), lambda b, j: (j, 0)),
            pl.BlockSpec((tf, d), lambda b, j: (j, 0)),
        ],
        out_specs=pl.BlockSpec((1, 2, tf, d), lambda b, j: (b, 0, j, 0)),
        out_shape=jax.ShapeDtypeStruct((bsz, 2, length, d), BF16),
        scratch_shapes=[pltpu.VMEM((length, d), BF16)],
        compiler_params=_cparams("parallel", "arbitrary"),
        name="hy_dft_fwd",
    )(cf, sf, gg, kre, kim)
    tt = 256
    return pl.pallas_call(
        _hy_inv_kernel,
        grid=(bsz, length // tt),
        in_specs=[
            pl.BlockSpec((tt, length), lambda b, i: (i, 0)),
            pl.BlockSpec((tt, length), lambda b, i: (i, 0)),
            pl.BlockSpec((1, 2, length, d), lambda b, i: (b, 0, 0, 0)),
            pl.BlockSpec((1, tt, d), lambda b, i: (b, i, 0)),
            pl.BlockSpec((1, tt, d), lambda b, i: (b, i, 0)),
            pl.BlockSpec((1, d), lambda b, i: (0, 0)),
        ],
        out_specs=pl.BlockSpec((1, tt, d), lambda b, i: (b, i, 0)),
        out_shape=jax.ShapeDtypeStruct((bsz, length, d), BF16),
        compiler_params=_cparams("parallel", "arbitrary"),
        name="hy_dft_inv",
    )(cf, sft, y, gg, x0, skip.reshape(1, d))


def _hyena_filter(length, w1, b1, w2, b2, w3, b3, sin_freq):
    t = jnp.linspace(0.0, 1.0, length, dtype=F32)[:, None]
    bands = (HY_EMB_DIM - 1) // 2
    w = (2.0 * math.pi / length) * jnp.arange(length, dtype=F32)[:, None]
    f = jnp.linspace(1e-4, bands - 1, bands, dtype=F32)[None, :]
    z = jnp.concatenate([t, jnp.cos(f * w), -jnp.sin(f * w)], axis=-1)
    hp = lax.Precision.HIGHEST
    h = jnp.sin(sin_freq[0] * (jnp.dot(z, w1, precision=hp) + b1))
    h = jnp.sin(sin_freq[1] * (jnp.dot(h, w2, precision=hp) + b2))
    h = jnp.dot(h, w3, precision=hp) + b3
    max_decay = math.log(HY_DECAY_TARGET) / HY_FAST_DECAY
    min_decay = math.log(HY_DECAY_TARGET) / HY_SLOW_DECAY
    deltas = jnp.abs(jnp.linspace(min_decay, max_decay, h.shape[-1] // 2, dtype=F32))
    deltas = jnp.tile(deltas, 2)
    return h * (jnp.exp(-t * deltas) + HY_MOD_SHIFT)


def _hy_filter_spectrum(filt, d, tables):
    length = filt.shape[0]
    cf, sf, _ = tables
    hf, hb = filt[:, :d], filt[:, d:]
    ka = hf
    kb = jnp.concatenate([jnp.zeros((1, d), F32), hb[1:][::-1]], axis=0)
    tab = jnp.concatenate([cf, sf], axis=0)[None]
    spec = _proj(tab, jnp.concatenate([ka, kb], axis=1))[0]
    sgn_f = jnp.where(jnp.arange(length) % 2 == 0, 1.0, -1.0).astype(F32)[:, None]
    wgt = jnp.where(jnp.arange(length) == 0, 1.0, 2.0).astype(F32)[:, None] / (2 * length)
    kre = (spec[:length, :d] + sgn_f * spec[:length, d:]) * wgt
    kim = (spec[length:, :d] + sgn_f * spec[length:, d:]) * wgt
    return kre, kim


def _hyena_mixer(x, ctx, g, sh, sc, csh, csc, w_in, conv_w, conv_b, f_w1, f_b1, f_w2, f_b2, f_w3, f_b3,
                 sin_freq, skip, w_o, gate, cgate):
    d = x.shape[-1]

    def one(u, shift, scale, res, gt):
        length = u.shape[1]
        z = _proj(u, w_in, g, shift, scale)
        x0, gg = _hy_conv(z, conv_w, conv_b, d)
        tables = _dft_tables(length)
        filt = _hyena_filter(length, f_w1, f_b1, f_w2, f_b2, f_w3, f_b3, sin_freq)
        kre, kim = _hy_filter_spectrum(filt, d, tables)
        y = _hy_long_conv_gate(gg, x0, kre, kim, skip, tables)
        return _proj_res(y, w_o, res, gt)

    return one(x, sh, sc, x, gate), (one(ctx, csh, csc, ctx, cgate) if cgate is not None else None)


def _ml_conv_kernel(z_ref, w_ref, b_ref, o_ref, *t_refs, out_scale):
    y = _conv3(z_ref[0], w_ref, b_ref)
    y = y * jax.nn.sigmoid(y) * out_scale
    o_ref[0] = y
    for t_ref in t_refs:
        t_ref[0] = y.T


def _ml_conv(z, conv_w, conv_b, col0, width, out_scale, transposed):
    bsz, length, _ = z.shape
    tc = 256
    j0 = col0 // tc
    out_specs = [pl.BlockSpec((1, length, tc), lambda b, j: (b, 0, j))]
    out_shape = [jax.ShapeDtypeStruct((bsz, length, width), F32)]
    if transposed:
        out_specs.append(pl.BlockSpec((1, tc, length), lambda b, j: (b, j, 0)))
        out_shape.append(jax.ShapeDtypeStruct((bsz, width, length), F32))
    return pl.pallas_call(
        functools.partial(_ml_conv_kernel, out_scale=out_scale),
        grid=(bsz, width // tc),
        in_specs=[pl.BlockSpec((1, length, tc), lambda b, j: (b, 0, j0 + j)),
                  pl.BlockSpec((3, tc), lambda b, j: (0, j0 + j)),
                  pl.BlockSpec((1, tc), lambda b, j: (0, j0 + j))],
        out_specs=out_specs,
        out_shape=out_shape,
        compiler_params=_cparams("parallel", "parallel"),
        name="ml_conv",
    )(z, conv_w, conv_b.reshape(1, -1))


def _ml_gates_kernel(x_ref, g_ref, sh_ref, sc_ref, w_ref, b_ref, o_ref, t_ref):
    x = x_ref[0]
    ms = jnp.mean(x * x, axis=-1, keepdims=True)
    h = x * lax.rsqrt(ms + NORM_EPS) * g_ref[...]
    h = h * (1.0 + sc_ref[0]) + sh_ref[0]
    gates = jnp.dot(h.astype(BF16), w_ref[...], preferred_element_type=F32) + b_ref[...]
    o_ref[0] = gates
    t_ref[0] = gates.T


def _ml_gates(x, g, shift, scale, w_gate, gate_b):
    bsz, length, d = x.shape
    pad = LANES - w_gate.shape[1]
    w = jnp.pad(w_gate, ((0, 0), (0, pad))).astype(BF16)
    b = jnp.pad(gate_b, (0, pad)).reshape(1, LANES)
    tm = _pick(length, (1024, 256))
    return pl.pallas_call(
        _ml_gates_kernel,
        grid=(bsz, length // tm),
        in_specs=[
            pl.BlockSpec((1, tm, d), lambda b_, i: (b_, i, 0)),
            pl.BlockSpec((1, d), lambda b_, i: (0, 0)),
            pl.BlockSpec((1, 1, d), lambda b_, i: (b_, 0, 0)),
            pl.BlockSpec((1, 1, d), lambda b_, i: (b_, 0, 0)),
            pl.BlockSpec((d, LANES), lambda b_, i: (0, 0)),
            pl.BlockSpec((1, LANES), lambda b_, i: (0, 0)),
        ],
        out_specs=[pl.BlockSpec((1, tm, LANES), lambda b_, i: (b_, i, 0)),
                   pl.BlockSpec((1, LANES, tm), lambda b_, i: (b_, 0, i))],
        out_shape=[jax.ShapeDtypeStruct((bsz, length, LANES), F32),
                   jax.ShapeDtypeStruct((bsz, LANES, length), F32)],
        compiler_params=_cparams("parallel", "parallel"),
        name="ml_gates",
    )(x, g.reshape(1, d), shift.reshape(bsz, 1, d), scale.reshape(bsz, 1, d), w, b)


def _log_sigmoid(x):
    return jnp.minimum(x, 0.0) - jnp.log1p(jnp.exp(-jnp.abs(x)))


def _ml_chunk_pair(q2, kt2, k2, v2, gates, i_rows, f_rows, f_lanes, state, reverse):
    ct2, n2, ms = state
    tlen = q2.shape[0]
    dk, dv = ML_QK_DIM, ML_V_DIM
    ti = lax.broadcasted_iota(jnp.int32, (tlen, tlen), 0)
    si = lax.broadcasted_iota(jnp.int32, (tlen, tlen), 1)
    causal = (si >= ti) if reverse else (si <= ti)
    causal_t = (ti >= si) if reverse else (ti <= si)
    lane = lax.broadcasted_iota(jnp.int32, (1, LANES), 1)
    sub = lax.broadcasted_iota(jnp.int32, (LANES, 1), 0)
    kt_b = kt2.astype(BF16)
    k_b = k2.astype(BF16)
    v_b = v2.astype(BF16)
    ct_b = ct2.astype(BF16)
    hs, ws, decays, m_news = [], [], [], []
    for j in range(2):
        own = (lane >= dk) if j else (lane < dk)
        lf_row = _log_sigmoid(f_rows[j])
        f_col = jnp.where(lane == f_lanes[j], gates, 0.0).sum(axis=-1, keepdims=True)
        lf_col = _log_sigmoid(f_col)
        b_col = jnp.where(causal, lf_row, 0.0).sum(axis=-1, keepdims=True)
        b_row = jnp.where(causal_t, lf_col, 0.0).sum(axis=0, keepdims=True)
        log_d = jnp.where(causal, b_col - b_row + i_rows[j], -jnp.inf)
        m_inter = b_col + ms[j]
        m_t = jnp.maximum(log_d.max(axis=-1, keepdims=True), m_inter)
        qj = jnp.where(own, q2, 0.0)
        qj_b = qj.astype(BF16)
        s = jnp.dot(qj_b, kt_b, preferred_element_type=F32) * jnp.exp(log_d - m_t)
        inter = jnp.exp(m_inter - m_t)
        sv = jnp.dot(s.astype(BF16), v_b, preferred_element_type=F32)[:, j * dv:(j + 1) * dv]
        num = sv + inter * jnp.dot(qj_b, ct_b, preferred_element_type=F32)
        qn = s.sum(axis=-1, keepdims=True) + inter * (qj * n2).sum(axis=-1, keepdims=True)
        hs.append(num / jnp.maximum(jnp.abs(qn), jnp.exp(-m_t)))
        b_end = b_row[:, 0:1] if reverse else b_row[:, tlen - 1:tlen]
        w_log = b_end - b_row + i_rows[j]
        m_new = jnp.maximum(b_end + ms[j], w_log.max(axis=-1, keepdims=True))
        ws.append(jnp.exp(w_log - m_new))
        decays.append(jnp.exp(b_end + ms[j] - m_new))
        m_news.append(m_new)
    head0_rows = sub < dk
    head0_lanes = lane < dk
    kw = (kt2 * jnp.where(head0_rows, ws[0], ws[1])).astype(BF16)
    upd = jnp.dot(kw, v_b, preferred_element_type=F32)
    ct_new = (jnp.where(head0_rows, decays[0], decays[1]) * ct2
              + jnp.where(head0_rows, upd[:, :dv], upd[:, dv:]))
    wk = [jnp.dot(w.astype(BF16), k_b, preferred_element_type=F32) for w in ws]
    n_new = jnp.where(head0_lanes, decays[0], decays[1]) * n2 + jnp.where(head0_lanes, wk[0], wk[1])
    return jnp.concatenate(hs, axis=-1), (ct_new, n_new, tuple(m_news))


def _mlstm_kernel(ql_ref, kl_ref, ktl_ref, vl_ref, ol_ref, gl_ref, gtl_ref,
                  qc_ref, kc_ref, ktc_ref, vc_ref, oc_ref, gc_ref, gtc_ref, ng_ref,
                  yl_ref, yc_ref, hfl_ref, hbl_ref, hfc_ref, hbc_ref):
    tlen = ML_CHUNK
    dv, nh = ML_V_DIM, ML_HEADS
    n_lat = ql_ref.shape[1] // tlen
    n_ctx = qc_ref.shape[1] // tlen
    head0 = 2 * pl.program_id(1)

    def run(refs, c, state, direction):
        q_ref, k_ref, kt_ref, v_ref, g_ref, gt_ref, hf_ref, hb_ref = refs
        rows = pl.ds(pl.multiple_of(c * tlen, tlen), tlen)
        gate_i, gate_f = 2 * direction * nh, (2 * direction + 1) * nh
        i_rows = [gt_ref[0, pl.ds(gate_i + head0 + j, 1), rows] for j in range(2)]
        f_rows = [gt_ref[0, pl.ds(gate_f + head0 + j, 1), rows] for j in range(2)]
        f_lanes = [gate_f + head0 + j for j in range(2)]
        h, state = _ml_chunk_pair(q_ref[0, rows, :], kt_ref[0, :, rows], k_ref[0, rows, :], v_ref[0, rows, :],
                                  g_ref[0, rows, :], i_rows, f_rows, f_lanes, state, direction == 1)
        (hb_ref if direction else hf_ref)[rows, :] = h
        return state

    def sweep(refs, n_chunks, states):
        def body(c, sts):
            return run(refs, c, sts[0], 0), run(refs, n_chunks - 1 - c, sts[1], 1)
        return lax.fori_loop(0, n_chunks, body, states)

    zero = (jnp.zeros((LANES, dv), F32), jnp.zeros((1, LANES), F32),
            (jnp.zeros((1, 1), F32), jnp.zeros((1, 1), F32)))
    lat = (ql_ref, kl_ref, ktl_ref, vl_ref, gl_ref, gtl_ref, hfl_ref, hbl_ref)
    ctx = (qc_ref, kc_ref, ktc_ref, vc_ref, gc_ref, gtc_ref, hfc_ref, hbc_ref)
    sweep(lat, n_lat, sweep(ctx, n_ctx, (zero, zero)))

    def finish(hf_ref, hb_ref, o_ref, y_ref):
        for j in range(2):
            vs = slice(j * dv, (j + 1) * dv)
            h = hf_ref[:, vs] + hb_ref[:, vs]
            h = h * lax.rsqrt(jnp.mean(h * h, axis=-1, keepdims=True) + NORM_EPS) * ng_ref[:, vs]
            y_ref[0, :, vs] = (h * jax.nn.sigmoid(o_ref[0, :, vs])).astype(y_ref.dtype)

    finish(hfl_ref, hbl_ref, ol_ref, yl_ref)
    finish(hfc_ref, hbc_ref, oc_ref, yc_ref)


def _mlstm_mixer(x, ctx, g, sh, sc, csh, csc, w_in, conv_w, conv_b, gate_b, out_norm_g, w_o, gate, cgate):
    nh, dk, dv = ML_HEADS, ML_QK_DIM, ML_V_DIM
    nqk = 2 * nh * dk
    wide = nqk + 2 * nh * dv
    w_main = w_in[:, :wide].astype(BF16)
    w_gate = w_in[:, wide:]

    def project(u, shift, scale):
        z = _proj(u, w_main, g, shift, scale)
        gates, gates_t = _ml_gates(u, g, shift, scale, w_gate, gate_b)
        (q,) = _ml_conv(z, conv_w, conv_b, 0, nh * dk, 1.0, False)
        k, kt = _ml_conv(z, conv_w, conv_b, nh * dk, nh * dk, dk ** -0.5, True)
        return z, q, k, kt, gates, gates_t

    zl, ql, kl, ktl, gl, gtl = project(x, sh, sc)
    zc, qc, kc, ktc, gc, gtc = project(ctx, csh, csc)
    bsz, length, d = x.shape
    lc = ctx.shape[1]
    npair = nh // 2
    qw, vw = 2 * dk, 2 * dv
    v_off, o_off = nqk // vw, (nqk + nh * dv) // vw

    def specs(n):
        return [
            pl.BlockSpec((1, n, qw), lambda b, h: (b, 0, h)),
            pl.BlockSpec((1, n, qw), lambda b, h: (b, 0, h)),
            pl.BlockSpec((1, qw, n), lambda b, h: (b, h, 0)),
            pl.BlockSpec((1, n, vw), lambda b, h: (b, 0, v_off + h)),
            pl.BlockSpec((1, n, vw), lambda b, h: (b, 0, o_off + h)),
            pl.BlockSpec((1, n, LANES), lambda b, h: (b, 0, 0)),
            pl.BlockSpec((1, LANES, n), lambda b, h: (b, 0, 0)),
        ]

    yl, yc = pl.pallas_call(
        _mlstm_kernel,
        grid=(bsz, npair),
        in_specs=specs(length) + specs(lc) + [pl.BlockSpec((1, vw), lambda b, h: (0, h))],
        out_specs=[pl.BlockSpec((1, length, vw), lambda b, h: (b, 0, h)),
                   pl.BlockSpec((1, lc, vw), lambda b, h: (b, 0, h))],
        out_shape=[jax.ShapeDtypeStruct((bsz, length, nh * dv), BF16),
                   jax.ShapeDtypeStruct((bsz, lc, nh * dv), BF16)],
        scratch_shapes=[pltpu.VMEM((length, vw), F32), pltpu.VMEM((length, vw), F32),
                        pltpu.VMEM((lc, vw), F32), pltpu.VMEM((lc, vw), F32)],
        compiler_params=_cparams("parallel", "parallel"),
        name="mlstm",
    )(ql, kl, ktl, zl, zl, gl, gtl, qc, kc, ktc, zc, zc, gc, gtc, out_norm_g.reshape(1, nh * dv))
    return _proj_res(yl, w_o, x, gate), (_proj_res(yc, w_o, ctx, cgate) if cgate is not None else None)


def _router_kernel(x_ref, g_ref, sh_ref, sc_ref, rw_ref, h_ref, aff_ref):
    x = x_ref[0]
    ms = jnp.mean(x * x, axis=-1, keepdims=True)
    h = x * lax.rsqrt(ms + NORM_EPS) * g_ref[...]
    h = h * (1.0 + sc_ref[0]) + sh_ref[0]
    h_ref[0] = h.astype(BF16)
    logits = lax.dot_general(rw_ref[...], h, (((1,), (1,)), ((), ())), precision=HIGHEST,
                             preferred_element_type=F32)
    mx = logits.max(axis=0, keepdims=True)
    p = jnp.exp(logits - mx)
    aff_ref[0] = p / p.sum(axis=0, keepdims=True)


def _router(x, g, shift, scale, router_w):
    bsz, length, d = x.shape
    ne = router_w.shape[1]
    tm = _pick(length, (1024, 256))
    return pl.pallas_call(
        _router_kernel,
        grid=(bsz, length // tm),
        in_specs=[
            pl.BlockSpec((1, tm, d), lambda b, i: (b, i, 0)),
            pl.BlockSpec((1, d), lambda b, i: (0, 0)),
            pl.BlockSpec((1, 1, d), lambda b, i: (b, 0, 0)),
            pl.BlockSpec((1, 1, d), lambda b, i: (b, 0, 0)),
            pl.BlockSpec((ne, d), lambda b, i: (0, 0)),
        ],
        out_specs=[pl.BlockSpec((1, tm, d), lambda b, i: (b, i, 0)),
                   pl.BlockSpec((1, ne, tm), lambda b, i: (b, 0, i))],
        out_shape=[jax.ShapeDtypeStruct((bsz, length, d), BF16),
                   jax.ShapeDtypeStruct((bsz, ne, length), F32)],
        compiler_params=_cparams("parallel", "parallel"),
        name="moe_router",
    )(x, g.reshape(1, d), shift.reshape(bsz, 1, d), scale.reshape(bsz, 1, d), router_w.T)


def _excl_cumsum_lanes(flags):
    rows, length = flags.shape
    cw = min(length, 256)
    si = lax.broadcasted_iota(jnp.int32, (cw, cw), 0)
    ti = lax.broadcasted_iota(jnp.int32, (cw, cw), 1)
    upper = jnp.where(si < ti, 1.0, 0.0).astype(BF16)
    carry = jnp.zeros((rows, 1), F32)
    parts = []
    for c in range(length // cw):
        blk = flags[:, c * cw:(c + 1) * cw]
        parts.append(jnp.dot(blk.astype(BF16), upper, preferred_element_type=F32) + carry)
        carry = carry + blk.sum(axis=-1, keepdims=True)
    return parts[0] if len(parts) == 1 else jnp.concatenate(parts, axis=-1)


def _select_kernel(aff_ref, pos_ref, *, cap):
    a = aff_ref[...]
    rows = a.shape[0]
    capf = float(cap)

    def bisect(_, lohi):
        lo, hi = lohi
        mid = 0.5 * (lo + hi)
        cnt = jnp.where(a >= mid, 1.0, 0.0).sum(axis=-1, keepdims=True)
        ge = cnt >= capf
        return jnp.where(ge, mid, lo), jnp.where(ge, hi, mid)

    lo, hi = lax.fori_loop(0, SELECT_BISECTIONS, bisect,
                           (jnp.zeros((rows, 1), F32), jnp.full((rows, 1), 2.0, F32)))
    above = a >= hi
    tie = jnp.logical_and(a >= lo, jnp.logical_not(above))
    n_above = jnp.where(above, 1.0, 0.0).sum(axis=-1, keepdims=True)
    tie_rank = _excl_cumsum_lanes(jnp.where(tie, 1.0, 0.0))
    sel = jnp.logical_or(above, jnp.logical_and(tie, tie_rank < capf - n_above))
    slot = _excl_cumsum_lanes(jnp.where(sel, 1.0, 0.0))
    pos_ref[...] = jnp.where(sel, slot, -1.0).astype(jnp.int32)


def _select(aff2d, cap):
    return pl.pallas_call(
        functools.partial(_select_kernel, cap=cap),
        out_shape=jax.ShapeDtypeStruct(aff2d.shape, jnp.int32),
        compiler_params=pltpu.CompilerParams(vmem_limit_bytes=V7X_VMEM_LIMIT_BYTES),
        name="moe_select",
    )(aff2d)


def _gather_kernel(h_ref, pos_ref, aff_ref, xg_ref, gs_ref, *, cap):
    pos = pos_ref[0, 0]
    length = pos.shape[1]
    slot = lax.broadcasted_iota(jnp.int32, (cap, length), 0)
    hit = slot == pos
    onehot = jnp.where(hit, 1.0, 0.0).astype(BF16)
    xg_ref[0] = jnp.dot(onehot, h_ref[0], preferred_element_type=F32).astype(BF16)
    gs_ref[0] = jnp.where(hit, aff_ref[0, 0], 0.0).sum(axis=-1, keepdims=True)


def _gather(h, pos, aff, cap):
    bsz, length, d = h.shape
    ne = pos.shape[1]
    return pl.pallas_call(
        functools.partial(_gather_kernel, cap=cap),
        grid=(bsz, ne),
        in_specs=[
            pl.BlockSpec((1, length, d), lambda b, e: (b, 0, 0)),
            pl.BlockSpec((1, 1, 1, length), lambda b, e: (b, e, 0, 0)),
            pl.BlockSpec((1, 1, 1, length), lambda b, e: (b, e, 0, 0)),
        ],
        out_specs=[pl.BlockSpec((1, cap, d), lambda b, e: (e, b, 0)),
                   pl.BlockSpec((1, cap, 1), lambda b, e: (e, b, 0))],
        out_shape=[jax.ShapeDtypeStruct((ne, bsz * cap, d), BF16),
                   jax.ShapeDtypeStruct((ne, bsz * cap, 1), F32)],
        compiler_params=_cparams("parallel", "arbitrary"),
        name="moe_gather",
    )(h, pos.reshape(bsz, ne, 1, length), aff.reshape(bsz, ne, 1, length))


def _ffn_kernel(*refs, n_groups, n_up, tf):
    xg_refs = refs[:n_groups]
    gs_refs = refs[n_groups:2 * n_groups]
    wg_ref, wu_ref, wd_ref = refs[2 * n_groups:2 * n_groups + 3]
    y_refs = refs[2 * n_groups + 3:3 * n_groups + 3]
    z_refs = refs[3 * n_groups + 3:]
    step = pl.program_id(1)

    @pl.when(step < n_up)
    def _():
        wg = wg_ref[0, 0].astype(BF16)
        wu = wu_ref[0, 0].astype(BF16)
        for xg_ref, z_ref in zip(xg_refs, z_refs):
            xg = xg_ref[0]
            a = jnp.dot(xg, wg, preferred_element_type=F32)
            u = jnp.dot(xg, wu, preferred_element_type=F32)
            z = (a * jax.nn.sigmoid(a) * u).astype(BF16)
            for t in range(n_up):
                @pl.when(step == t)
                def _():
                    z_ref[:, t * tf:(t + 1) * tf] = z

    @pl.when(step >= n_up)
    def _():
        wd = wd_ref[0, 0].astype(BF16)
        for gs_ref, y_ref, z_ref in zip(gs_refs, y_refs, z_refs):
            y = jnp.dot(z_ref[...], wd, preferred_element_type=F32)
            y_ref[0] = (y * gs_ref[0]).astype(BF16)


def _expert_ffn(xgs, gss, w_gate, w_up, w_down, layer):
    ne, _, d = xgs[0].shape
    ff = w_gate.shape[3]
    tf = 256
    tn = 256
    n_up, n_down = ff // tf, d // tn
    n = len(xgs)
    tok = lambda a: pl.BlockSpec((1,) + a.shape[1:], lambda e, s: (e, 0, 0))
    up_tile = lambda e, s: (layer, e, 0, jnp.minimum(s, n_up - 1))
    down_tile = lambda e, s: (layer, e, 0, jnp.maximum(s - n_up, 0))
    return pl.pallas_call(
        functools.partial(_ffn_kernel, n_groups=n, n_up=n_up, tf=tf),
        grid=(ne, n_up + n_down),
        in_specs=[tok(a) for a in xgs] + [tok(a) for a in gss] + [
            pl.BlockSpec((1, 1, d, tf), up_tile),
            pl.BlockSpec((1, 1, d, tf), up_tile),
            pl.BlockSpec((1, 1, ff, tn), down_tile),
        ],
        out_specs=[pl.BlockSpec((1, a.shape[1], tn), lambda e, s: (e, 0, jnp.maximum(s - n_up, 0)))
                   for a in xgs],
        out_shape=[jax.ShapeDtypeStruct(a.shape, BF16) for a in xgs],
        scratch_shapes=[pltpu.VMEM((a.shape[1], ff), BF16) for a in xgs],
        compiler_params=_cparams("parallel", "arbitrary"),
        name="moe_ffn",
    )(*xgs, *gss, w_gate, w_up, w_down)


def _combine_kernel(post_ref, y_ref, x_ref, gt_ref, o_ref, *, cap):
    post = post_ref[0]
    tl, ne = post.shape
    slot = lax.broadcasted_iota(jnp.int32, (tl, cap), 1)
    acc = jnp.zeros(o_ref.shape[1:], F32)
    for e in range(ne):
        onehot = jnp.where(post[:, e:e + 1] == slot, 1.0, 0.0).astype(BF16)
        acc = acc + jnp.dot(onehot, y_ref[e], preferred_element_type=F32)
    o_ref[0] = x_ref[0] + gt_ref[0] * acc


def _combine(pos_t, y, x, gate, cap):
    bsz, length, d = x.shape
    ne = y.shape[0]
    tl = _pick(length, (512, 256))
    return pl.pallas_call(
        functools.partial(_combine_kernel, cap=cap),
        grid=(bsz, length // tl),
        in_specs=[
            pl.BlockSpec((1, tl, ne), lambda b, i: (b, i, 0)),
            pl.BlockSpec((ne, cap, d), lambda b, i: (0, b, 0)),
            pl.BlockSpec((1, tl, d), lambda b, i: (b, i, 0)),
            pl.BlockSpec((1, 1, d), lambda b, i: (b, 0, 0)),
        ],
        out_specs=pl.BlockSpec((1, tl, d), lambda b, i: (b, i, 0)),
        out_shape=jax.ShapeDtypeStruct(x.shape, F32),
        compiler_params=_cparams("parallel", "arbitrary"),
        name="moe_combine",
    )(pos_t, y, x, gate.reshape(bsz, 1, d))


def _route(x, g, shift, scale, router_w):
    bsz, length, _ = x.shape
    ne = router_w.shape[1]
    cap = max(1, EC_CAPACITY_FACTOR * length // ne)
    h, aff = _router(x, g, shift, scale, router_w)
    pos = _select(aff.reshape(bsz * ne, length), cap).reshape(bsz, ne, length)
    xg, gs = _gather(h, pos, aff, cap)
    return xg, gs, pos.transpose(0, 2, 1), cap


def _moe(x, ctx, g, sh, sc, gate, csh, csc, cgate, router_w, w_gate, w_up, w_down, layer):
    xg, gs, pos_t, cap = _route(x, g, sh, sc, router_w)
    if ctx is None:
        (y,) = _expert_ffn([xg], [gs], w_gate, w_up, w_down, layer)
        return _combine(pos_t, y, x, gate, cap), None
    xg_c, gs_c, pos_tc, cap_c = _route(ctx, g, csh, csc, router_w)
    y, y_c = _expert_ffn([xg, xg_c], [gs, gs_c], w_gate, w_up, w_down, layer)
    return _combine(pos_t, y, x, gate, cap), _combine(pos_tc, y_c, ctx, cgate, cap_c)


def kernel(x, c, ctx, c_ctx, mod_w, mod_b, norm_mix_g, norm_ffn_g, router_w, moe_w_gate, moe_w_up, moe_w_down, na_w_qkv, na_rpb, na_w_o, mla_w_in, mla_q_norm_g, mla_w_q_b, mla_kv_norm_g, mla_w_kv_b, mla_w_o, hy_w_in, hy_conv_w, hy_conv_b, hy_f_w1, hy_f_b1, hy_f_w2, hy_f_b2, hy_f_w3, hy_f_b3, hy_sin_freq, hy_skip, hy_w_o, ml_w_in, ml_conv_w, ml_conv_b, ml_gate_b, ml_out_norm_g, ml_w_o, final_norm_g):
    bsz, _, d = x.shape
    depth = mod_w.shape[0]
    n_mixers = 4
    cond = jnp.concatenate([c, c_ctx[None, :]], axis=0)
    cond = jnp.pad(cond, ((0, (-cond.shape[0]) % 8), (0, 0)))
    mods = _modulation(cond, mod_w, mod_b)
    for i in range(depth):
        last = i == depth - 1
        mod = mods[i]
        sh1, sc1, g1, sh2, sc2, g2 = [mod[:bsz, k * d:(k + 1) * d] for k in range(6)]
        bc = lambda v: jnp.broadcast_to(v[None, :], (bsz, d))
        csh1, csc1, cg1, csh2, csc2, cg2 = [bc(mod[bsz, k * d:(k + 1) * d]) for k in range(6)]
        if last:
            cg1 = None
        kind, j = i % n_mixers, i // n_mixers
        gm = norm_mix_g[i]
        if kind == 0:
            x, ctx_new = _na_mixer(x, ctx, gm, sh1, sc1, csh1, csc1, na_w_qkv[j], na_rpb[j], na_w_o[j], g1, cg1)
        elif kind == 1:
            x, ctx_new = _mla_mixer(x, ctx, gm, sh1, sc1, csh1, csc1, mla_w_in[j], mla_q_norm_g[j],
                                    mla_w_q_b[j], mla_kv_norm_g[j], mla_w_kv_b[j], mla_w_o[j], g1, cg1)
        elif kind == 2:
            x, ctx_new = _hyena_mixer(x, ctx, gm, sh1, sc1, csh1, csc1, hy_w_in[j], hy_conv_w[j], hy_conv_b[j],
                                      hy_f_w1[j], hy_f_b1[j], hy_f_w2[j], hy_f_b2[j], hy_f_w3[j], hy_f_b3[j],
                                      hy_sin_freq[j], hy_skip[j], hy_w_o[j], g1, cg1)
        else:
            x, ctx_new = _mlstm_mixer(x, ctx, gm, sh1, sc1, csh1, csc1, ml_w_in[j], ml_conv_w[j], ml_conv_b[j],
                                      ml_gate_b[j], ml_out_norm_g[j], ml_w_o[j], g1, cg1)
        if last:
            x, _ = _moe(x, None, norm_ffn_g[i], sh2, sc2, g2, None, None, None,
                        router_w[i], moe_w_gate, moe_w_up, moe_w_down, i)
        else:
            x, ctx = _moe(x, ctx_new, norm_ffn_g[i], sh2, sc2, g2, csh2, csc2, cg2,
                          router_w[i], moe_w_gate, moe_w_up, moe_w_down, i)
    return _final_norm(x, final_norm_g)
```

```python
import functools
import math

import numpy as np
import jax
import jax.numpy as jnp
from jax import lax
from jax.experimental import pallas as pl
from jax.experimental.pallas import tpu as pltpu

F32 = jnp.float32
BF16 = jnp.bfloat16
HIGHEST = lax.Precision.HIGHEST

V7X_VMEM_LIMIT_BYTES = 56 * 1024 * 1024
LANES = 128

GRID_W = 64
NORM_EPS = 1e-6
ROPE_BASE = 10000.0

NA_HEADS = 16
NA_HEAD_DIM = 64
NA_WIN_H = 8
NA_WIN_W = 16
NA_QROWS = 4
NA_KROWS = NA_QROWS - 1 + NA_WIN_H

MLA_HEADS = 16
MLA_Q_RANK = 384
MLA_KV_RANK = 256
MLA_NOPE_DIM = 64
MLA_ROPE_DIM = 32
MLA_V_DIM = 64

HY_EMB_DIM = 33
HY_DECAY_TARGET = 1e-2
HY_FAST_DECAY = 0.3
HY_SLOW_DECAY = 1.5
HY_MOD_SHIFT = 0.05

ML_HEADS = 8
ML_V_DIM = 128
ML_QK_DIM = 64
ML_CHUNK = 256

N_EXPERTS = 16
EC_CAPACITY_FACTOR = 2
SELECT_BISECTIONS = 64


def _cparams(*sem):
    return pltpu.CompilerParams(dimension_semantics=sem, vmem_limit_bytes=V7X_VMEM_LIMIT_BYTES)


def _pick(n, prefs):
    for p in prefs:
        if n % p == 0:
            return p
    return n


def _proj_kernel(x_ref, g_ref, sh_ref, sc_ref, w_ref, o_ref, h_ref, *, norm):
    @pl.when(pl.program_id(2) == 0)
    def _():
        x = x_ref[0].astype(F32)
        if norm:
            ms = jnp.mean(x * x, axis=-1, keepdims=True)
            x = x * lax.rsqrt(ms + NORM_EPS) * g_ref[...]
            x = x * (1.0 + sc_ref[0]) + sh_ref[0]
        h_ref[...] = x.astype(BF16)

    o_ref[0] = jnp.dot(h_ref[...], w_ref[...].astype(BF16),
                       preferred_element_type=F32).astype(o_ref.dtype)


def _proj(x, w, g=None, shift=None, scale=None, out_dtype=F32):
    bsz, length, kdim = x.shape
    n = w.shape[1]
    norm = g is not None
    if not norm:
        g = jnp.ones((kdim,), F32)
    if shift is None:
        shift = jnp.zeros((bsz, kdim), F32)
        scale = jnp.zeros((bsz, kdim), F32)
    tm = _pick(length, (1024, 512, 256))
    tn = _pick(n, (1024, 768, 512, 384, 256, 128))
    return pl.pallas_call(
        functools.partial(_proj_kernel, norm=norm),
        grid=(bsz, length // tm, n // tn),
        in_specs=[
            pl.BlockSpec((1, tm, kdim), lambda b, i, j: (b, i, 0)),
            pl.BlockSpec((1, kdim), lambda b, i, j: (0, 0)),
            pl.BlockSpec((1, 1, kdim), lambda b, i, j: (b, 0, 0)),
            pl.BlockSpec((1, 1, kdim), lambda b, i, j: (b, 0, 0)),
            pl.BlockSpec((kdim, tn), lambda b, i, j: (0, j)),
        ],
        out_specs=pl.BlockSpec((1, tm, tn), lambda b, i, j: (b, i, j)),
        out_shape=jax.ShapeDtypeStruct((bsz, length, n), out_dtype),
        scratch_shapes=[pltpu.VMEM((tm, kdim), BF16)],
        compiler_params=_cparams("parallel", "parallel", "arbitrary"),
        name="proj",
    )(x, g.reshape(1, kdim).astype(F32), shift.reshape(bsz, 1, kdim), scale.reshape(bsz, 1, kdim),
      w.astype(BF16))


def _proj_res_kernel(y_ref, w_ref, r_ref, gt_ref, o_ref):
    acc = jnp.dot(y_ref[0].astype(BF16), w_ref[...].astype(BF16), preferred_element_type=F32)
    o_ref[0] = r_ref[0] + gt_ref[0] * acc


def _proj_res(y, w, res, gate):
    bsz, length, kdim = y.shape
    n = w.shape[1]
    tm = _pick(length, (1024, 512, 256))
    tn = _pick(n, (1024, 512, 256, 128))
    return pl.pallas_call(
        _proj_res_kernel,
        grid=(bsz, length // tm, n // tn),
        in_specs=[
            pl.BlockSpec((1, tm, kdim), lambda b, i, j: (b, i, 0)),
            pl.BlockSpec((kdim, tn), lambda b, i, j: (0, j)),
            pl.BlockSpec((1, tm, tn), lambda b, i, j: (b, i, j)),
            pl.BlockSpec((1, 1, tn), lambda b, i, j: (b, 0, j)),
        ],
        out_specs=pl.BlockSpec((1, tm, tn), lambda b, i, j: (b, i, j)),
        out_shape=jax.ShapeDtypeStruct((bsz, length, n), F32),
        compiler_params=_cparams("parallel", "parallel", "arbitrary"),
        name="proj_res",
    )(y, w.astype(BF16), res, gate.reshape(bsz, 1, n))


def _mod_kernel(c_ref, w_ref, b_ref, o_ref):
    c = c_ref[...]
    s = c * jax.nn.sigmoid(c)
    o_ref[0] = jnp.dot(s.astype(BF16), w_ref[0].astype(BF16), preferred_element_type=F32) + b_ref[0]


def _modulation(cc, w, b):
    rows, d = cc.shape
    depth, _, n = w.shape
    tn = 512
    return pl.pallas_call(
        _mod_kernel,
        grid=(depth, n // tn),
        in_specs=[
            pl.BlockSpec((rows, d), lambda l, j: (0, 0)),
            pl.BlockSpec((1, d, tn), lambda l, j: (l, 0, j)),
            pl.BlockSpec((1, 1, tn), lambda l, j: (l, 0, j)),
        ],
        out_specs=pl.BlockSpec((1, rows, tn), lambda l, j: (l, 0, j)),
        out_shape=jax.ShapeDtypeStruct((depth, rows, n), F32),
        compiler_params=_cparams("parallel", "parallel"),
        name="modulation",
    )(cc, w, b.reshape(depth, 1, n))


ATTN_KEY_CHUNK = 512


def _softmax_pv(s_parts, v_parts):
    m = den = acc = None
    for s, v in zip(s_parts, v_parts):
        m_part = s.max(axis=-1, keepdims=True)
        m_new = m_part if m is None else jnp.maximum(m, m_part)
        p = jnp.exp(s - m_new)
        l = p.sum(axis=-1, keepdims=True)
        o = jnp.dot(p.astype(BF16), v, preferred_element_type=F32)
        if m is None:
            den, acc = l, o
        else:
            alpha = jnp.exp(m - m_new)
            den, acc = alpha * den + l, alpha * acc + o
        m = m_new
    return acc / den


def _key_chunks(n):
    step = ATTN_KEY_CHUNK if n % ATTN_KEY_CHUNK == 0 else n
    return [slice(c, c + step) for c in range(0, n, step)]


HEAD_V = 64


def _pair_queries(q2, hw):
    if hw == LANES:
        return [q2[:, :LANES], q2[:, LANES:]]
    lane = lax.broadcasted_iota(jnp.int32, (1, LANES), 1)
    return [jnp.where(lane < hw, q2, jnp.zeros_like(q2)), jnp.where(lane >= hw, q2, jnp.zeros_like(q2))]


def _pair_keys(k_ref, hw, j, rows=None):
    rows = slice(None) if rows is None else rows
    return k_ref[0, rows, j * LANES:(j + 1) * LANES] if hw == LANES else k_ref[0, rows, :]


def _pair_select(o0, o1):
    lane = lax.broadcasted_iota(jnp.int32, (1, LANES), 1)
    return jnp.where(lane < HEAD_V, o0, o1)


def _pair_attn_kernel(*refs, nparts, hw):
    q_ref, o_ref = refs[0], refs[-1]
    k_refs = refs[1:1 + nparts]
    v_refs = refs[1 + nparts:1 + 2 * nparts]
    nt = (((1,), (1,)), ((), ()))
    outs = []
    for j, q in enumerate(_pair_queries(q_ref[0], hw)):
        s_parts, v_parts = [], []
        for k_ref, v_ref in zip(k_refs, v_refs):
            for rows in _key_chunks(k_ref.shape[1]):
                s_parts.append(lax.dot_general(q, _pair_keys(k_ref, hw, j, rows), nt,
                                               preferred_element_type=F32))
                v_parts.append(v_ref[0, rows, :])
        outs.append(_softmax_pv(s_parts, v_parts))
    o_ref[0] = _pair_select(*outs).astype(o_ref.dtype)


def _pair_attention(q_src, kv_srcs, hw, nh):
    q, q_off = q_src
    bsz, lq = q.shape[0], q.shape[1]
    tq = _pick(lq, (512, 256))
    qw = 2 * hw
    in_specs = [pl.BlockSpec((1, tq, qw), lambda b, h, i: (b, i, q_off // qw + h))]
    args = [q]
    for k_arr, k_off, _, _ in kv_srcs:
        in_specs.append(pl.BlockSpec((1, k_arr.shape[1], qw), lambda b, h, i, o=k_off // qw: (b, 0, o + h)))
        args.append(k_arr)
    for _, _, v_arr, v_off in kv_srcs:
        in_specs.append(pl.BlockSpec((1, v_arr.shape[1], LANES), lambda b, h, i, o=v_off // LANES: (b, 0, o + h)))
        args.append(v_arr)
    return pl.pallas_call(
        functools.partial(_pair_attn_kernel, nparts=len(kv_srcs), hw=hw),
        grid=(bsz, nh // 2, lq // tq),
        in_specs=in_specs,
        out_specs=pl.BlockSpec((1, tq, LANES), lambda b, h, i: (b, i, h)),
        out_shape=jax.ShapeDtypeStruct((bsz, lq, nh * HEAD_V), BF16),
        compiler_params=_cparams("parallel", "parallel", "arbitrary"),
        name="attention",
    )(*args)


def _na_bias_table(rpb, rows):
    nh, n_dr, n_dc = rpb.shape
    kh = min(NA_WIN_H, rows)
    nblk = rows // NA_QROWS
    w = GRID_W
    qc = np.arange(w)[:, None]
    kc = np.arange(w)[None, :]
    cs = np.clip(qc - NA_WIN_W // 2, 0, w - NA_WIN_W)
    col_ok = (kc >= cs) & (kc < cs + NA_WIN_W)
    dc = np.clip(kc - qc, -(NA_WIN_W - 1), NA_WIN_W - 1) + NA_WIN_W - 1
    pick_dc = (dc.reshape(1, w * w) == np.arange(n_dc)[:, None]).astype(np.float32)
    tile = jnp.dot(rpb.reshape(nh * n_dr, n_dc).astype(F32), jnp.asarray(pick_dc), precision=HIGHEST)
    tile = jnp.where(col_ok, tile.reshape(nh, n_dr, w, w), -1e30)
    masked = jnp.full((nh, w, w), -1e30, F32)
    tables = []
    for blk in (0, 1, nblk - 1):
        start = int(np.clip(blk * NA_QROWS - NA_WIN_H // 2, 0, rows - NA_KROWS))
        q_rows = []
        for qr in range(NA_QROWS):
            r = blk * NA_QROWS + qr
            rs = int(np.clip(r - kh // 2, 0, rows - kh))
            tiles = [tile[:, kr - r + NA_WIN_H - 1] if rs <= kr < rs + kh else masked
                     for kr in range(start, start + NA_KROWS)]
            q_rows.append(jnp.concatenate(tiles, axis=-1))
        tables.append(jnp.concatenate(q_rows, axis=1))
    return jnp.stack(tables, axis=1)


def _na_kernel(q_ref, kl_ref, vl_ref, kc_ref, vc_ref, bias_ref, o_ref, *, nblk, rows):
    i = pl.program_id(2)
    start = jnp.clip(i * NA_QROWS - NA_WIN_H // 2, 0, rows - NA_KROWS) * GRID_W
    start = pl.multiple_of(start, GRID_W)
    pat = jnp.where(i == 0, 0, jnp.where(i == nblk - 1, 2, 1))
    local = pl.ds(start, NA_KROWS * GRID_W)
    nt = (((1,), (1,)), ((), ()))
    hw = NA_HEAD_DIM
    outs = []
    for j, q in enumerate(_pair_queries(q_ref[0], hw)):
        s_ctx = lax.dot_general(q, _pair_keys(kc_ref, hw, j), nt, preferred_element_type=F32)
        s_loc = (lax.dot_general(q, _pair_keys(kl_ref, hw, j, local), nt, preferred_element_type=F32)
                 + bias_ref[j, pat])
        outs.append(_softmax_pv([s_ctx, s_loc], [vc_ref[0], vl_ref[0, local, :]]))
    o_ref[0] = _pair_select(*outs).astype(o_ref.dtype)


def _na_mixer(x, ctx, g, sh, sc, csh, csc, w_qkv, rpb, w_o, gate, cgate):
    bsz, length, d = x.shape
    lc = ctx.shape[1]
    rows = length // GRID_W
    nblk = rows // NA_QROWS
    nh, hd = NA_HEADS, NA_HEAD_DIM
    w = jnp.concatenate([w_qkv[:, :d] * hd ** -0.5, w_qkv[:, d:]], axis=1).astype(BF16)
    qkv = _proj(x, w, g, sh, sc, out_dtype=BF16)
    qkv_c = _proj(ctx, w, g, csh, csc, out_dtype=BF16)
    bias = _na_bias_table(rpb, rows)
    tq = NA_QROWS * GRID_W
    npair = nh // 2
    pw = 2 * hd
    y = pl.pallas_call(
        functools.partial(_na_kernel, nblk=nblk, rows=rows),
        grid=(npair, bsz, nblk),
        in_specs=[
            pl.BlockSpec((1, tq, pw), lambda h, b, i: (b, i, h)),
            pl.BlockSpec((1, length, pw), lambda h, b, i: (b, 0, npair + h)),
            pl.BlockSpec((1, length, pw), lambda h, b, i: (b, 0, 2 * npair + h)),
            pl.BlockSpec((1, lc, pw), lambda h, b, i: (b, 0, npair + h)),
            pl.BlockSpec((1, lc, pw), lambda h, b, i: (b, 0, 2 * npair + h)),
            pl.BlockSpec((2, 3, tq, NA_KROWS * GRID_W), lambda h, b, i: (h, 0, 0, 0)),
        ],
        out_specs=pl.BlockSpec((1, tq, pw), lambda h, b, i: (b, i, h)),
        out_shape=jax.ShapeDtypeStruct((bsz, length, d), BF16),
        compiler_params=_cparams("parallel", "parallel", "arbitrary"),
        name="na_local",
    )(qkv, qkv, qkv, qkv_c, qkv_c, bias)
    if cgate is None:
        return _proj_res(y, w_o, x, gate), None
    yc = _pair_attention((qkv_c, 0), [(qkv_c, d, qkv_c, 2 * d)], hd, nh)
    return _proj_res(y, w_o, x, gate), _proj_res(yc, w_o, ctx, cgate)


def _axial_rope(length):
    t = jnp.arange(length)
    row = (t // GRID_W).astype(F32)
    col = (t % GRID_W).astype(F32)
    n_freq = MLA_ROPE_DIM // 4
    inv = ROPE_BASE ** (-jnp.arange(n_freq, dtype=F32) / n_freq)
    ang = jnp.concatenate([row[:, None] * inv, col[:, None] * inv], axis=-1)
    return jnp.cos(ang), jnp.sin(ang)


def _rot_half_cols(w):
    half = w.shape[-1] // 2
    return jnp.concatenate([-w[..., half:], w[..., :half]], axis=-1)


def _mla_weights(w_in, w_q_b, w_kv_b):
    nh, nope, rope = MLA_HEADS, MLA_NOPE_DIM, MLA_ROPE_DIM
    off = MLA_Q_RANK + MLA_KV_RANK
    w_in_ext = jnp.concatenate([w_in, _rot_half_cols(w_in[:, off:])], axis=1)
    rq, rkv = w_q_b.shape[0], w_kv_b.shape[0]
    wq = w_q_b.reshape(rq, nh, nope + rope)
    pad = jnp.zeros((rq, nh, LANES - nope - rope), F32)
    wq_pad = jnp.concatenate([wq, pad], axis=-1).reshape(rq, nh * LANES)
    wq_rot = jnp.concatenate([jnp.zeros((rq, nh, nope), F32), _rot_half_cols(wq[..., nope:]), pad],
                             axis=-1).reshape(rq, nh * LANES)
    wkv = w_kv_b.reshape(rkv, nh, nope + MLA_V_DIM)
    wk_pad = jnp.concatenate([wkv[..., :nope], jnp.zeros((rkv, nh, LANES - nope), F32)],
                             axis=-1).reshape(rkv, nh * LANES)
    wv = wkv[..., nope:].reshape(rkv, nh * MLA_V_DIM)
    place = np.zeros((rope, nh, LANES), np.float32)
    place[np.arange(rope), :, nope + np.arange(rope)] = 1.0
    place = jnp.asarray(place.reshape(rope, nh * LANES))
    return (w_in_ext,) + tuple(a.astype(BF16) for a in (wq_pad, wq_rot, wk_pad, wv, place))


def _mla_rope_tables(length, rope_on):
    if rope_on:
        cos, sin = _axial_rope(length)
    else:
        cos = jnp.ones((length, MLA_ROPE_DIM // 2), F32)
        sin = jnp.zeros((length, MLA_ROPE_DIM // 2), F32)
    ck = jnp.concatenate([cos, cos], axis=-1)
    sk = jnp.concatenate([sin, sin], axis=-1)
    one = jnp.ones((length, MLA_NOPE_DIM), F32)
    tail = LANES - MLA_NOPE_DIM - MLA_ROPE_DIM
    cq = jnp.concatenate([one, ck, one[:, :tail]], axis=-1)
    sq = jnp.concatenate([0.0 * one, sk, 0.0 * one[:, :tail]], axis=-1)
    return cq, sq, ck, sk


def _mla_qkv_kernel(z_ref, gq_ref, gkv_ref, wq_ref, wqr_ref, wk_ref, wv_ref, place_ref,
                    cq_ref, sq_ref, ck_ref, sk_ref, q_ref, k_ref, v_ref):
    z = z_ref[0]
    off = MLA_Q_RANK + MLA_KV_RANK
    rope = MLA_ROPE_DIM

    def norm(a, g_ref):
        ms = jnp.mean(a * a, axis=-1, keepdims=True)
        return (a * lax.rsqrt(ms + NORM_EPS) * g_ref[...]).astype(BF16)

    cq = norm(z[:, :MLA_Q_RANK], gq_ref)
    ckv = norm(z[:, MLA_Q_RANK:off], gkv_ref)
    k_rope = (z[:, off:off + rope] * ck_ref[...] + z[:, off + rope:off + 2 * rope] * sk_ref[...]).astype(BF16)
    qa = jnp.dot(cq, wq_ref[...], preferred_element_type=F32)
    qb = jnp.dot(cq, wqr_ref[...], preferred_element_type=F32)
    cos_q = cq_ref[...]
    sin_q = sq_ref[...]
    for h in range(MLA_HEADS):
        cols = slice(h * LANES, (h + 1) * LANES)
        q_ref[0, :, cols] = (qa[:, cols] * cos_q + qb[:, cols] * sin_q).astype(BF16)
    k_ref[0] = (jnp.dot(ckv, wk_ref[...], preferred_element_type=F32)
                + jnp.dot(k_rope, place_ref[...], preferred_element_type=F32)).astype(BF16)
    v_ref[0] = jnp.dot(ckv, wv_ref[...], preferred_element_type=F32).astype(BF16)


def _mla_qkv(z, q_norm_g, kv_norm_g, weights, tables):
    bsz, length, zw = z.shape
    wq, wqr, wk, wv, place = weights
    nh = MLA_HEADS
    tm = _pick(length, (512, 256))
    full = lambda a: pl.BlockSpec(a.shape, lambda b, i: (0, 0))
    rows = lambda a: pl.BlockSpec((tm, a.shape[1]), lambda b, i: (i, 0))
    gq = q_norm_g.reshape(1, -1)
    gkv = kv_norm_g.reshape(1, -1)
    out = lambda w: pl.BlockSpec((1, tm, w), lambda b, i: (b, i, 0))
    return pl.pallas_call(
        _mla_qkv_kernel,
        grid=(bsz, length // tm),
        in_specs=[pl.BlockSpec((1, tm, zw), lambda b, i: (b, i, 0)), full(gq), full(gkv),
                  full(wq), full(wqr), full(wk), full(wv), full(place)] + [rows(t) for t in tables],
        out_specs=[out(nh * LANES), out(nh * LANES), out(nh * MLA_V_DIM)],
        out_shape=[jax.ShapeDtypeStruct((bsz, length, nh * LANES), BF16),
                   jax.ShapeDtypeStruct((bsz, length, nh * LANES), BF16),
                   jax.ShapeDtypeStruct((bsz, length, nh * MLA_V_DIM), BF16)],
        compiler_params=_cparams("parallel", "parallel"),
        name="mla_qkv",
    )(z, gq, gkv, wq, wqr, wk, wv, place, *tables)


def _mla_mixer(x, ctx, g, sh, sc, csh, csc, w_in, q_norm_g, w_q_b, kv_norm_g, w_kv_b, w_o, gate, cgate):
    nh = MLA_HEADS
    scale = (MLA_NOPE_DIM + MLA_ROPE_DIM) ** -0.5
    w_in_ext, *weights = _mla_weights(w_in, w_q_b, w_kv_b)

    def project(u, shift, scl, rope_on):
        z = _proj(u, w_in_ext, g, shift, scl)
        cos_q, sin_q, cos_k, sin_k = _mla_rope_tables(u.shape[1], rope_on)
        return _mla_qkv(z, q_norm_g, kv_norm_g, weights, (cos_q * scale, sin_q * scale, cos_k, sin_k))

    ql, kl, vl = project(x, sh, sc, True)
    qc, kc, vc = project(ctx, csh, csc, False)
    y = _pair_attention((ql, 0), [(kc, 0, vc, 0), (kl, 0, vl, 0)], LANES, nh)
    if cgate is None:
        return _proj_res(y, w_o, x, gate), None
    yc = _pair_attention((qc, 0), [(kc, 0, vc, 0)], LANES, nh)
    return _proj_res(y, w_o, x, gate), _proj_res(yc, w_o, ctx, cgate)


def _conv3(z, w_ref, b_ref):
    length = z.shape[0]
    t = lax.broadcasted_iota(jnp.int32, z.shape, 0)
    prev = jnp.where(t == 0, 0.0, pltpu.roll(z, 1, axis=0))
    nxt = jnp.where(t == length - 1, 0.0, pltpu.roll(z, length - 1, axis=0))
    return b_ref[...] + prev * w_ref[0:1, :] + z * w_ref[1:2, :] + nxt * w_ref[2:3, :]


def _hy_conv_kernel(z0_ref, z1_ref, z2_ref, w0_ref, w1_ref, w2_ref, b0_ref, b1_ref, b2_ref,
                    x0_ref, gg_ref):
    x0_ref[0] = _conv3(z0_ref[0], w0_ref, b0_ref)
    x1 = _conv3(z1_ref[0], w1_ref, b1_ref)
    v = _conv3(z2_ref[0], w2_ref, b2_ref)
    gg_ref[0] = v * x1


def _hy_conv(z, conv_w, conv_b, d):
    bsz, length, _ = z.shape
    tc = 256
    nb = d // tc
    zspec = lambda k: pl.BlockSpec((1, length, tc), lambda b, j: (b, 0, k * nb + j))
    wspec = lambda k: pl.BlockSpec((3, tc), lambda b, j: (0, k * nb + j))
    bspec = lambda k: pl.BlockSpec((1, tc), lambda b, j: (0, k * nb + j))
    ospec = pl.BlockSpec((1, length, tc), lambda b, j: (b, 0, j))
    cb = conv_b.reshape(1, 3 * d)
    return pl.pallas_call(
        _hy_conv_kernel,
        grid=(bsz, nb),
        in_specs=[zspec(0), zspec(1), zspec(2), wspec(0), wspec(1), wspec(2), bspec(0), bspec(1), bspec(2)],
        out_specs=[ospec, ospec],
        out_shape=[jax.ShapeDtypeStruct((bsz, length, d), F32)] * 2,
        compiler_params=_cparams("parallel", "parallel"),
        name="hy_conv",
    )(z, z, z, conv_w, conv_w, conv_w, cb, cb, cb)


def _dft_tables(length):
    n2 = 2 * length
    lo = GRID_W
    w = 2.0 * math.pi / n2
    f = jnp.arange(length, dtype=jnp.int32)[:, None]
    n_hi = jnp.arange(length // lo, dtype=jnp.int32)[None, :] * lo
    n_lo = jnp.arange(lo, dtype=jnp.int32)[None, :]
    ang_hi = ((f * n_hi) % n2).astype(F32) * w
    ang_lo = ((f * n_lo) % n2).astype(F32) * w
    c_hi, s_hi = jnp.cos(ang_hi)[:, :, None], jnp.sin(ang_hi)[:, :, None]
    c_lo, s_lo = jnp.cos(ang_lo)[:, None, :], jnp.sin(ang_lo)[:, None, :]
    cf = (c_hi * c_lo - s_hi * s_lo).reshape(length, length)
    sn = (s_hi * c_lo + c_hi * s_lo).reshape(length, length)
    n = jnp.arange(length, dtype=jnp.int32)[None, :]
    sgn_n = jnp.where(n % 2 == 0, 1.0, -1.0).astype(F32)
    sf = jnp.where(f == 0, sgn_n, -sn)
    return cf.astype(BF16), sf.astype(BF16), sf.T.astype(BF16)


def _hy_fwd_kernel(cf_ref, sf_ref, gg_ref, kre_ref, kim_ref, y_ref, gb_ref):
    j = pl.program_id(1)

    @pl.when(j == 0)
    def _():
        gb_ref[...] = gg_ref[0].astype(BF16)

    ure = jnp.dot(cf_ref[...], gb_ref[...], preferred_element_type=F32)
    uim = jnp.dot(sf_ref[...], gb_ref[...], preferred_element_type=F32)
    kre = kre_ref[...]
    kim = kim_ref[...]
    tf = ure.shape[0]
    row = lax.broadcasted_iota(jnp.int32, ure.shape, 0) + j * tf
    packed = row == 0
    yre = ure * kre - jnp.where(packed, 0.0, uim * kim)
    yim = uim * jnp.where(packed, kim, kre) + jnp.where(packed, 0.0, ure * kim)
    y_ref[0, 0] = yre.astype(BF16)
    y_ref[0, 1] = yim.astype(BF16)


def _hy_inv_kernel(ci_ref, si_ref, y_ref, gg_ref, x0_ref, skip_ref, o_ref):
    y = (jnp.dot(ci_ref[...], y_ref[0, 0], preferred_element_type=F32)
         + jnp.dot(si_ref[...], y_ref[0, 1], preferred_element_type=F32))
    o_ref[0] = ((y + gg_ref[0] * skip_ref[...]) * x0_ref[0]).astype(o_ref.dtype)


def _hy_long_conv_gate(gg, x0, kre, kim, skip, tables):
    bsz, length, d = gg.shape
    cf, sf, sft = tables
    tf = 256
    y = pl.pallas_call(
        _hy_fwd_kernel,
        grid=(bsz, length // tf),
        in_specs=[
            pl.BlockSpec((tf, length), lambda b, j: (j, 0)),
            pl.BlockSpec((tf, length), lambda b, j: (j, 0)),
            pl.BlockSpec((1, length, d), lambda b, j: (b, 0, 0)),
            pl.BlockSpec((tf, d), lambda b, j: (j, 0)),
            pl.BlockSpec((tf, d), lambda b, j: (j, 0)),
        ],
        out_specs=pl.BlockSpec((1, 2, tf, d), lambda b, j: (b, 0, j, 0)),
        out_shape=jax.ShapeDtypeStruct((bsz, 2, length, d), BF16),
        scratch_shapes=[pltpu.VMEM((length, d), BF16)],
        compiler_params=_cparams("parallel", "arbitrary"),
        name="hy_dft_fwd",
    )(cf, sf, gg, kre, kim)
    tt = 256
    return pl.pallas_call(
        _hy_inv_kernel,
        grid=(bsz, length // tt),
        in_specs=[
            pl.BlockSpec((tt, length), lambda b, i: (i, 0)),
            pl.BlockSpec((tt, length), lambda b, i: (i, 0)),
            pl.BlockSpec((1, 2, length, d), lambda b, i: (b, 0, 0, 0)),
            pl.BlockSpec((1, tt, d), lambda b, i: (b, i, 0)),
            pl.BlockSpec((1, tt, d), lambda b, i: (b, i, 0)),
            pl.BlockSpec((1, d), lambda b, i: (0, 0)),
        ],
        out_specs=pl.BlockSpec((1, tt, d), lambda b, i: (b, i, 0)),
        out_shape=jax.ShapeDtypeStruct((bsz, length, d), BF16),
        compiler_params=_cparams("parallel", "arbitrary"),
        name="hy_dft_inv",
    )(cf, sft, y, gg, x0, skip.reshape(1, d))


def _hyena_filter(length, w1, b1, w2, b2, w3, b3, sin_freq):
    t = jnp.linspace(0.0, 1.0, length, dtype=F32)[:, None]
    bands = (HY_EMB_DIM - 1) // 2
    w = (2.0 * math.pi / length) * jnp.arange(length, dtype=F32)[:, None]
    f = jnp.linspace(1e-4, bands - 1, bands, dtype=F32)[None, :]
    z = jnp.concatenate([t, jnp.cos(f * w), -jnp.sin(f * w)], axis=-1)
    hp = lax.Precision.HIGHEST
    h = jnp.sin(sin_freq[0] * (jnp.dot(z, w1, precision=hp) + b1))
    h = jnp.sin(sin_freq[1] * (jnp.dot(h, w2, precision=hp) + b2))
    h = jnp.dot(h, w3, precision=hp) + b3
    max_decay = math.log(HY_DECAY_TARGET) / HY_FAST_DECAY
    min_decay = math.log(HY_DECAY_TARGET) / HY_SLOW_DECAY
    deltas = jnp.abs(jnp.linspace(min_decay, max_decay, h.shape[-1] // 2, dtype=F32))
    deltas = jnp.tile(deltas, 2)
    return h * (jnp.exp(-t * deltas) + HY_MOD_SHIFT)


def _hy_filter_spectrum(filt, d, tables):
    length = filt.shape[0]
    cf, sf, _ = tables
    row0 = (jnp.arange(length) == 0)[:, None]
    hf = filt[:, :d]
    hb0 = jnp.where(row0, 0.0, filt[:, d:])
    tab = jnp.concatenate([cf, sf], axis=0)[None]
    spec = _proj(tab, jnp.concatenate([hf, hb0], axis=1))[0]
    wgt = jnp.where(row0, 1.0, 2.0).astype(F32) / (2 * length)
    kre = (spec[:length, :d] + spec[:length, d:]) * wgt
    kim = (spec[length:, :d] + jnp.where(row0, 1.0, -1.0) * spec[length:, d:]) * wgt
    return kre, kim


def _hyena_mixer(x, ctx, g, sh, sc, csh, csc, w_in, conv_w, conv_b, f_w1, f_b1, f_w2, f_b2, f_w3, f_b3,
                 sin_freq, skip, w_o, gate, cgate):
    d = x.shape[-1]

    def one(u, shift, scale, res, gt):
        length = u.shape[1]
        z = _proj(u, w_in, g, shift, scale)
        x0, gg = _hy_conv(z, conv_w, conv_b, d)
        tables = _dft_tables(length)
        filt = _hyena_filter(length, f_w1, f_b1, f_w2, f_b2, f_w3, f_b3, sin_freq)
        kre, kim = _hy_filter_spectrum(filt, d, tables)
        y = _hy_long_conv_gate(gg, x0, kre, kim, skip, tables)
        return _proj_res(y, w_o, res, gt)

    return one(x, sh, sc, x, gate), (one(ctx, csh, csc, ctx, cgate) if cgate is not None else None)


def _ml_conv_kernel(z_ref, w_ref, b_ref, o_ref, *t_refs, out_scale):
    y = _conv3(z_ref[0], w_ref, b_ref)
    y = y * jax.nn.sigmoid(y) * out_scale
    o_ref[0] = y
    for t_ref in t_refs:
        t_ref[0] = y.T


def _ml_conv(z, conv_w, conv_b, col0, width, out_scale, transposed):
    bsz, length, _ = z.shape
    tc = 256
    j0 = col0 // tc
    out_specs = [pl.BlockSpec((1, length, tc), lambda b, j: (b, 0, j))]
    out_shape = [jax.ShapeDtypeStruct((bsz, length, width), F32)]
    if transposed:
        out_specs.append(pl.BlockSpec((1, tc, length), lambda b, j: (b, j, 0)))
        out_shape.append(jax.ShapeDtypeStruct((bsz, width, length), F32))
    return pl.pallas_call(
        functools.partial(_ml_conv_kernel, out_scale=out_scale),
        grid=(bsz, width // tc),
        in_specs=[pl.BlockSpec((1, length, tc), lambda b, j: (b, 0, j0 + j)),
                  pl.BlockSpec((3, tc), lambda b, j: (0, j0 + j)),
                  pl.BlockSpec((1, tc), lambda b, j: (0, j0 + j))],
        out_specs=out_specs,
        out_shape=out_shape,
        compiler_params=_cparams("parallel", "parallel"),
        name="ml_conv",
    )(z, conv_w, conv_b.reshape(1, -1))


def _ml_gates_kernel(x_ref, g_ref, sh_ref, sc_ref, w_ref, b_ref, o_ref, t_ref):
    x = x_ref[0]
    ms = jnp.mean(x * x, axis=-1, keepdims=True)
    h = x * lax.rsqrt(ms + NORM_EPS) * g_ref[...]
    h = h * (1.0 + sc_ref[0]) + sh_ref[0]
    gates = jnp.dot(h.astype(BF16), w_ref[...], preferred_element_type=F32) + b_ref[...]
    o_ref[0] = gates
    t_ref[0] = gates.T


def _ml_gates(x, g, shift, scale, w_gate, gate_b):
    bsz, length, d = x.shape
    pad = LANES - w_gate.shape[1]
    w = jnp.pad(w_gate, ((0, 0), (0, pad))).astype(BF16)
    b = jnp.pad(gate_b, (0, pad)).reshape(1, LANES)
    tm = _pick(length, (1024, 256))
    return pl.pallas_call(
        _ml_gates_kernel,
        grid=(bsz, length // tm),
        in_specs=[
            pl.BlockSpec((1, tm, d), lambda b_, i: (b_, i, 0)),
            pl.BlockSpec((1, d), lambda b_, i: (0, 0)),
            pl.BlockSpec((1, 1, d), lambda b_, i: (b_, 0, 0)),
            pl.BlockSpec((1, 1, d), lambda b_, i: (b_, 0, 0)),
            pl.BlockSpec((d, LANES), lambda b_, i: (0, 0)),
            pl.BlockSpec((1, LANES), lambda b_, i: (0, 0)),
        ],
        out_specs=[pl.BlockSpec((1, tm, LANES), lambda b_, i: (b_, i, 0)),
                   pl.BlockSpec((1, LANES, tm), lambda b_, i: (b_, 0, i))],
        out_shape=[jax.ShapeDtypeStruct((bsz, length, LANES), F32),
                   jax.ShapeDtypeStruct((bsz, LANES, length), F32)],
        compiler_params=_cparams("parallel", "parallel"),
        name="ml_gates",
    )(x, g.reshape(1, d), shift.reshape(bsz, 1, d), scale.reshape(bsz, 1, d), w, b)


def _log_sigmoid(x):
    return jnp.minimum(x, 0.0) - jnp.log1p(jnp.exp(-jnp.abs(x)))


def _ml_chunk_pair(q2, kt2, k2, v2, gates, i_rows, f_rows, f_lanes, state, reverse):
    ct2, n2, ms = state
    tlen = q2.shape[0]
    dk, dv = ML_QK_DIM, ML_V_DIM
    ti = lax.broadcasted_iota(jnp.int32, (tlen, tlen), 0)
    si = lax.broadcasted_iota(jnp.int32, (tlen, tlen), 1)
    causal = (si >= ti) if reverse else (si <= ti)
    causal_t = (ti >= si) if reverse else (ti <= si)
    lane = lax.broadcasted_iota(jnp.int32, (1, LANES), 1)
    sub = lax.broadcasted_iota(jnp.int32, (LANES, 1), 0)
    kt_b = kt2.astype(BF16)
    k_b = k2.astype(BF16)
    v_b = v2.astype(BF16)
    ct_b = ct2.astype(BF16)
    hs, ws, decays, m_news = [], [], [], []
    for j in range(2):
        own = (lane >= dk) if j else (lane < dk)
        lf_row = _log_sigmoid(f_rows[j])
        f_col = jnp.where(lane == f_lanes[j], gates, 0.0).sum(axis=-1, keepdims=True)
        lf_col = _log_sigmoid(f_col)
        b_col = jnp.where(causal, lf_row, 0.0).sum(axis=-1, keepdims=True)
        b_row = jnp.where(causal_t, lf_col, 0.0).sum(axis=0, keepdims=True)
        log_d = jnp.where(causal, b_col - b_row + i_rows[j], -jnp.inf)
        m_inter = b_col + ms[j]
        m_t = jnp.maximum(log_d.max(axis=-1, keepdims=True), m_inter)
        qj = jnp.where(own, q2, 0.0)
        qj_b = qj.astype(BF16)
        s = jnp.dot(qj_b, kt_b, preferred_element_type=F32) * jnp.exp(log_d - m_t)
        inter = jnp.exp(m_inter - m_t)
        sv = jnp.dot(s.astype(BF16), v_b, preferred_element_type=F32)[:, j * dv:(j + 1) * dv]
        num = sv + inter * jnp.dot(qj_b, ct_b, preferred_element_type=F32)
        qn = s.sum(axis=-1, keepdims=True) + inter * (qj * n2).sum(axis=-1, keepdims=True)
        hs.append(num / jnp.maximum(jnp.abs(qn), jnp.exp(-m_t)))
        b_end = b_row[:, 0:1] if reverse else b_row[:, tlen - 1:tlen]
        w_log = b_end - b_row + i_rows[j]
        m_new = jnp.maximum(b_end + ms[j], w_log.max(axis=-1, keepdims=True))
        ws.append(jnp.exp(w_log - m_new))
        decays.append(jnp.exp(b_end + ms[j] - m_new))
        m_news.append(m_new)
    head0_rows = sub < dk
    head0_lanes = lane < dk
    kw = (kt2 * jnp.where(head0_rows, ws[0], ws[1])).astype(BF16)
    upd = jnp.dot(kw, v_b, preferred_element_type=F32)
    ct_new = (jnp.where(head0_rows, decays[0], decays[1]) * ct2
              + jnp.where(head0_rows, upd[:, :dv], upd[:, dv:]))
    wk = [jnp.dot(w.astype(BF16), k_b, preferred_element_type=F32) for w in ws]
    n_new = jnp.where(head0_lanes, decays[0], decays[1]) * n2 + jnp.where(head0_lanes, wk[0], wk[1])
    return jnp.concatenate(hs, axis=-1), (ct_new, n_new, tuple(m_news))


def _mlstm_kernel(ql_ref, kl_ref, ktl_ref, vl_ref, ol_ref, gl_ref, gtl_ref,
                  qc_ref, kc_ref, ktc_ref, vc_ref, oc_ref, gc_ref, gtc_ref, ng_ref,
                  yl_ref, yc_ref, hfl_ref, hbl_ref, hfc_ref, hbc_ref):
    tlen = ML_CHUNK
    dv, nh = ML_V_DIM, ML_HEADS
    n_lat = ql_ref.shape[1] // tlen
    n_ctx = qc_ref.shape[1] // tlen
    head0 = 2 * pl.program_id(1)

    def run(refs, c, state, direction):
        q_ref, k_ref, kt_ref, v_ref, g_ref, gt_ref, hf_ref, hb_ref = refs
        rows = pl.ds(pl.multiple_of(c * tlen, tlen), tlen)
        gate_i, gate_f = 2 * direction * nh, (2 * direction + 1) * nh
        i_rows = [gt_ref[0, pl.ds(gate_i + head0 + j, 1), rows] for j in range(2)]
        f_rows = [gt_ref[0, pl.ds(gate_f + head0 + j, 1), rows] for j in range(2)]
        f_lanes = [gate_f + head0 + j for j in range(2)]
        h, state = _ml_chunk_pair(q_ref[0, rows, :], kt_ref[0, :, rows], k_ref[0, rows, :], v_ref[0, rows, :],
                                  g_ref[0, rows, :], i_rows, f_rows, f_lanes, state, direction == 1)
        (hb_ref if direction else hf_ref)[rows, :] = h
        return state

    def sweep(refs, n_chunks, states):
        def body(c, sts):
            return run(refs, c, sts[0], 0), run(refs, n_chunks - 1 - c, sts[1], 1)
        return lax.fori_loop(0, n_chunks, body, states)

    zero = (jnp.zeros((LANES, dv), F32), jnp.zeros((1, LANES), F32),
            (jnp.zeros((1, 1), F32), jnp.zeros((1, 1), F32)))
    lat = (ql_ref, kl_ref, ktl_ref, vl_ref, gl_ref, gtl_ref, hfl_ref, hbl_ref)
    ctx = (qc_ref, kc_ref, ktc_ref, vc_ref, gc_ref, gtc_ref, hfc_ref, hbc_ref)
    sweep(lat, n_lat, sweep(ctx, n_ctx, (zero, zero)))

    def finish(hf_ref, hb_ref, o_ref, y_ref):
        for j in range(2):
            vs = slice(j * dv, (j + 1) * dv)
            h = hf_ref[:, vs] + hb_ref[:, vs]
            h = h * lax.rsqrt(jnp.mean(h * h, axis=-1, keepdims=True) + NORM_EPS) * ng_ref[:, vs]
            y_ref[0, :, vs] = (h * jax.nn.sigmoid(o_ref[0, :, vs])).astype(y_ref.dtype)

    finish(hfl_ref, hbl_ref, ol_ref, yl_ref)
    finish(hfc_ref, hbc_ref, oc_ref, yc_ref)


def _mlstm_mixer(x, ctx, g, sh, sc, csh, csc, w_in, conv_w, conv_b, gate_b, out_norm_g, w_o, gate, cgate):
    nh, dk, dv = ML_HEADS, ML_QK_DIM, ML_V_DIM
    nqk = 2 * nh * dk
    wide = nqk + 2 * nh * dv
    w_main = w_in[:, :wide].astype(BF16)
    w_gate = w_in[:, wide:]

    def project(u, shift, scale):
        z = _proj(u, w_main, g, shift, scale)
        gates, gates_t = _ml_gates(u, g, shift, scale, w_gate, gate_b)
        (q,) = _ml_conv(z, conv_w, conv_b, 0, nh * dk, 1.0, False)
        k, kt = _ml_conv(z, conv_w, conv_b, nh * dk, nh * dk, dk ** -0.5, True)
        return z, q, k, kt, gates, gates_t

    zl, ql, kl, ktl, gl, gtl = project(x, sh, sc)
    zc, qc, kc, ktc, gc, gtc = project(ctx, csh, csc)
    bsz, length, d = x.shape
    lc = ctx.shape[1]
    npair = nh // 2
    qw, vw = 2 * dk, 2 * dv
    v_off, o_off = nqk // vw, (nqk + nh * dv) // vw

    def specs(n):
        return [
            pl.BlockSpec((1, n, qw), lambda b, h: (b, 0, h)),
            pl.BlockSpec((1, n, qw), lambda b, h: (b, 0, h)),
            pl.BlockSpec((1, qw, n), lambda b, h: (b, h, 0)),
            pl.BlockSpec((1, n, vw), lambda b, h: (b, 0, v_off + h)),
            pl.BlockSpec((1, n, vw), lambda b, h: (b, 0, o_off + h)),
            pl.BlockSpec((1, n, LANES), lambda b, h: (b, 0, 0)),
            pl.BlockSpec((1, LANES, n), lambda b, h: (b, 0, 0)),
        ]

    yl, yc = pl.pallas_call(
        _mlstm_kernel,
        grid=(bsz, npair),
        in_specs=specs(length) + specs(lc) + [pl.BlockSpec((1, vw), lambda b, h: (0, h))],
        out_specs=[pl.BlockSpec((1, length, vw), lambda b, h: (b, 0, h)),
                   pl.BlockSpec((1, lc, vw), lambda b, h: (b, 0, h))],
        out_shape=[jax.ShapeDtypeStruct((bsz, length, nh * dv), BF16),
                   jax.ShapeDtypeStruct((bsz, lc, nh * dv), BF16)],
        scratch_shapes=[pltpu.VMEM((length, vw), F32), pltpu.VMEM((length, vw), F32),
                        pltpu.VMEM((lc, vw), F32), pltpu.VMEM((lc, vw), F32)],
        compiler_params=_cparams("parallel", "parallel"),
        name="mlstm",
    )(ql, kl, ktl, zl, zl, gl, gtl, qc, kc, ktc, zc, zc, gc, gtc, out_norm_g.reshape(1, nh * dv))
    return _proj_res(yl, w_o, x, gate), (_proj_res(yc, w_o, ctx, cgate) if cgate is not None else None)


def _router_kernel(x_ref, g_ref, sh_ref, sc_ref, rw_ref, h_ref, aff_ref):
    x = x_ref[0]
    ms = jnp.mean(x * x, axis=-1, keepdims=True)
    h = x * lax.rsqrt(ms + NORM_EPS) * g_ref[...]
    h = h * (1.0 + sc_ref[0]) + sh_ref[0]
    h_ref[0] = h.astype(BF16)
    logits = lax.dot_general(rw_ref[...], h, (((1,), (1,)), ((), ())), precision=HIGHEST,
                             preferred_element_type=F32)
    mx = logits.max(axis=0, keepdims=True)
    p = jnp.exp(logits - mx)
    aff_ref[0] = p / p.sum(axis=0, keepdims=True)


def _router(x, g, shift, scale, router_w):
    bsz, length, d = x.shape
    ne = router_w.shape[1]
    tm = _pick(length, (1024, 256))
    return pl.pallas_call(
        _router_kernel,
        grid=(bsz, length // tm),
        in_specs=[
            pl.BlockSpec((1, tm, d), lambda b, i: (b, i, 0)),
            pl.BlockSpec((1, d), lambda b, i: (0, 0)),
            pl.BlockSpec((1, 1, d), lambda b, i: (b, 0, 0)),
            pl.BlockSpec((1, 1, d), lambda b, i: (b, 0, 0)),
            pl.BlockSpec((ne, d), lambda b, i: (0, 0)),
        ],
        out_specs=[pl.BlockSpec((1, tm, d), lambda b, i: (b, i, 0)),
                   pl.BlockSpec((1, ne, tm), lambda b, i: (b, 0, i))],
        out_shape=[jax.ShapeDtypeStruct((bsz, length, d), BF16),
                   jax.ShapeDtypeStruct((bsz, ne, length), F32)],
        compiler_params=_cparams("parallel", "parallel"),
        name="moe_router",
    )(x, g.reshape(1, d), shift.reshape(bsz, 1, d), scale.reshape(bsz, 1, d), router_w.T)


def _excl_cumsum_lanes(flags):
    rows, length = flags.shape
    cw = min(length, 256)
    si = lax.broadcasted_iota(jnp.int32, (cw, cw), 0)
    ti = lax.broadcasted_iota(jnp.int32, (cw, cw), 1)
    upper = jnp.where(si < ti, 1.0, 0.0).astype(BF16)
    carry = jnp.zeros((rows, 1), F32)
    parts = []
    for c in range(length // cw):
        blk = flags[:, c * cw:(c + 1) * cw]
        parts.append(jnp.dot(blk.astype(BF16), upper, preferred_element_type=F32) + carry)
        carry = carry + blk.sum(axis=-1, keepdims=True)
    return parts[0] if len(parts) == 1 else jnp.concatenate(parts, axis=-1)


def _select_kernel(aff_ref, pos_ref, *, cap):
    a = aff_ref[...]
    rows = a.shape[0]
    capf = float(cap)

    def bisect(_, lohi):
        lo, hi = lohi
        mid = 0.5 * (lo + hi)
        cnt = jnp.where(a >= mid, 1.0, 0.0).sum(axis=-1, keepdims=True)
        ge = cnt >= capf
        return jnp.where(ge, mid, lo), jnp.where(ge, hi, mid)

    lo, hi = lax.fori_loop(0, SELECT_BISECTIONS, bisect,
                           (jnp.zeros((rows, 1), F32), jnp.full((rows, 1), 2.0, F32)))
    above = a >= hi
    tie = jnp.logical_and(a >= lo, jnp.logical_not(above))
    n_above = jnp.where(above, 1.0, 0.0).sum(axis=-1, keepdims=True)
    tie_rank = _excl_cumsum_lanes(jnp.where(tie, 1.0, 0.0))
    sel = jnp.logical_or(above, jnp.logical_and(tie, tie_rank < capf - n_above))
    slot = _excl_cumsum_lanes(jnp.where(sel, 1.0, 0.0))
    pos_ref[...] = jnp.where(sel, slot, -1.0).astype(jnp.int32)


def _select(aff2d, cap):
    return pl.pallas_call(
        functools.partial(_select_kernel, cap=cap),
        out_shape=jax.ShapeDtypeStruct(aff2d.shape, jnp.int32),
        compiler_params=pltpu.CompilerParams(vmem_limit_bytes=V7X_VMEM_LIMIT_BYTES),
        name="moe_select",
    )(aff2d)


def _gather_kernel(h_ref, pos_ref, aff_ref, xg_ref, gs_ref, *, cap):
    pos = pos_ref[0, 0]
    length = pos.shape[1]
    slot = lax.broadcasted_iota(jnp.int32, (cap, length), 0)
    hit = slot == pos
    onehot = jnp.where(hit, 1.0, 0.0).astype(BF16)
    xg_ref[0] = jnp.dot(onehot, h_ref[0], preferred_element_type=F32).astype(BF16)
    gs_ref[0] = jnp.where(hit, aff_ref[0, 0], 0.0).sum(axis=-1, keepdims=True)


def _gather(h, pos, aff, cap):
    bsz, length, d = h.shape
    ne = pos.shape[1]
    return pl.pallas_call(
        functools.partial(_gather_kernel, cap=cap),
        grid=(bsz, ne),
        in_specs=[
            pl.BlockSpec((1, length, d), lambda b, e: (b, 0, 0)),
            pl.BlockSpec((1, 1, 1, length), lambda b, e: (b, e, 0, 0)),
            pl.BlockSpec((1, 1, 1, length), lambda b, e: (b, e, 0, 0)),
        ],
        out_specs=[pl.BlockSpec((1, cap, d), lambda b, e: (e, b, 0)),
                   pl.BlockSpec((1, cap, 1), lambda b, e: (e, b, 0))],
        out_shape=[jax.ShapeDtypeStruct((ne, bsz * cap, d), BF16),
                   jax.ShapeDtypeStruct((ne, bsz * cap, 1), F32)],
        compiler_params=_cparams("parallel", "arbitrary"),
        name="moe_gather",
    )(h, pos.reshape(bsz, ne, 1, length), aff.reshape(bsz, ne, 1, length))


FFN_TILES_PER_STEP = 2
FFN_ROW_CHUNK = 1024


def _ffn_kernel(*refs, n_groups, n_up, n_tiles, tf):
    per = FFN_TILES_PER_STEP
    xg_refs = refs[:n_groups]
    gs_refs = refs[n_groups:2 * n_groups]
    w_refs = refs[2 * n_groups:2 * n_groups + 2 * per + 1]
    wg_refs, wu_refs, wd_ref = w_refs[:per], w_refs[per:2 * per], w_refs[2 * per]
    y_refs = refs[2 * n_groups + 2 * per + 1:3 * n_groups + 2 * per + 1]
    z_refs = refs[3 * n_groups + 2 * per + 1:]
    step = pl.program_id(1)

    def up_tile(k):
        wg = wg_refs[k][0, 0].astype(BF16)
        wu = wu_refs[k][0, 0].astype(BF16)
        for xg_ref, z_ref in zip(xg_refs, z_refs):
            m = xg_ref.shape[1]
            rc = min(m, FFN_ROW_CHUNK)
            for r in range(m // rc):
                rows = slice(r * rc, (r + 1) * rc)
                xg = xg_ref[0, rows, :]
                a = jnp.dot(xg, wg, preferred_element_type=F32)
                u = jnp.dot(xg, wu, preferred_element_type=F32)
                z = (a * jax.nn.sigmoid(a) * u).astype(BF16)
                for t in range(k, n_tiles, per):
                    @pl.when(step == t // per)
                    def _():
                        z_ref[rows, t * tf:(t + 1) * tf] = z

    for k in range(per):
        pl.when(per * step + k < n_tiles)(functools.partial(up_tile, k))

    @pl.when(step >= n_up)
    def _():
        wd = wd_ref[0, 0].astype(BF16)
        for gs_ref, y_ref, z_ref in zip(gs_refs, y_refs, z_refs):
            y = jnp.dot(z_ref[...], wd, preferred_element_type=F32)
            y_ref[0] = (y * gs_ref[0]).astype(BF16)


def _expert_ffn(xgs, gss, w_gate, w_up, w_down, layer):
    ne, _, d = xgs[0].shape
    ff = w_gate.shape[3]
    tf = 256
    tn = 256
    per = FFN_TILES_PER_STEP
    n_tiles, n_down = ff // tf, d // tn
    n_up = -(-n_tiles // per)
    n = len(xgs)
    tok = lambda a: pl.BlockSpec((1,) + a.shape[1:], lambda e, s: (e, 0, 0))
    up_spec = lambda k: pl.BlockSpec(
        (1, 1, d, tf), lambda e, s: (layer, e, 0, jnp.minimum(per * s + k, n_tiles - 1)))
    down_tile = lambda e, s: (layer, e, 0, jnp.maximum(s - n_up, 0))
    up_specs = [up_spec(k) for k in range(per)]
    return pl.pallas_call(
        functools.partial(_ffn_kernel, n_groups=n, n_up=n_up, n_tiles=n_tiles, tf=tf),
        grid=(ne, n_up + n_down),
        in_specs=[tok(a) for a in xgs] + [tok(a) for a in gss] + up_specs + up_specs + [
            pl.BlockSpec((1, 1, ff, tn), down_tile),
        ],
        out_specs=[pl.BlockSpec((1, a.shape[1], tn), lambda e, s: (e, 0, jnp.maximum(s - n_up, 0)))
                   for a in xgs],
        out_shape=[jax.ShapeDtypeStruct(a.shape, BF16) for a in xgs],
        scratch_shapes=[pltpu.VMEM((a.shape[1], ff), BF16) for a in xgs],
        compiler_params=_cparams("parallel", "arbitrary"),
        name="moe_ffn",
    )(*xgs, *gss, *([w_gate] * per), *([w_up] * per), w_down)


def _combine_kernel(post_ref, y_ref, x_ref, gt_ref, fg_ref, o_ref, *, cap, final_norm):
    post = post_ref[0]
    tl, ne = post.shape
    d = y_ref.shape[2]
    slot = lax.broadcasted_iota(jnp.int32, (tl, cap), 1)
    hits = [jnp.where(post[:, e:e + 1] == slot, 1.0, 0.0).astype(BF16) for e in range(ne)]
    if cap % LANES == 0:
        acc = jnp.dot(jnp.concatenate(hits, axis=1), y_ref[...].reshape(ne * cap, d),
                      preferred_element_type=F32)
    else:
        acc = jnp.zeros((tl, d), F32)
        for e in range(ne):
            acc = acc + jnp.dot(hits[e], y_ref[e], preferred_element_type=F32)
    out = x_ref[0] + gt_ref[0] * acc
    if final_norm:
        ms = jnp.mean(out * out, axis=-1, keepdims=True)
        out = out * lax.rsqrt(ms + NORM_EPS) * fg_ref[...]
    o_ref[0] = out


def _combine(pos_t, y, x, gate, cap, final_g=None):
    bsz, length, d = x.shape
    ne = y.shape[0]
    tl = _pick(length, (512, 256))
    final_norm = final_g is not None
    fg = (final_g if final_norm else jnp.ones((d,), F32)).reshape(1, d)
    return pl.pallas_call(
        functools.partial(_combine_kernel, cap=cap, final_norm=final_norm),
        grid=(bsz, length // tl),
        in_specs=[
            pl.BlockSpec((1, tl, ne), lambda b, i: (b, i, 0)),
            pl.BlockSpec((ne, cap, d), lambda b, i: (0, b, 0)),
            pl.BlockSpec((1, tl, d), lambda b, i: (b, i, 0)),
            pl.BlockSpec((1, 1, d), lambda b, i: (b, 0, 0)),
            pl.BlockSpec((1, d), lambda b, i: (0, 0)),
        ],
        out_specs=pl.BlockSpec((1, tl, d), lambda b, i: (b, i, 0)),
        out_shape=jax.ShapeDtypeStruct(x.shape, F32),
        compiler_params=_cparams("parallel", "arbitrary"),
        name="moe_combine",
    )(pos_t, y, x, gate.reshape(bsz, 1, d), fg)


def _route(x, g, shift, scale, router_w):
    bsz, length, _ = x.shape
    ne = router_w.shape[1]
    cap = max(1, EC_CAPACITY_FACTOR * length // ne)
    h, aff = _router(x, g, shift, scale, router_w)
    pos = _select(aff.reshape(bsz * ne, length), cap).reshape(bsz, ne, length)
    xg, gs = _gather(h, pos, aff, cap)
    return xg, gs, pos.transpose(0, 2, 1), cap


def _moe(x, ctx, g, sh, sc, gate, csh, csc, cgate, router_w, w_gate, w_up, w_down, layer, final_g=None):
    xg, gs, pos_t, cap = _route(x, g, sh, sc, router_w)
    if ctx is None:
        (y,) = _expert_ffn([xg], [gs], w_gate, w_up, w_down, layer)
        return _combine(pos_t, y, x, gate, cap, final_g), None
    xg_c, gs_c, pos_tc, cap_c = _route(ctx, g, csh, csc, router_w)
    y, y_c = _expert_ffn([xg, xg_c], [gs, gs_c], w_gate, w_up, w_down, layer)
    return _combine(pos_t, y, x, gate, cap), _combine(pos_tc, y_c, ctx, cgate, cap_c)


def kernel(x, c, ctx, c_ctx, mod_w, mod_b, norm_mix_g, norm_ffn_g, router_w, moe_w_gate, moe_w_up, moe_w_down, na_w_qkv, na_rpb, na_w_o, mla_w_in, mla_q_norm_g, mla_w_q_b, mla_kv_norm_g, mla_w_kv_b, mla_w_o, hy_w_in, hy_conv_w, hy_conv_b, hy_f_w1, hy_f_b1, hy_f_w2, hy_f_b2, hy_f_w3, hy_f_b3, hy_sin_freq, hy_skip, hy_w_o, ml_w_in, ml_conv_w, ml_conv_b, ml_gate_b, ml_out_norm_g, ml_w_o, final_norm_g):
    bsz, _, d = x.shape
    depth = mod_w.shape[0]
    n_mixers = 4
    cond = jnp.concatenate([c, c_ctx[None, :]], axis=0)
    cond = jnp.pad(cond, ((0, (-cond.shape[0]) % 8), (0, 0)))
    mods = _modulation(cond, mod_w, mod_b)
    for i in range(depth):
        last = i == depth - 1
        mod = mods[i]
        sh1, sc1, g1, sh2, sc2, g2 = [mod[:bsz, k * d:(k + 1) * d] for k in range(6)]
        bc = lambda v: jnp.broadcast_to(v[None, :], (bsz, d))
        csh1, csc1, cg1, csh2, csc2, cg2 = [bc(mod[bsz, k * d:(k + 1) * d]) for k in range(6)]
        if last:
            cg1 = None
        kind, j = i % n_mixers, i // n_mixers
        gm = norm_mix_g[i]
        if kind == 0:
            x, ctx_new = _na_mixer(x, ctx, gm, sh1, sc1, csh1, csc1, na_w_qkv[j], na_rpb[j], na_w_o[j], g1, cg1)
        elif kind == 1:
            x, ctx_new = _mla_mixer(x, ctx, gm, sh1, sc1, csh1, csc1, mla_w_in[j], mla_q_norm_g[j],
                                    mla_w_q_b[j], mla_kv_norm_g[j], mla_w_kv_b[j], mla_w_o[j], g1, cg1)
        elif kind == 2:
            x, ctx_new = _hyena_mixer(x, ctx, gm, sh1, sc1, csh1, csc1, hy_w_in[j], hy_conv_w[j], hy_conv_b[j],
                                      hy_f_w1[j], hy_f_b1[j], hy_f_w2[j], hy_f_b2[j], hy_f_w3[j], hy_f_b3[j],
                                      hy_sin_freq[j], hy_skip[j], hy_w_o[j], g1, cg1)
        else:
            x, ctx_new = _mlstm_mixer(x, ctx, gm, sh1, sc1, csh1, csc1, ml_w_in[j], ml_conv_w[j], ml_conv_b[j],
                                      ml_gate_b[j], ml_out_norm_g[j], ml_w_o[j], g1, cg1)
        if last:
            x, _ = _moe(x, None, norm_ffn_g[i], sh2, sc2, g2, None, None, None,
                        router_w[i], moe_w_gate, moe_w_up, moe_w_down, i, final_norm_g)
        else:
            x, ctx = _moe(x, ctx_new, norm_ffn_g[i], sh2, sc2, g2, csh2, csc2, cg2,
                          router_w[i], moe_w_gate, moe_w_up, moe_w_down, i)
    return x
```

```python
import functools
import math

import numpy as np
import jax
import jax.numpy as jnp
from jax import lax
from jax.experimental import pallas as pl
from jax.experimental.pallas import tpu as pltpu

F32 = jnp.float32
BF16 = jnp.bfloat16
HIGHEST = lax.Precision.HIGHEST

V7X_VMEM_LIMIT_BYTES = 56 * 1024 * 1024
LANES = 128

GRID_W = 64
NORM_EPS = 1e-6
ROPE_BASE = 10000.0

NA_HEADS = 16
NA_HEAD_DIM = 64
NA_WIN_H = 8
NA_WIN_W = 16
NA_QROWS = 4
NA_KROWS = NA_QROWS - 1 + NA_WIN_H

MLA_HEADS = 16
MLA_Q_RANK = 384
MLA_KV_RANK = 256
MLA_NOPE_DIM = 64
MLA_ROPE_DIM = 32
MLA_V_DIM = 64

HY_EMB_DIM = 33
HY_DECAY_TARGET = 1e-2
HY_FAST_DECAY = 0.3
HY_SLOW_DECAY = 1.5
HY_MOD_SHIFT = 0.05

ML_HEADS = 8
ML_V_DIM = 128
ML_QK_DIM = 64
ML_CHUNK = 256

N_EXPERTS = 16
EC_CAPACITY_FACTOR = 2
SELECT_BISECTIONS = 64


def _cparams(*sem):
    return pltpu.CompilerParams(dimension_semantics=sem, vmem_limit_bytes=V7X_VMEM_LIMIT_BYTES)


def _pick(n, prefs):
    for p in prefs:
        if n % p == 0:
            return p
    return n


def _proj_kernel(x_ref, g_ref, sh_ref, sc_ref, w_ref, o_ref, h_ref, *, norm):
    @pl.when(pl.program_id(2) == 0)
    def _():
        x = x_ref[0].astype(F32)
        if norm:
            ms = jnp.mean(x * x, axis=-1, keepdims=True)
            x = x * lax.rsqrt(ms + NORM_EPS) * g_ref[...]
            x = x * (1.0 + sc_ref[0]) + sh_ref[0]
        h_ref[...] = x.astype(BF16)

    o_ref[0] = jnp.dot(h_ref[...], w_ref[...].astype(BF16),
                       preferred_element_type=F32).astype(o_ref.dtype)


def _proj(x, w, g=None, shift=None, scale=None, out_dtype=F32):
    bsz, length, kdim = x.shape
    n = w.shape[1]
    norm = g is not None
    if not norm:
        g = jnp.ones((kdim,), F32)
    if shift is None:
        shift = jnp.zeros((bsz, kdim), F32)
        scale = jnp.zeros((bsz, kdim), F32)
    tm = _pick(length, (1024, 512, 256))
    tn = _pick(n, (1024, 768, 512, 384, 256, 128))
    return pl.pallas_call(
        functools.partial(_proj_kernel, norm=norm),
        grid=(bsz, length // tm, n // tn),
        in_specs=[
            pl.BlockSpec((1, tm, kdim), lambda b, i, j: (b, i, 0)),
            pl.BlockSpec((1, kdim), lambda b, i, j: (0, 0)),
            pl.BlockSpec((1, 1, kdim), lambda b, i, j: (b, 0, 0)),
            pl.BlockSpec((1, 1, kdim), lambda b, i, j: (b, 0, 0)),
            pl.BlockSpec((kdim, tn), lambda b, i, j: (0, j)),
        ],
        out_specs=pl.BlockSpec((1, tm, tn), lambda b, i, j: (b, i, j)),
        out_shape=jax.ShapeDtypeStruct((bsz, length, n), out_dtype),
        scratch_shapes=[pltpu.VMEM((tm, kdim), BF16)],
        compiler_params=_cparams("parallel", "parallel", "arbitrary"),
        name="proj",
    )(x, g.reshape(1, kdim).astype(F32), shift.reshape(bsz, 1, kdim), scale.reshape(bsz, 1, kdim),
      w.astype(BF16))


def _proj_res_kernel(y_ref, w_ref, r_ref, gt_ref, o_ref):
    acc = jnp.dot(y_ref[0].astype(BF16), w_ref[...].astype(BF16), preferred_element_type=F32)
    o_ref[0] = r_ref[0] + gt_ref[0] * acc


def _proj_res(y, w, res, gate):
    bsz, length, kdim = y.shape
    n = w.shape[1]
    tm = _pick(length, (1024, 512, 256))
    tn = _pick(n, (1024, 512, 256, 128))
    return pl.pallas_call(
        _proj_res_kernel,
        grid=(bsz, length // tm, n // tn),
        in_specs=[
            pl.BlockSpec((1, tm, kdim), lambda b, i, j: (b, i, 0)),
            pl.BlockSpec((kdim, tn), lambda b, i, j: (0, j)),
            pl.BlockSpec((1, tm, tn), lambda b, i, j: (b, i, j)),
            pl.BlockSpec((1, 1, tn), lambda b, i, j: (b, 0, j)),
        ],
        out_specs=pl.BlockSpec((1, tm, tn), lambda b, i, j: (b, i, j)),
        out_shape=jax.ShapeDtypeStruct((bsz, length, n), F32),
        compiler_params=_cparams("parallel", "parallel", "arbitrary"),
        name="proj_res",
    )(y, w.astype(BF16), res, gate.reshape(bsz, 1, n))


def _mod_kernel(c_ref, w_ref, b_ref, o_ref):
    c = c_ref[...]
    s = c * jax.nn.sigmoid(c)
    o_ref[0] = jnp.dot(s.astype(BF16), w_ref[0].astype(BF16), preferred_element_type=F32) + b_ref[0]


def _modulation(cc, w, b):
    rows, d = cc.shape
    depth, _, n = w.shape
    tn = 512
    return pl.pallas_call(
        _mod_kernel,
        grid=(depth, n // tn),
        in_specs=[
            pl.BlockSpec((rows, d), lambda l, j: (0, 0)),
            pl.BlockSpec((1, d, tn), lambda l, j: (l, 0, j)),
            pl.BlockSpec((1, 1, tn), lambda l, j: (l, 0, j)),
        ],
        out_specs=pl.BlockSpec((1, rows, tn), lambda l, j: (l, 0, j)),
        out_shape=jax.ShapeDtypeStruct((depth, rows, n), F32),
        compiler_params=_cparams("parallel", "parallel"),
        name="modulation",
    )(cc, w, b.reshape(depth, 1, n))


ATTN_KEY_CHUNK = 512
LOG2_E = math.log2(math.e)


def _softmax_pv(s_parts, v_parts):
    m = den = acc = None
    for s, v in zip(s_parts, v_parts):
        m_part = s.max(axis=-1, keepdims=True)
        m_new = m_part if m is None else jnp.maximum(m, m_part)
        p = jnp.exp2(s - m_new)
        l = p.sum(axis=-1, keepdims=True)
        o = jnp.dot(p.astype(BF16), v, preferred_element_type=F32)
        if m is None:
            den, acc = l, o
        else:
            alpha = jnp.exp2(m - m_new)
            den, acc = alpha * den + l, alpha * acc + o
        m = m_new
    return acc / den


def _key_chunks(n):
    step = ATTN_KEY_CHUNK if n % ATTN_KEY_CHUNK == 0 else n
    return [slice(c, c + step) for c in range(0, n, step)]


HEAD_V = 64


def _pair_queries(q2, hw):
    if hw == LANES:
        return [q2[:, :LANES], q2[:, LANES:]]
    lane = lax.broadcasted_iota(jnp.int32, (1, LANES), 1)
    return [jnp.where(lane < hw, q2, jnp.zeros_like(q2)), jnp.where(lane >= hw, q2, jnp.zeros_like(q2))]


def _pair_keys(k_ref, hw, j, rows=None):
    rows = slice(None) if rows is None else rows
    return k_ref[0, rows, j * LANES:(j + 1) * LANES] if hw == LANES else k_ref[0, rows, :]


def _pair_select(o0, o1):
    lane = lax.broadcasted_iota(jnp.int32, (1, LANES), 1)
    return jnp.where(lane < HEAD_V, o0, o1)


def _pair_attn_kernel(*refs, nparts, hw):
    q_ref, o_ref = refs[0], refs[-1]
    k_refs = refs[1:1 + nparts]
    v_refs = refs[1 + nparts:1 + 2 * nparts]
    nt = (((1,), (1,)), ((), ()))
    outs = []
    for j, q in enumerate(_pair_queries(q_ref[0], hw)):
        s_parts, v_parts = [], []
        for k_ref, v_ref in zip(k_refs, v_refs):
            for rows in _key_chunks(k_ref.shape[1]):
                s_parts.append(lax.dot_general(q, _pair_keys(k_ref, hw, j, rows), nt,
                                               preferred_element_type=F32))
                v_parts.append(v_ref[0, rows, :])
        outs.append(_softmax_pv(s_parts, v_parts))
    o_ref[0] = _pair_select(*outs).astype(o_ref.dtype)


def _pair_attention(q_src, kv_srcs, hw, nh):
    q, q_off = q_src
    bsz, lq = q.shape[0], q.shape[1]
    tq = _pick(lq, (1024, 512, 256))
    qw = 2 * hw
    in_specs = [pl.BlockSpec((1, tq, qw), lambda b, h, i: (b, i, q_off // qw + h))]
    args = [q]
    for k_arr, k_off, _, _ in kv_srcs:
        in_specs.append(pl.BlockSpec((1, k_arr.shape[1], qw), lambda b, h, i, o=k_off // qw: (b, 0, o + h)))
        args.append(k_arr)
    for _, _, v_arr, v_off in kv_srcs:
        in_specs.append(pl.BlockSpec((1, v_arr.shape[1], LANES), lambda b, h, i, o=v_off // LANES: (b, 0, o + h)))
        args.append(v_arr)
    return pl.pallas_call(
        functools.partial(_pair_attn_kernel, nparts=len(kv_srcs), hw=hw),
        grid=(bsz, nh // 2, lq // tq),
        in_specs=in_specs,
        out_specs=pl.BlockSpec((1, tq, LANES), lambda b, h, i: (b, i, h)),
        out_shape=jax.ShapeDtypeStruct((bsz, lq, nh * HEAD_V), BF16),
        compiler_params=_cparams("parallel", "parallel", "arbitrary"),
        name="attention",
    )(*args)


def _na_bias_table(rpb, rows):
    nh, n_dr, n_dc = rpb.shape
    kh = min(NA_WIN_H, rows)
    nblk = rows // NA_QROWS
    w = GRID_W
    qc = np.arange(w)[:, None]
    kc = np.arange(w)[None, :]
    cs = np.clip(qc - NA_WIN_W // 2, 0, w - NA_WIN_W)
    col_ok = (kc >= cs) & (kc < cs + NA_WIN_W)
    dc = np.clip(kc - qc, -(NA_WIN_W - 1), NA_WIN_W - 1) + NA_WIN_W - 1
    pick_dc = (dc.reshape(1, w * w) == np.arange(n_dc)[:, None]).astype(np.float32)
    tile = jnp.dot(rpb.reshape(nh * n_dr, n_dc).astype(F32), jnp.asarray(pick_dc), precision=HIGHEST)
    tile = jnp.where(col_ok, tile.reshape(nh, n_dr, w, w), -1e30)
    masked = jnp.full((nh, w, w), -1e30, F32)
    tables = []
    for blk in (0, 1, nblk - 1):
        start = int(np.clip(blk * NA_QROWS - NA_WIN_H // 2, 0, rows - NA_KROWS))
        q_rows = []
        for qr in range(NA_QROWS):
            r = blk * NA_QROWS + qr
            rs = int(np.clip(r - kh // 2, 0, rows - kh))
            tiles = [tile[:, kr - r + NA_WIN_H - 1] if rs <= kr < rs + kh else masked
                     for kr in range(start, start + NA_KROWS)]
            q_rows.append(jnp.concatenate(tiles, axis=-1))
        tables.append(jnp.concatenate(q_rows, axis=1))
    return jnp.stack(tables, axis=1)


NA_BLOCKS_PER_STEP = 2


def _na_kernel(q_ref, kl_ref, vl_ref, kc_ref, vc_ref, bias_ref, o_ref, *, nblk, rows):
    nt = (((1,), (1,)), ((), ()))
    hw = NA_HEAD_DIM
    tq = NA_QROWS * GRID_W
    for sub in range(NA_BLOCKS_PER_STEP):
        i = pl.program_id(2) * NA_BLOCKS_PER_STEP + sub
        start = jnp.clip(i * NA_QROWS - NA_WIN_H // 2, 0, rows - NA_KROWS) * GRID_W
        start = pl.multiple_of(start, GRID_W)
        pat = jnp.where(i == 0, 0, jnp.where(i == nblk - 1, 2, 1))
        local = pl.ds(start, NA_KROWS * GRID_W)
        qrows = slice(sub * tq, (sub + 1) * tq)
        outs = []
        for j, q in enumerate(_pair_queries(q_ref[0, qrows, :], hw)):
            s_ctx = lax.dot_general(q, _pair_keys(kc_ref, hw, j), nt, preferred_element_type=F32)
            s_loc = (lax.dot_general(q, _pair_keys(kl_ref, hw, j, local), nt, preferred_element_type=F32)
                     + bias_ref[j, pat])
            outs.append(_softmax_pv([s_ctx, s_loc], [vc_ref[0], vl_ref[0, local, :]]))
        o_ref[0, qrows, :] = _pair_select(*outs).astype(o_ref.dtype)


def _na_mixer(x, ctx, g, sh, sc, csh, csc, w_qkv, rpb, w_o, gate, cgate):
    bsz, length, d = x.shape
    lc = ctx.shape[1]
    rows = length // GRID_W
    nblk = rows // NA_QROWS
    nh, hd = NA_HEADS, NA_HEAD_DIM
    w = jnp.concatenate([w_qkv[:, :d] * (hd ** -0.5 * LOG2_E), w_qkv[:, d:]], axis=1).astype(BF16)
    qkv = _proj(x, w, g, sh, sc, out_dtype=BF16)
    qkv_c = _proj(ctx, w, g, csh, csc, out_dtype=BF16)
    bias = _na_bias_table(rpb * LOG2_E, rows)
    tq = NA_QROWS * GRID_W
    tstep = NA_BLOCKS_PER_STEP * tq
    npair = nh // 2
    pw = 2 * hd
    y = pl.pallas_call(
        functools.partial(_na_kernel, nblk=nblk, rows=rows),
        grid=(npair, bsz, nblk // NA_BLOCKS_PER_STEP),
        in_specs=[
            pl.BlockSpec((1, tstep, pw), lambda h, b, i: (b, i, h)),
            pl.BlockSpec((1, length, pw), lambda h, b, i: (b, 0, npair + h)),
            pl.BlockSpec((1, length, pw), lambda h, b, i: (b, 0, 2 * npair + h)),
            pl.BlockSpec((1, lc, pw), lambda h, b, i: (b, 0, npair + h)),
            pl.BlockSpec((1, lc, pw), lambda h, b, i: (b, 0, 2 * npair + h)),
            pl.BlockSpec((2, 3, tq, NA_KROWS * GRID_W), lambda h, b, i: (h, 0, 0, 0)),
        ],
        out_specs=pl.BlockSpec((1, tstep, pw), lambda h, b, i: (b, i, h)),
        out_shape=jax.ShapeDtypeStruct((bsz, length, d), BF16),
        compiler_params=_cparams("parallel", "parallel", "arbitrary"),
        name="na_local",
    )(qkv, qkv, qkv, qkv_c, qkv_c, bias)
    if cgate is None:
        return _proj_res(y, w_o, x, gate), None
    yc = _pair_attention((qkv_c, 0), [(qkv_c, d, qkv_c, 2 * d)], hd, nh)
    return _proj_res(y, w_o, x, gate), _proj_res(yc, w_o, ctx, cgate)


def _axial_rope(length):
    t = jnp.arange(length)
    row = (t // GRID_W).astype(F32)
    col = (t % GRID_W).astype(F32)
    n_freq = MLA_ROPE_DIM // 4
    inv = ROPE_BASE ** (-jnp.arange(n_freq, dtype=F32) / n_freq)
    ang = jnp.concatenate([row[:, None] * inv, col[:, None] * inv], axis=-1)
    return jnp.cos(ang), jnp.sin(ang)


def _rot_half_cols(w):
    half = w.shape[-1] // 2
    return jnp.concatenate([-w[..., half:], w[..., :half]], axis=-1)


def _mla_weights(w_in, w_q_b, w_kv_b):
    nh, nope, rope = MLA_HEADS, MLA_NOPE_DIM, MLA_ROPE_DIM
    off = MLA_Q_RANK + MLA_KV_RANK
    w_in_ext = jnp.concatenate([w_in, _rot_half_cols(w_in[:, off:])], axis=1)
    rq, rkv = w_q_b.shape[0], w_kv_b.shape[0]
    wq = w_q_b.reshape(rq, nh, nope + rope)
    pad = jnp.zeros((rq, nh, LANES - nope - rope), F32)
    wq_pad = jnp.concatenate([wq, pad], axis=-1).reshape(rq, nh * LANES)
    wq_rot = jnp.concatenate([jnp.zeros((rq, nh, nope), F32), _rot_half_cols(wq[..., nope:]), pad],
                             axis=-1).reshape(rq, nh * LANES)
    wkv = w_kv_b.reshape(rkv, nh, nope + MLA_V_DIM)
    wk_pad = jnp.concatenate([wkv[..., :nope], jnp.zeros((rkv, nh, LANES - nope), F32)],
                             axis=-1).reshape(rkv, nh * LANES)
    wv = wkv[..., nope:].reshape(rkv, nh * MLA_V_DIM)
    place = np.zeros((rope, nh, LANES), np.float32)
    place[np.arange(rope), :, nope + np.arange(rope)] = 1.0
    place = jnp.asarray(place.reshape(rope, nh * LANES))
    return (w_in_ext,) + tuple(a.astype(BF16) for a in (wq_pad, wq_rot, wk_pad, wv, place))


def _mla_rope_tables(length, rope_on):
    if rope_on:
        cos, sin = _axial_rope(length)
    else:
        cos = jnp.ones((length, MLA_ROPE_DIM // 2), F32)
        sin = jnp.zeros((length, MLA_ROPE_DIM // 2), F32)
    ck = jnp.concatenate([cos, cos], axis=-1)
    sk = jnp.concatenate([sin, sin], axis=-1)
    one = jnp.ones((length, MLA_NOPE_DIM), F32)
    tail = LANES - MLA_NOPE_DIM - MLA_ROPE_DIM
    cq = jnp.concatenate([one, ck, one[:, :tail]], axis=-1)
    sq = jnp.concatenate([0.0 * one, sk, 0.0 * one[:, :tail]], axis=-1)
    return cq, sq, ck, sk


def _mla_qkv_kernel(z_ref, gq_ref, gkv_ref, wq_ref, wqr_ref, wk_ref, wv_ref, place_ref,
                    cq_ref, sq_ref, ck_ref, sk_ref, q_ref, k_ref, v_ref):
    z = z_ref[0]
    off = MLA_Q_RANK + MLA_KV_RANK
    rope = MLA_ROPE_DIM

    def norm(a, g_ref):
        ms = jnp.mean(a * a, axis=-1, keepdims=True)
        return (a * lax.rsqrt(ms + NORM_EPS) * g_ref[...]).astype(BF16)

    cq = norm(z[:, :MLA_Q_RANK], gq_ref)
    ckv = norm(z[:, MLA_Q_RANK:off], gkv_ref)
    k_rope = (z[:, off:off + rope] * ck_ref[...] + z[:, off + rope:off + 2 * rope] * sk_ref[...]).astype(BF16)
    qa = jnp.dot(cq, wq_ref[...], preferred_element_type=F32)
    qb = jnp.dot(cq, wqr_ref[...], preferred_element_type=F32)
    cos_q = cq_ref[...]
    sin_q = sq_ref[...]
    for h in range(MLA_HEADS):
        cols = slice(h * LANES, (h + 1) * LANES)
        q_ref[0, :, cols] = (qa[:, cols] * cos_q + qb[:, cols] * sin_q).astype(BF16)
    k_ref[0] = (jnp.dot(ckv, wk_ref[...], preferred_element_type=F32)
                + jnp.dot(k_rope, place_ref[...], preferred_element_type=F32)).astype(BF16)
    v_ref[0] = jnp.dot(ckv, wv_ref[...], preferred_element_type=F32).astype(BF16)


def _mla_qkv(z, q_norm_g, kv_norm_g, weights, tables):
    bsz, length, zw = z.shape
    wq, wqr, wk, wv, place = weights
    nh = MLA_HEADS
    tm = _pick(length, (512, 256))
    full = lambda a: pl.BlockSpec(a.shape, lambda b, i: (0, 0))
    rows = lambda a: pl.BlockSpec((tm, a.shape[1]), lambda b, i: (i, 0))
    gq = q_norm_g.reshape(1, -1)
    gkv = kv_norm_g.reshape(1, -1)
    out = lambda w: pl.BlockSpec((1, tm, w), lambda b, i: (b, i, 0))
    return pl.pallas_call(
        _mla_qkv_kernel,
        grid=(bsz, length // tm),
        in_specs=[pl.BlockSpec((1, tm, zw), lambda b, i: (b, i, 0)), full(gq), full(gkv),
                  full(wq), full(wqr), full(wk), full(wv), full(place)] + [rows(t) for t in tables],
        out_specs=[out(nh * LANES), out(nh * LANES), out(nh * MLA_V_DIM)],
        out_shape=[jax.ShapeDtypeStruct((bsz, length, nh * LANES), BF16),
                   jax.ShapeDtypeStruct((bsz, length, nh * LANES), BF16),
                   jax.ShapeDtypeStruct((bsz, length, nh * MLA_V_DIM), BF16)],
        compiler_params=_cparams("parallel", "parallel"),
        name="mla_qkv",
    )(z, gq, gkv, wq, wqr, wk, wv, place, *tables)


def _mla_mixer(x, ctx, g, sh, sc, csh, csc, w_in, q_norm_g, w_q_b, kv_norm_g, w_kv_b, w_o, gate, cgate):
    nh = MLA_HEADS
    scale = (MLA_NOPE_DIM + MLA_ROPE_DIM) ** -0.5
    w_in_ext, *weights = _mla_weights(w_in, w_q_b, w_kv_b)

    def project(u, shift, scl, rope_on):
        z = _proj(u, w_in_ext, g, shift, scl)
        cos_q, sin_q, cos_k, sin_k = _mla_rope_tables(u.shape[1], rope_on)
        qs = scale * LOG2_E
        return _mla_qkv(z, q_norm_g, kv_norm_g, weights, (cos_q * qs, sin_q * qs, cos_k, sin_k))

    ql, kl, vl = project(x, sh, sc, True)
    qc, kc, vc = project(ctx, csh, csc, False)
    y = _pair_attention((ql, 0), [(kc, 0, vc, 0), (kl, 0, vl, 0)], LANES, nh)
    if cgate is None:
        return _proj_res(y, w_o, x, gate), None
    yc = _pair_attention((qc, 0), [(kc, 0, vc, 0)], LANES, nh)
    return _proj_res(y, w_o, x, gate), _proj_res(yc, w_o, ctx, cgate)


def _conv3(z, w_ref, b_ref):
    length = z.shape[0]
    t = lax.broadcasted_iota(jnp.int32, z.shape, 0)
    prev = jnp.where(t == 0, 0.0, pltpu.roll(z, 1, axis=0))
    nxt = jnp.where(t == length - 1, 0.0, pltpu.roll(z, length - 1, axis=0))
    return b_ref[...] + prev * w_ref[0:1, :] + z * w_ref[1:2, :] + nxt * w_ref[2:3, :]


def _hy_conv_kernel(z0_ref, z1_ref, z2_ref, w0_ref, w1_ref, w2_ref, b0_ref, b1_ref, b2_ref,
                    x0_ref, gg_ref):
    x0_ref[0] = _conv3(z0_ref[0], w0_ref, b0_ref)
    x1 = _conv3(z1_ref[0], w1_ref, b1_ref)
    v = _conv3(z2_ref[0], w2_ref, b2_ref)
    gg_ref[0] = v * x1


def _hy_conv(z, conv_w, conv_b, d):
    bsz, length, _ = z.shape
    tc = 256
    nb = d // tc
    zspec = lambda k: pl.BlockSpec((1, length, tc), lambda b, j: (b, 0, k * nb + j))
    wspec = lambda k: pl.BlockSpec((3, tc), lambda b, j: (0, k * nb + j))
    bspec = lambda k: pl.BlockSpec((1, tc), lambda b, j: (0, k * nb + j))
    ospec = pl.BlockSpec((1, length, tc), lambda b, j: (b, 0, j))
    cb = conv_b.reshape(1, 3 * d)
    return pl.pallas_call(
        _hy_conv_kernel,
        grid=(bsz, nb),
        in_specs=[zspec(0), zspec(1), zspec(2), wspec(0), wspec(1), wspec(2), bspec(0), bspec(1), bspec(2)],
        out_specs=[ospec, ospec],
        out_shape=[jax.ShapeDtypeStruct((bsz, length, d), F32)] * 2,
        compiler_params=_cparams("parallel", "parallel"),
        name="hy_conv",
    )(z, z, z, conv_w, conv_w, conv_w, cb, cb, cb)


def _dft_tables(length):
    n2 = 2 * length
    lo = GRID_W
    w = 2.0 * math.pi / n2
    f = jnp.arange(length, dtype=jnp.int32)[:, None]
    n_hi = jnp.arange(length // lo, dtype=jnp.int32)[None, :] * lo
    n_lo = jnp.arange(lo, dtype=jnp.int32)[None, :]
    ang_hi = ((f * n_hi) % n2).astype(F32) * w
    ang_lo = ((f * n_lo) % n2).astype(F32) * w
    c_hi, s_hi = jnp.cos(ang_hi)[:, :, None], jnp.sin(ang_hi)[:, :, None]
    c_lo, s_lo = jnp.cos(ang_lo)[:, None, :], jnp.sin(ang_lo)[:, None, :]
    cf = (c_hi * c_lo - s_hi * s_lo).reshape(length, length)
    sn = (s_hi * c_lo + c_hi * s_lo).reshape(length, length)
    n = jnp.arange(length, dtype=jnp.int32)[None, :]
    sgn_n = jnp.where(n % 2 == 0, 1.0, -1.0).astype(F32)
    sf = jnp.where(f == 0, sgn_n, -sn)
    return cf.astype(BF16), sf.astype(BF16), sf.T.astype(BF16)


def _hy_fwd_kernel(cf_ref, sf_ref, gg_ref, kre_ref, kim_ref, y_ref, gb_ref):
    j = pl.program_id(1)

    @pl.when(j == 0)
    def _():
        gb_ref[...] = gg_ref[0].astype(BF16)

    ure = jnp.dot(cf_ref[...], gb_ref[...], preferred_element_type=F32)
    uim = jnp.dot(sf_ref[...], gb_ref[...], preferred_element_type=F32)
    kre = kre_ref[...]
    kim = kim_ref[...]
    tf = ure.shape[0]
    row = lax.broadcasted_iota(jnp.int32, ure.shape, 0) + j * tf
    packed = row == 0
    yre = ure * kre - jnp.where(packed, 0.0, uim * kim)
    yim = uim * jnp.where(packed, kim, kre) + jnp.where(packed, 0.0, ure * kim)
    y_ref[0, 0] = yre.astype(BF16)
    y_ref[0, 1] = yim.astype(BF16)


def _hy_inv_kernel(ci_ref, si_ref, y_ref, gg_ref, x0_ref, skip_ref, o_ref):
    y = (jnp.dot(ci_ref[...], y_ref[0, 0], preferred_element_type=F32)
         + jnp.dot(si_ref[...], y_ref[0, 1], preferred_element_type=F32))
    o_ref[0] = ((y + gg_ref[0] * skip_ref[...]) * x0_ref[0]).astype(o_ref.dtype)


def _hy_long_conv_gate(gg, x0, kre, kim, skip, tables):
    bsz, length, d = gg.shape
    cf, sf, sft = tables
    tf = 256
    y = pl.pallas_call(
        _hy_fwd_kernel,
        grid=(bsz, length // tf),
        in_specs=[
            pl.BlockSpec((tf, length), lambda b, j: (j, 0)),
            pl.BlockSpec((tf, length), lambda b, j: (j, 0)),
            pl.BlockSpec((1, length, d), lambda b, j: (b, 0, 0)),
            pl.BlockSpec((tf, d), lambda b, j: (j, 0)),
            pl.BlockSpec((tf, d), lambda b, j: (j, 0)),
        ],
        out_specs=pl.BlockSpec((1, 2, tf, d), lambda b, j: (b, 0, j, 0)),
        out_shape=jax.ShapeDtypeStruct((bsz, 2, length, d), BF16),
        scratch_shapes=[pltpu.VMEM((length, d), BF16)],
        compiler_params=_cparams("parallel", "arbitrary"),
        name="hy_dft_fwd",
    )(cf, sf, gg, kre, kim)
    tt = 256
    return pl.pallas_call(
        _hy_inv_kernel,
        grid=(bsz, length // tt),
        in_specs=[
            pl.BlockSpec((tt, length), lambda b, i: (i, 0)),
            pl.BlockSpec((tt, length), lambda b, i: (i, 0)),
            pl.BlockSpec((1, 2, length, d), lambda b, i: (b, 0, 0, 0)),
            pl.BlockSpec((1, tt, d), lambda b, i: (b, i, 0)),
            pl.BlockSpec((1, tt, d), lambda b, i: (b, i, 0)),
            pl.BlockSpec((1, d), lambda b, i: (0, 0)),
        ],
        out_specs=pl.BlockSpec((1, tt, d), lambda b, i: (b, i, 0)),
        out_shape=jax.ShapeDtypeStruct((bsz, length, d), BF16),
        compiler_params=_cparams("parallel", "arbitrary"),
        name="hy_dft_inv",
    )(cf, sft, y, gg, x0, skip.reshape(1, d))


def _hyena_filter(length, w1, b1, w2, b2, w3, b3, sin_freq):
    t = jnp.linspace(0.0, 1.0, length, dtype=F32)[:, None]
    bands = (HY_EMB_DIM - 1) // 2
    w = (2.0 * math.pi / length) * jnp.arange(length, dtype=F32)[:, None]
    f = jnp.linspace(1e-4, bands - 1, bands, dtype=F32)[None, :]
    z = jnp.concatenate([t, jnp.cos(f * w), -jnp.sin(f * w)], axis=-1)
    hp = lax.Precision.HIGHEST
    h = jnp.sin(sin_freq[0] * (jnp.dot(z, w1, precision=hp) + b1))
    h = jnp.sin(sin_freq[1] * (jnp.dot(h, w2, precision=hp) + b2))
    h = jnp.dot(h, w3, precision=hp) + b3
    max_decay = math.log(HY_DECAY_TARGET) / HY_FAST_DECAY
    min_decay = math.log(HY_DECAY_TARGET) / HY_SLOW_DECAY
    deltas = jnp.abs(jnp.linspace(min_decay, max_decay, h.shape[-1] // 2, dtype=F32))
    deltas = jnp.tile(deltas, 2)
    return h * (jnp.exp(-t * deltas) + HY_MOD_SHIFT)


def _hy_filter_spectrum(filt, d, tables):
    length = filt.shape[0]
    cf, sf, _ = tables
    row0 = (jnp.arange(length) == 0)[:, None]
    hf = filt[:, :d]
    hb0 = jnp.where(row0, 0.0, filt[:, d:])
    tab = jnp.concatenate([cf, sf], axis=0)[None]
    spec = _proj(tab, jnp.concatenate([hf, hb0], axis=1))[0]
    wgt = jnp.where(row0, 1.0, 2.0).astype(F32) / (2 * length)
    kre = (spec[:length, :d] + spec[:length, d:]) * wgt
    kim = (spec[length:, :d] + jnp.where(row0, 1.0, -1.0) * spec[length:, d:]) * wgt
    return kre, kim


def _hyena_mixer(x, ctx, g, sh, sc, csh, csc, w_in, conv_w, conv_b, f_w1, f_b1, f_w2, f_b2, f_w3, f_b3,
                 sin_freq, skip, w_o, gate, cgate):
    d = x.shape[-1]

    def one(u, shift, scale, res, gt):
        length = u.shape[1]
        z = _proj(u, w_in, g, shift, scale)
        x0, gg = _hy_conv(z, conv_w, conv_b, d)
        tables = _dft_tables(length)
        filt = _hyena_filter(length, f_w1, f_b1, f_w2, f_b2, f_w3, f_b3, sin_freq)
        kre, kim = _hy_filter_spectrum(filt, d, tables)
        y = _hy_long_conv_gate(gg, x0, kre, kim, skip, tables)
        return _proj_res(y, w_o, res, gt)

    return one(x, sh, sc, x, gate), (one(ctx, csh, csc, ctx, cgate) if cgate is not None else None)


def _ml_conv_kernel(z_ref, w_ref, b_ref, o_ref, *t_refs, out_scale):
    y = _conv3(z_ref[0], w_ref, b_ref)
    y = y * jax.nn.sigmoid(y) * out_scale
    o_ref[0] = y
    for t_ref in t_refs:
        t_ref[0] = y.T


def _ml_conv(z, conv_w, conv_b, col0, width, out_scale, transposed):
    bsz, length, _ = z.shape
    tc = 256
    j0 = col0 // tc
    out_specs = [pl.BlockSpec((1, length, tc), lambda b, j: (b, 0, j))]
    out_shape = [jax.ShapeDtypeStruct((bsz, length, width), F32)]
    if transposed:
        out_specs.append(pl.BlockSpec((1, tc, length), lambda b, j: (b, j, 0)))
        out_shape.append(jax.ShapeDtypeStruct((bsz, width, length), F32))
    return pl.pallas_call(
        functools.partial(_ml_conv_kernel, out_scale=out_scale),
        grid=(bsz, width // tc),
        in_specs=[pl.BlockSpec((1, length, tc), lambda b, j: (b, 0, j0 + j)),
                  pl.BlockSpec((3, tc), lambda b, j: (0, j0 + j)),
                  pl.BlockSpec((1, tc), lambda b, j: (0, j0 + j))],
        out_specs=out_specs,
        out_shape=out_shape,
        compiler_params=_cparams("parallel", "parallel"),
        name="ml_conv",
    )(z, conv_w, conv_b.reshape(1, -1))


def _ml_gates_kernel(x_ref, g_ref, sh_ref, sc_ref, w_ref, b_ref, o_ref, t_ref):
    x = x_ref[0]
    ms = jnp.mean(x * x, axis=-1, keepdims=True)
    h = x * lax.rsqrt(ms + NORM_EPS) * g_ref[...]
    h = h * (1.0 + sc_ref[0]) + sh_ref[0]
    gates = jnp.dot(h.astype(BF16), w_ref[...], preferred_element_type=F32) + b_ref[...]
    o_ref[0] = gates
    t_ref[0] = gates.T


def _ml_gates(x, g, shift, scale, w_gate, gate_b):
    bsz, length, d = x.shape
    pad = LANES - w_gate.shape[1]
    w = jnp.pad(w_gate, ((0, 0), (0, pad))).astype(BF16)
    b = jnp.pad(gate_b, (0, pad)).reshape(1, LANES)
    tm = _pick(length, (1024, 256))
    return pl.pallas_call(
        _ml_gates_kernel,
        grid=(bsz, length // tm),
        in_specs=[
            pl.BlockSpec((1, tm, d), lambda b_, i: (b_, i, 0)),
            pl.BlockSpec((1, d), lambda b_, i: (0, 0)),
            pl.BlockSpec((1, 1, d), lambda b_, i: (b_, 0, 0)),
            pl.BlockSpec((1, 1, d), lambda b_, i: (b_, 0, 0)),
            pl.BlockSpec((d, LANES), lambda b_, i: (0, 0)),
            pl.BlockSpec((1, LANES), lambda b_, i: (0, 0)),
        ],
        out_specs=[pl.BlockSpec((1, tm, LANES), lambda b_, i: (b_, i, 0)),
                   pl.BlockSpec((1, LANES, tm), lambda b_, i: (b_, 0, i))],
        out_shape=[jax.ShapeDtypeStruct((bsz, length, LANES), F32),
                   jax.ShapeDtypeStruct((bsz, LANES, length), F32)],
        compiler_params=_cparams("parallel", "parallel"),
        name="ml_gates",
    )(x, g.reshape(1, d), shift.reshape(bsz, 1, d), scale.reshape(bsz, 1, d), w, b)


def _log_sigmoid(x):
    return jnp.minimum(x, 0.0) - jnp.log1p(jnp.exp(-jnp.abs(x)))


def _ml_chunk_pair(q2, kt2, k2, v2, gates, i_rows, f_rows, f_lanes, state, reverse):
    ct2, n2, ms = state
    tlen = q2.shape[0]
    dk, dv = ML_QK_DIM, ML_V_DIM
    ti = lax.broadcasted_iota(jnp.int32, (tlen, tlen), 0)
    si = lax.broadcasted_iota(jnp.int32, (tlen, tlen), 1)
    causal = (si >= ti) if reverse else (si <= ti)
    causal_t = (ti >= si) if reverse else (ti <= si)
    lane = lax.broadcasted_iota(jnp.int32, (1, LANES), 1)
    sub = lax.broadcasted_iota(jnp.int32, (LANES, 1), 0)
    kt_b = kt2.astype(BF16)
    k_b = k2.astype(BF16)
    v_b = v2.astype(BF16)
    ct_b = ct2.astype(BF16)
    hs, ws, decays, m_news = [], [], [], []
    for j in range(2):
        own = (lane >= dk) if j else (lane < dk)
        lf_row = _log_sigmoid(f_rows[j])
        f_col = jnp.where(lane == f_lanes[j], gates, 0.0).sum(axis=-1, keepdims=True)
        lf_col = _log_sigmoid(f_col)
        b_col = jnp.where(causal, lf_row, 0.0).sum(axis=-1, keepdims=True)
        b_row = jnp.where(causal_t, lf_col, 0.0).sum(axis=0, keepdims=True)
        log_d = jnp.where(causal, b_col - b_row + i_rows[j], -jnp.inf)
        m_inter = b_col + ms[j]
        m_t = jnp.maximum(log_d.max(axis=-1, keepdims=True), m_inter)
        qj = jnp.where(own, q2, 0.0)
        qj_b = qj.astype(BF16)
        s = jnp.dot(qj_b, kt_b, preferred_element_type=F32) * jnp.exp(log_d - m_t)
        inter = jnp.exp(m_inter - m_t)
        sv = jnp.dot(s.astype(BF16), v_b, preferred_element_type=F32)[:, j * dv:(j + 1) * dv]
        num = sv + inter * jnp.dot(qj_b, ct_b, preferred_element_type=F32)
        folded = inter * (qj * n2)
        for c in range(0, tlen, LANES):
            folded = folded + s[:, c:c + LANES]
        qn = folded.sum(axis=-1, keepdims=True)
        hs.append(num / jnp.maximum(jnp.abs(qn), jnp.exp(-m_t)))
        b_end = b_row[:, 0:1] if reverse else b_row[:, tlen - 1:tlen]
        w_log = b_end - b_row + i_rows[j]
        m_new = jnp.maximum(b_end + ms[j], w_log.max(axis=-1, keepdims=True))
        ws.append(jnp.exp(w_log - m_new))
        decays.append(jnp.exp(b_end + ms[j] - m_new))
        m_news.append(m_new)
    head0_rows = sub < dk
    head0_lanes = lane < dk
    kw = (kt2 * jnp.where(head0_rows, ws[0], ws[1])).astype(BF16)
    upd = jnp.dot(kw, v_b, preferred_element_type=F32)
    ct_new = (jnp.where(head0_rows, decays[0], decays[1]) * ct2
              + jnp.where(head0_rows, upd[:, :dv], upd[:, dv:]))
    wk = [jnp.dot(w.astype(BF16), k_b, preferred_element_type=F32) for w in ws]
    n_new = jnp.where(head0_lanes, decays[0], decays[1]) * n2 + jnp.where(head0_lanes, wk[0], wk[1])
    return jnp.concatenate(hs, axis=-1), (ct_new, n_new, tuple(m_news))


def _mlstm_kernel(ql_ref, kl_ref, ktl_ref, vl_ref, ol_ref, gl_ref, gtl_ref,
                  qc_ref, kc_ref, ktc_ref, vc_ref, oc_ref, gc_ref, gtc_ref, ng_ref,
                  yl_ref, yc_ref, hfl_ref, hbl_ref, hfc_ref, hbc_ref):
    tlen = ML_CHUNK
    dv, nh = ML_V_DIM, ML_HEADS
    n_lat = ql_ref.shape[1] // tlen
    n_ctx = qc_ref.shape[1] // tlen
    head0 = 2 * pl.program_id(1)

    def run(refs, c, state, direction):
        q_ref, k_ref, kt_ref, v_ref, g_ref, gt_ref, hf_ref, hb_ref = refs
        rows = pl.ds(pl.multiple_of(c * tlen, tlen), tlen)
        gate_i, gate_f = 2 * direction * nh, (2 * direction + 1) * nh
        i_rows = [gt_ref[0, pl.ds(gate_i + head0 + j, 1), rows] for j in range(2)]
        f_rows = [gt_ref[0, pl.ds(gate_f + head0 + j, 1), rows] for j in range(2)]
        f_lanes = [gate_f + head0 + j for j in range(2)]
        h, state = _ml_chunk_pair(q_ref[0, rows, :], kt_ref[0, :, rows], k_ref[0, rows, :], v_ref[0, rows, :],
                                  g_ref[0, rows, :], i_rows, f_rows, f_lanes, state, direction == 1)
        (hb_ref if direction else hf_ref)[rows, :] = h
        return state

    def sweep(refs, n_chunks, states):
        def body(c, sts):
            return run(refs, c, sts[0], 0), run(refs, n_chunks - 1 - c, sts[1], 1)
        return lax.fori_loop(0, n_chunks, body, states)

    zero = (jnp.zeros((LANES, dv), F32), jnp.zeros((1, LANES), F32),
            (jnp.zeros((1, 1), F32), jnp.zeros((1, 1), F32)))
    lat = (ql_ref, kl_ref, ktl_ref, vl_ref, gl_ref, gtl_ref, hfl_ref, hbl_ref)
    ctx = (qc_ref, kc_ref, ktc_ref, vc_ref, gc_ref, gtc_ref, hfc_ref, hbc_ref)
    sweep(lat, n_lat, sweep(ctx, n_ctx, (zero, zero)))

    def finish(hf_ref, hb_ref, o_ref, y_ref):
        for j in range(2):
            vs = slice(j * dv, (j + 1) * dv)
            h = hf_ref[:, vs] + hb_ref[:, vs]
            h = h * lax.rsqrt(jnp.mean(h * h, axis=-1, keepdims=True) + NORM_EPS) * ng_ref[:, vs]
            y_ref[0, :, vs] = (h * jax.nn.sigmoid(o_ref[0, :, vs])).astype(y_ref.dtype)

    finish(hfl_ref, hbl_ref, ol_ref, yl_ref)
    finish(hfc_ref, hbc_ref, oc_ref, yc_ref)


def _mlstm_mixer(x, ctx, g, sh, sc, csh, csc, w_in, conv_w, conv_b, gate_b, out_norm_g, w_o, gate, cgate):
    nh, dk, dv = ML_HEADS, ML_QK_DIM, ML_V_DIM
    nqk = 2 * nh * dk
    wide = nqk + 2 * nh * dv
    w_main = w_in[:, :wide].astype(BF16)
    w_gate = w_in[:, wide:]

    def project(u, shift, scale):
        z = _proj(u, w_main, g, shift, scale)
        gates, gates_t = _ml_gates(u, g, shift, scale, w_gate, gate_b)
        (q,) = _ml_conv(z, conv_w, conv_b, 0, nh * dk, 1.0, False)
        k, kt = _ml_conv(z, conv_w, conv_b, nh * dk, nh * dk, dk ** -0.5, True)
        return z, q, k, kt, gates, gates_t

    zl, ql, kl, ktl, gl, gtl = project(x, sh, sc)
    zc, qc, kc, ktc, gc, gtc = project(ctx, csh, csc)
    bsz, length, d = x.shape
    lc = ctx.shape[1]
    npair = nh // 2
    qw, vw = 2 * dk, 2 * dv
    v_off, o_off = nqk // vw, (nqk + nh * dv) // vw

    def specs(n):
        return [
            pl.BlockSpec((1, n, qw), lambda b, h: (b, 0, h)),
            pl.BlockSpec((1, n, qw), lambda b, h: (b, 0, h)),
            pl.BlockSpec((1, qw, n), lambda b, h: (b, h, 0)),
            pl.BlockSpec((1, n, vw), lambda b, h: (b, 0, v_off + h)),
            pl.BlockSpec((1, n, vw), lambda b, h: (b, 0, o_off + h)),
            pl.BlockSpec((1, n, LANES), lambda b, h: (b, 0, 0)),
            pl.BlockSpec((1, LANES, n), lambda b, h: (b, 0, 0)),
        ]

    yl, yc = pl.pallas_call(
        _mlstm_kernel,
        grid=(bsz, npair),
        in_specs=specs(length) + specs(lc) + [pl.BlockSpec((1, vw), lambda b, h: (0, h))],
        out_specs=[pl.BlockSpec((1, length, vw), lambda b, h: (b, 0, h)),
                   pl.BlockSpec((1, lc, vw), lambda b, h: (b, 0, h))],
        out_shape=[jax.ShapeDtypeStruct((bsz, length, nh * dv), BF16),
                   jax.ShapeDtypeStruct((bsz, lc, nh * dv), BF16)],
        scratch_shapes=[pltpu.VMEM((length, vw), F32), pltpu.VMEM((length, vw), F32),
                        pltpu.VMEM((lc, vw), F32), pltpu.VMEM((lc, vw), F32)],
        compiler_params=_cparams("parallel", "parallel"),
        name="mlstm",
    )(ql, kl, ktl, zl, zl, gl, gtl, qc, kc, ktc, zc, zc, gc, gtc, out_norm_g.reshape(1, nh * dv))
    return _proj_res(yl, w_o, x, gate), (_proj_res(yc, w_o, ctx, cgate) if cgate is not None else None)


def _router_kernel(x_ref, g_ref, sh_ref, sc_ref, rw_ref, h_ref, aff_ref):
    x = x_ref[0]
    ms = jnp.mean(x * x, axis=-1, keepdims=True)
    h = x * lax.rsqrt(ms + NORM_EPS) * g_ref[...]
    h = h * (1.0 + sc_ref[0]) + sh_ref[0]
    h_ref[0] = h.astype(BF16)
    logits = lax.dot_general(rw_ref[...], h, (((1,), (1,)), ((), ())), precision=HIGHEST,
                             preferred_element_type=F32)
    mx = logits.max(axis=0, keepdims=True)
    p = jnp.exp(logits - mx)
    aff_ref[0] = p / p.sum(axis=0, keepdims=True)


def _router(x, g, shift, scale, router_w):
    bsz, length, d = x.shape
    ne = router_w.shape[1]
    tm = _pick(length, (1024, 256))
    return pl.pallas_call(
        _router_kernel,
        grid=(bsz, length // tm),
        in_specs=[
            pl.BlockSpec((1, tm, d), lambda b, i: (b, i, 0)),
            pl.BlockSpec((1, d), lambda b, i: (0, 0)),
            pl.BlockSpec((1, 1, d), lambda b, i: (b, 0, 0)),
            pl.BlockSpec((1, 1, d), lambda b, i: (b, 0, 0)),
            pl.BlockSpec((ne, d), lambda b, i: (0, 0)),
        ],
        out_specs=[pl.BlockSpec((1, tm, d), lambda b, i: (b, i, 0)),
                   pl.BlockSpec((1, ne, tm), lambda b, i: (b, 0, i))],
        out_shape=[jax.ShapeDtypeStruct((bsz, length, d), BF16),
                   jax.ShapeDtypeStruct((bsz, ne, length), F32)],
        compiler_params=_cparams("parallel", "parallel"),
        name="moe_router",
    )(x, g.reshape(1, d), shift.reshape(bsz, 1, d), scale.reshape(bsz, 1, d), router_w.T)


def _excl_cumsum_lanes(flags):
    rows, length = flags.shape
    cw = min(length, 256)
    si = lax.broadcasted_iota(jnp.int32, (cw, cw), 0)
    ti = lax.broadcasted_iota(jnp.int32, (cw, cw), 1)
    upper = jnp.where(si < ti, 1.0, 0.0).astype(BF16)
    carry = jnp.zeros((rows, 1), F32)
    parts = []
    for c in range(length // cw):
        blk = flags[:, c * cw:(c + 1) * cw]
        parts.append(jnp.dot(blk.astype(BF16), upper, preferred_element_type=F32) + carry)
        carry = carry + blk.sum(axis=-1, keepdims=True)
    return parts[0] if len(parts) == 1 else jnp.concatenate(parts, axis=-1)


def _select_kernel(aff_ref, pos_ref, *, cap):
    a = aff_ref[...]
    rows = a.shape[0]
    capf = float(cap)

    def bisect(_, lohi):
        lo, hi = lohi
        mid = 0.5 * (lo + hi)
        cnt = jnp.where(a >= mid, 1.0, 0.0).sum(axis=-1, keepdims=True)
        ge = cnt >= capf
        return jnp.where(ge, mid, lo), jnp.where(ge, hi, mid)

    lo, hi = lax.fori_loop(0, SELECT_BISECTIONS, bisect,
                           (jnp.zeros((rows, 1), F32), jnp.full((rows, 1), 2.0, F32)))
    above = a >= hi
    tie = jnp.logical_and(a >= lo, jnp.logical_not(above))
    n_above = jnp.where(above, 1.0, 0.0).sum(axis=-1, keepdims=True)
    tie_rank = _excl_cumsum_lanes(jnp.where(tie, 1.0, 0.0))
    sel = jnp.logical_or(above, jnp.logical_and(tie, tie_rank < capf - n_above))
    slot = _excl_cumsum_lanes(jnp.where(sel, 1.0, 0.0))
    pos_ref[...] = jnp.where(sel, slot, -1.0).astype(jnp.int32)


def _select(aff2d, cap):
    return pl.pallas_call(
        functools.partial(_select_kernel, cap=cap),
        out_shape=jax.ShapeDtypeStruct(aff2d.shape, jnp.int32),
        compiler_params=pltpu.CompilerParams(vmem_limit_bytes=V7X_VMEM_LIMIT_BYTES),
        name="moe_select",
    )(aff2d)


def _gather_kernel(h_ref, pos_ref, aff_ref, xg_ref, gs_ref, *, cap):
    pos = pos_ref[0, 0]
    length = pos.shape[1]
    slot = lax.broadcasted_iota(jnp.int32, (cap, length), 0)
    hit = slot == pos
    onehot = jnp.where(hit, 1.0, 0.0).astype(BF16)
    xg_ref[0] = jnp.dot(onehot, h_ref[0], preferred_element_type=F32).astype(BF16)
    gs_ref[0] = jnp.where(hit, aff_ref[0, 0], 0.0).sum(axis=-1, keepdims=True)


def _gather(h, pos, aff, cap):
    bsz, length, d = h.shape
    ne = pos.shape[1]
    return pl.pallas_call(
        functools.partial(_gather_kernel, cap=cap),
        grid=(bsz, ne),
        in_specs=[
            pl.BlockSpec((1, length, d), lambda b, e: (b, 0, 0)),
            pl.BlockSpec((1, 1, 1, length), lambda b, e: (b, e, 0, 0)),
            pl.BlockSpec((1, 1, 1, length), lambda b, e: (b, e, 0, 0)),
        ],
        out_specs=[pl.BlockSpec((1, cap, d), lambda b, e: (e, b, 0)),
                   pl.BlockSpec((1, cap, 1), lambda b, e: (e, b, 0))],
        out_shape=[jax.ShapeDtypeStruct((ne, bsz * cap, d), BF16),
                   jax.ShapeDtypeStruct((ne, bsz * cap, 1), F32)],
        compiler_params=_cparams("parallel", "arbitrary"),
        name="moe_gather",
    )(h, pos.reshape(bsz, ne, 1, length), aff.reshape(bsz, ne, 1, length))


def _ffn_kernel(*refs, n_groups, n_up, tf):
    xg_refs = refs[:n_groups]
    gs_refs = refs[n_groups:2 * n_groups]
    wg_ref, wu_ref, wd_ref = refs[2 * n_groups:2 * n_groups + 3]
    y_refs = refs[2 * n_groups + 3:3 * n_groups + 3]
    z_refs = refs[3 * n_groups + 3:]
    step = pl.program_id(1)

    @pl.when(step < n_up)
    def _():
        wg = wg_ref[0, 0].astype(BF16)
        wu = wu_ref[0, 0].astype(BF16)
        cols = pl.ds(pl.multiple_of(step * tf, tf), tf)
        for xg_ref, z_ref in zip(xg_refs, z_refs):
            xg = xg_ref[0]
            a = jnp.dot(xg, wg, preferred_element_type=F32)
            u = jnp.dot(xg, wu, preferred_element_type=F32)
            z_ref[:, cols] = (a * jax.nn.sigmoid(a) * u).astype(BF16)

    @pl.when(step >= n_up)
    def _():
        wd = wd_ref[0, 0].astype(BF16)
        for gs_ref, y_ref, z_ref in zip(gs_refs, y_refs, z_refs):
            y = jnp.dot(z_ref[...], wd, preferred_element_type=F32)
            y_ref[0] = (y * gs_ref[0]).astype(BF16)


def _expert_ffn(xgs, gss, w_gate, w_up, w_down, layer):
    ne, _, d = xgs[0].shape
    ff = w_gate.shape[3]
    tf = 256
    tn = 256
    n_up, n_down = ff // tf, d // tn
    n = len(xgs)
    tok = lambda a: pl.BlockSpec((1,) + a.shape[1:], lambda e, s: (e, 0, 0))
    up_tile = lambda e, s: (layer, e, 0, jnp.minimum(s, n_up - 1))
    down_tile = lambda e, s: (layer, e, 0, jnp.maximum(s - n_up, 0))
    return pl.pallas_call(
        functools.partial(_ffn_kernel, n_groups=n, n_up=n_up, tf=tf),
        grid=(ne, n_up + n_down),
        in_specs=[tok(a) for a in xgs] + [tok(a) for a in gss] + [
            pl.BlockSpec((1, 1, d, tf), up_tile),
            pl.BlockSpec((1, 1, d, tf), up_tile),
            pl.BlockSpec((1, 1, ff, tn), down_tile),
        ],
        out_specs=[pl.BlockSpec((1, a.shape[1], tn), lambda e, s: (e, 0, jnp.maximum(s - n_up, 0)))
                   for a in xgs],
        out_shape=[jax.ShapeDtypeStruct(a.shape, BF16) for a in xgs],
        scratch_shapes=[pltpu.VMEM((a.shape[1], ff), BF16) for a in xgs],
        compiler_params=_cparams("parallel", "arbitrary"),
        name="moe_ffn",
    )(*xgs, *gss, w_gate, w_up, w_down)


def _combine_kernel(post_ref, y_ref, x_ref, gt_ref, fg_ref, o_ref, *, cap, final_norm):
    post = post_ref[0]
    tl, ne = post.shape
    d = y_ref.shape[2]
    slot = lax.broadcasted_iota(jnp.int32, (tl, cap), 1)
    hits = [jnp.where(post[:, e:e + 1] == slot, 1.0, 0.0).astype(BF16) for e in range(ne)]
    if cap % LANES == 0:
        acc = jnp.dot(jnp.concatenate(hits, axis=1), y_ref[...].reshape(ne * cap, d),
                      preferred_element_type=F32)
    else:
        acc = jnp.zeros((tl, d), F32)
        for e in range(ne):
            acc = acc + jnp.dot(hits[e], y_ref[e], preferred_element_type=F32)
    out = x_ref[0] + gt_ref[0] * acc
    if final_norm:
        ms = jnp.mean(out * out, axis=-1, keepdims=True)
        out = out * lax.rsqrt(ms + NORM_EPS) * fg_ref[...]
    o_ref[0] = out


def _combine(pos_t, y, x, gate, cap, final_g=None):
    bsz, length, d = x.shape
    ne = y.shape[0]
    tl = _pick(length, (512, 256))
    final_norm = final_g is not None
    fg = (final_g if final_norm else jnp.ones((d,), F32)).reshape(1, d)
    return pl.pallas_call(
        functools.partial(_combine_kernel, cap=cap, final_norm=final_norm),
        grid=(bsz, length // tl),
        in_specs=[
            pl.BlockSpec((1, tl, ne), lambda b, i: (b, i, 0)),
            pl.BlockSpec((ne, cap, d), lambda b, i: (0, b, 0)),
            pl.BlockSpec((1, tl, d), lambda b, i: (b, i, 0)),
            pl.BlockSpec((1, 1, d), lambda b, i: (b, 0, 0)),
            pl.BlockSpec((1, d), lambda b, i: (0, 0)),
        ],
        out_specs=pl.BlockSpec((1, tl, d), lambda b, i: (b, i, 0)),
        out_shape=jax.ShapeDtypeStruct(x.shape, F32),
        compiler_params=_cparams("parallel", "arbitrary"),
        name="moe_combine",
    )(pos_t, y, x, gate.reshape(bsz, 1, d), fg)


def _route(x, g, shift, scale, router_w):
    bsz, length, _ = x.shape
    ne = router_w.shape[1]
    cap = max(1, EC_CAPACITY_FACTOR * length // ne)
    h, aff = _router(x, g, shift, scale, router_w)
    pos = _select(aff.reshape(bsz * ne, length), cap).reshape(bsz, ne, length)
    xg, gs = _gather(h, pos, aff, cap)
    return xg, gs, pos.transpose(0, 2, 1), cap


def _moe(x, ctx, g, sh, sc, gate, csh, csc, cgate, router_w, w_gate, w_up, w_down, layer, final_g=None):
    xg, gs, pos_t, cap = _route(x, g, sh, sc, router_w)
    if ctx is None:
        (y,) = _expert_ffn([xg], [gs], w_gate, w_up, w_down, layer)
        return _combine(pos_t, y, x, gate, cap, final_g), None
    xg_c, gs_c, pos_tc, cap_c = _route(ctx, g, csh, csc, router_w)
    y, y_c = _expert_ffn([xg, xg_c], [gs, gs_c], w_gate, w_up, w_down, layer)
    return _combine(pos_t, y, x, gate, cap), _combine(pos_tc, y_c, ctx, cgate, cap_c)


def kernel(x, c, ctx, c_ctx, mod_w, mod_b, norm_mix_g, norm_ffn_g, router_w, moe_w_gate, moe_w_up, moe_w_down, na_w_qkv, na_rpb, na_w_o, mla_w_in, mla_q_norm_g, mla_w_q_b, mla_kv_norm_g, mla_w_kv_b, mla_w_o, hy_w_in, hy_conv_w, hy_conv_b, hy_f_w1, hy_f_b1, hy_f_w2, hy_f_b2, hy_f_w3, hy_f_b3, hy_sin_freq, hy_skip, hy_w_o, ml_w_in, ml_conv_w, ml_conv_b, ml_gate_b, ml_out_norm_g, ml_w_o, final_norm_g):
    bsz, _, d = x.shape
    depth = mod_w.shape[0]
    n_mixers = 4
    cond = jnp.concatenate([c, c_ctx[None, :]], axis=0)
    cond = jnp.pad(cond, ((0, (-cond.shape[0]) % 8), (0, 0)))
    mods = _modulation(cond, mod_w, mod_b)
    for i in range(depth):
        last = i == depth - 1
        mod = mods[i]
        sh1, sc1, g1, sh2, sc2, g2 = [mod[:bsz, k * d:(k + 1) * d] for k in range(6)]
        bc = lambda v: jnp.broadcast_to(v[None, :], (bsz, d))
        csh1, csc1, cg1, csh2, csc2, cg2 = [bc(mod[bsz, k * d:(k + 1) * d]) for k in range(6)]
        if last:
            cg1 = None
        kind, j = i % n_mixers, i // n_mixers
        gm = norm_mix_g[i]
        if kind == 0:
            x, ctx_new = _na_mixer(x, ctx, gm, sh1, sc1, csh1, csc1, na_w_qkv[j], na_rpb[j], na_w_o[j], g1, cg1)
        elif kind == 1:
            x, ctx_new = _mla_mixer(x, ctx, gm, sh1, sc1, csh1, csc1, mla_w_in[j], mla_q_norm_g[j],
                                    mla_w_q_b[j], mla_kv_norm_g[j], mla_w_kv_b[j], mla_w_o[j], g1, cg1)
        elif kind == 2:
            x, ctx_new = _hyena_mixer(x, ctx, gm, sh1, sc1, csh1, csc1, hy_w_in[j], hy_conv_w[j], hy_conv_b[j],
                                      hy_f_w1[j], hy_f_b1[j], hy_f_w2[j], hy_f_b2[j], hy_f_w3[j], hy_f_b3[j],
                                      hy_sin_freq[j], hy_skip[j], hy_w_o[j], g1, cg1)
        else:
            x, ctx_new = _mlstm_mixer(x, ctx, gm, sh1, sc1, csh1, csc1, ml_w_in[j], ml_conv_w[j], ml_conv_b[j],
                                      ml_gate_b[j], ml_out_norm_g[j], ml_w_o[j], g1, cg1)
        if last:
            x, _ = _moe(x, None, norm_ffn_g[i], sh2, sc2, g2, None, None, None,
                        router_w[i], moe_w_gate, moe_w_up, moe_w_down, i, final_norm_g)
        else:
            x, ctx = _moe(x, ctx_new, norm_ffn_g[i], sh2, sc2, g2, csh2, csc2, cg2,
                          router_w[i], moe_w_gate, moe_w_up, moe_w_down, i)
    return x
```

```python
import functools
import math

import numpy as np
import jax
import jax.numpy as jnp
from jax import lax
from jax.experimental import pallas as pl
from jax.experimental.pallas import tpu as pltpu

F32 = jnp.float32
BF16 = jnp.bfloat16
HIGHEST = lax.Precision.HIGHEST

V7X_VMEM_LIMIT_BYTES = 56 * 1024 * 1024
LANES = 128

GRID_W = 64
NORM_EPS = 1e-6
ROPE_BASE = 10000.0

NA_HEADS = 16
NA_HEAD_DIM = 64
NA_WIN_H = 8
NA_WIN_W = 16
NA_QROWS = 4
NA_KROWS = NA_QROWS - 1 + NA_WIN_H

MLA_HEADS = 16
MLA_Q_RANK = 384
MLA_KV_RANK = 256
MLA_NOPE_DIM = 64
MLA_ROPE_DIM = 32
MLA_V_DIM = 64

HY_EMB_DIM = 33
HY_DECAY_TARGET = 1e-2
HY_FAST_DECAY = 0.3
HY_SLOW_DECAY = 1.5
HY_MOD_SHIFT = 0.05

ML_HEADS = 8
ML_V_DIM = 128
ML_QK_DIM = 64
ML_CHUNK = 256

N_EXPERTS = 16
EC_CAPACITY_FACTOR = 2
SELECT_BISECTIONS = 64


def _cparams(*sem):
    return pltpu.CompilerParams(dimension_semantics=sem, vmem_limit_bytes=V7X_VMEM_LIMIT_BYTES)


def _pick(n, prefs):
    for p in prefs:
        if n % p == 0:
            return p
    return n


def _proj_kernel(x_ref, g_ref, sh_ref, sc_ref, w_ref, o_ref, h_ref, *, norm):
    @pl.when(pl.program_id(2) == 0)
    def _():
        x = x_ref[0].astype(F32)
        if norm:
            ms = jnp.mean(x * x, axis=-1, keepdims=True)
            x = x * lax.rsqrt(ms + NORM_EPS) * g_ref[...]
            x = x * (1.0 + sc_ref[0]) + sh_ref[0]
        h_ref[...] = x.astype(BF16)

    o_ref[0] = jnp.dot(h_ref[...], w_ref[...].astype(BF16),
                       preferred_element_type=F32).astype(o_ref.dtype)


def _proj(x, w, g=None, shift=None, scale=None, out_dtype=F32):
    bsz, length, kdim = x.shape
    n = w.shape[1]
    norm = g is not None
    if not norm:
        g = jnp.ones((kdim,), F32)
    if shift is None:
        shift = jnp.zeros((bsz, kdim), F32)
        scale = jnp.zeros((bsz, kdim), F32)
    tm = _pick(length, (1024, 512, 256))
    tn = _pick(n, (1024, 768, 512, 384, 256, 128))
    return pl.pallas_call(
        functools.partial(_proj_kernel, norm=norm),
        grid=(bsz, length // tm, n // tn),
        in_specs=[
            pl.BlockSpec((1, tm, kdim), lambda b, i, j: (b, i, 0)),
            pl.BlockSpec((1, kdim), lambda b, i, j: (0, 0)),
            pl.BlockSpec((1, 1, kdim), lambda b, i, j: (b, 0, 0)),
            pl.BlockSpec((1, 1, kdim), lambda b, i, j: (b, 0, 0)),
            pl.BlockSpec((kdim, tn), lambda b, i, j: (0, j)),
        ],
        out_specs=pl.BlockSpec((1, tm, tn), lambda b, i, j: (b, i, j)),
        out_shape=jax.ShapeDtypeStruct((bsz, length, n), out_dtype),
        scratch_shapes=[pltpu.VMEM((tm, kdim), BF16)],
        compiler_params=_cparams("parallel", "parallel", "arbitrary"),
        name="proj",
    )(x, g.reshape(1, kdim).astype(F32), shift.reshape(bsz, 1, kdim), scale.reshape(bsz, 1, kdim),
      w.astype(BF16))


def _proj_res_kernel(y_ref, w_ref, r_ref, gt_ref, o_ref):
    acc = jnp.dot(y_ref[0].astype(BF16), w_ref[...].astype(BF16), preferred_element_type=F32)
    o_ref[0] = r_ref[0] + gt_ref[0] * acc


def _proj_res(y, w, res, gate):
    bsz, length, kdim = y.shape
    n = w.shape[1]
    tm = _pick(length, (1024, 512, 256))
    tn = _pick(n, (1024, 512, 256, 128))
    return pl.pallas_call(
        _proj_res_kernel,
        grid=(bsz, length // tm, n // tn),
        in_specs=[
            pl.BlockSpec((1, tm, kdim), lambda b, i, j: (b, i, 0)),
            pl.BlockSpec((kdim, tn), lambda b, i, j: (0, j)),
            pl.BlockSpec((1, tm, tn), lambda b, i, j: (b, i, j)),
            pl.BlockSpec((1, 1, tn), lambda b, i, j: (b, 0, j)),
        ],
        out_specs=pl.BlockSpec((1, tm, tn), lambda b, i, j: (b, i, j)),
        out_shape=jax.ShapeDtypeStruct((bsz, length, n), F32),
        compiler_params=_cparams("parallel", "parallel", "arbitrary"),
        name="proj_res",
    )(y, w.astype(BF16), res, gate.reshape(bsz, 1, n))


def _mod_kernel(c_ref, w_ref, b_ref, o_ref):
    c = c_ref[...]
    s = c * jax.nn.sigmoid(c)
    o_ref[0] = jnp.dot(s.astype(BF16), w_ref[0].astype(BF16), preferred_element_type=F32) + b_ref[0]


def _modulation(cc, w, b):
    rows, d = cc.shape
    depth, _, n = w.shape
    tn = 512
    return pl.pallas_call(
        _mod_kernel,
        grid=(depth, n // tn),
        in_specs=[
            pl.BlockSpec((rows, d), lambda l, j: (0, 0)),
            pl.BlockSpec((1, d, tn), lambda l, j: (l, 0, j)),
            pl.BlockSpec((1, 1, tn), lambda l, j: (l, 0, j)),
        ],
        out_specs=pl.BlockSpec((1, rows, tn), lambda l, j: (l, 0, j)),
        out_shape=jax.ShapeDtypeStruct((depth, rows, n), F32),
        compiler_params=_cparams("parallel", "parallel"),
        name="modulation",
    )(cc, w, b.reshape(depth, 1, n))


ATTN_KEY_CHUNK = 512
LOG2_E = math.log2(math.e)


def _softmax_pv(s_parts, v_parts):
    m = den = acc = None
    for s, v in zip(s_parts, v_parts):
        m_part = s.max(axis=-1, keepdims=True)
        m_new = m_part if m is None else jnp.maximum(m, m_part)
        p = jnp.exp2(s - m_new)
        l = p.sum(axis=-1, keepdims=True)
        o = jnp.dot(p.astype(BF16), v, preferred_element_type=F32)
        if m is None:
            den, acc = l, o
        else:
            alpha = jnp.exp2(m - m_new)
            den, acc = alpha * den + l, alpha * acc + o
        m = m_new
    return acc / den


def _key_chunks(n):
    step = ATTN_KEY_CHUNK if n % ATTN_KEY_CHUNK == 0 else n
    return [slice(c, c + step) for c in range(0, n, step)]


HEAD_V = 64


def _pair_queries(q2, hw):
    if hw == LANES:
        return [q2[:, :LANES], q2[:, LANES:]]
    lane = lax.broadcasted_iota(jnp.int32, (1, LANES), 1)
    return [jnp.where(lane < hw, q2, jnp.zeros_like(q2)), jnp.where(lane >= hw, q2, jnp.zeros_like(q2))]


def _pair_keys(k_ref, hw, j, rows=None):
    rows = slice(None) if rows is None else rows
    return k_ref[0, rows, j * LANES:(j + 1) * LANES] if hw == LANES else k_ref[0, rows, :]


def _pair_select(o0, o1):
    lane = lax.broadcasted_iota(jnp.int32, (1, LANES), 1)
    return jnp.where(lane < HEAD_V, o0, o1)


def _pair_attn_kernel(*refs, nparts, hw):
    q_ref, o_ref = refs[0], refs[-1]
    k_refs = refs[1:1 + nparts]
    v_refs = refs[1 + nparts:1 + 2 * nparts]
    nt = (((1,), (1,)), ((), ()))
    outs = []
    for j, q in enumerate(_pair_queries(q_ref[0], hw)):
        s_parts, v_parts = [], []
        for k_ref, v_ref in zip(k_refs, v_refs):
            for rows in _key_chunks(k_ref.shape[1]):
                s_parts.append(lax.dot_general(q, _pair_keys(k_ref, hw, j, rows), nt,
                                               preferred_element_type=F32))
                v_parts.append(v_ref[0, rows, :])
        outs.append(_softmax_pv(s_parts, v_parts))
    o_ref[0] = _pair_select(*outs).astype(o_ref.dtype)


def _pair_attention(q_src, kv_srcs, hw, nh):
    q, q_off = q_src
    bsz, lq = q.shape[0], q.shape[1]
    tq = _pick(lq, (1024, 512, 256))
    qw = 2 * hw
    in_specs = [pl.BlockSpec((1, tq, qw), lambda b, h, i: (b, i, q_off // qw + h))]
    args = [q]
    for k_arr, k_off, _, _ in kv_srcs:
        in_specs.append(pl.BlockSpec((1, k_arr.shape[1], qw), lambda b, h, i, o=k_off // qw: (b, 0, o + h)))
        args.append(k_arr)
    for _, _, v_arr, v_off in kv_srcs:
        in_specs.append(pl.BlockSpec((1, v_arr.shape[1], LANES), lambda b, h, i, o=v_off // LANES: (b, 0, o + h)))
        args.append(v_arr)
    return pl.pallas_call(
        functools.partial(_pair_attn_kernel, nparts=len(kv_srcs), hw=hw),
        grid=(bsz, nh // 2, lq // tq),
        in_specs=in_specs,
        out_specs=pl.BlockSpec((1, tq, LANES), lambda b, h, i: (b, i, h)),
        out_shape=jax.ShapeDtypeStruct((bsz, lq, nh * HEAD_V), BF16),
        compiler_params=_cparams("parallel", "parallel", "arbitrary"),
        name="attention",
    )(*args)


def _na_bias_table(rpb, rows):
    nh, n_dr, n_dc = rpb.shape
    kh = min(NA_WIN_H, rows)
    nblk = rows // NA_QROWS
    w = GRID_W
    qc = np.arange(w)[:, None]
    kc = np.arange(w)[None, :]
    cs = np.clip(qc - NA_WIN_W // 2, 0, w - NA_WIN_W)
    col_ok = (kc >= cs) & (kc < cs + NA_WIN_W)
    dc = np.clip(kc - qc, -(NA_WIN_W - 1), NA_WIN_W - 1) + NA_WIN_W - 1
    pick_dc = (dc.reshape(1, w * w) == np.arange(n_dc)[:, None]).astype(np.float32)
    tile = jnp.dot(rpb.reshape(nh * n_dr, n_dc).astype(F32), jnp.asarray(pick_dc), precision=HIGHEST)
    tile = jnp.where(col_ok, tile.reshape(nh, n_dr, w, w), -1e30)
    masked = jnp.full((nh, w, w), -1e30, F32)
    tables = []
    for blk in (0, 1, nblk - 1):
        start = int(np.clip(blk * NA_QROWS - NA_WIN_H // 2, 0, rows - NA_KROWS))
        q_rows = []
        for qr in range(NA_QROWS):
            r = blk * NA_QROWS + qr
            rs = int(np.clip(r - kh // 2, 0, rows - kh))
            tiles = [tile[:, kr - r + NA_WIN_H - 1] if rs <= kr < rs + kh else masked
                     for kr in range(start, start + NA_KROWS)]
            q_rows.append(jnp.concatenate(tiles, axis=-1))
        tables.append(jnp.concatenate(q_rows, axis=1))
    return jnp.stack(tables, axis=1)


NA_BLOCKS_PER_STEP = 4


def _na_kernel(q_ref, kl_ref, vl_ref, kc_ref, vc_ref, bias_ref, o_ref, *, nblk, rows):
    nt = (((1,), (1,)), ((), ()))
    hw = NA_HEAD_DIM
    tq = NA_QROWS * GRID_W
    for sub in range(NA_BLOCKS_PER_STEP):
        i = pl.program_id(2) * NA_BLOCKS_PER_STEP + sub
        start = jnp.clip(i * NA_QROWS - NA_WIN_H // 2, 0, rows - NA_KROWS) * GRID_W
        start = pl.multiple_of(start, GRID_W)
        pat = jnp.where(i == 0, 0, jnp.where(i == nblk - 1, 2, 1))
        local = pl.ds(start, NA_KROWS * GRID_W)
        qrows = slice(sub * tq, (sub + 1) * tq)
        outs = []
        for j, q in enumerate(_pair_queries(q_ref[0, qrows, :], hw)):
            s_ctx = lax.dot_general(q, _pair_keys(kc_ref, hw, j), nt, preferred_element_type=F32)
            s_loc = (lax.dot_general(q, _pair_keys(kl_ref, hw, j, local), nt, preferred_element_type=F32)
                     + bias_ref[j, pat])
            outs.append(_softmax_pv([s_ctx, s_loc], [vc_ref[0], vl_ref[0, local, :]]))
        o_ref[0, qrows, :] = _pair_select(*outs).astype(o_ref.dtype)


def _na_mixer(x, ctx, g, sh, sc, csh, csc, w_qkv, rpb, w_o, gate, cgate):
    bsz, length, d = x.shape
    lc = ctx.shape[1]
    rows = length // GRID_W
    nblk = rows // NA_QROWS
    nh, hd = NA_HEADS, NA_HEAD_DIM
    w = jnp.concatenate([w_qkv[:, :d] * (hd ** -0.5 * LOG2_E), w_qkv[:, d:]], axis=1).astype(BF16)
    qkv = _proj(x, w, g, sh, sc, out_dtype=BF16)
    qkv_c = _proj(ctx, w, g, csh, csc, out_dtype=BF16)
    bias = _na_bias_table(rpb * LOG2_E, rows)
    tq = NA_QROWS * GRID_W
    tstep = NA_BLOCKS_PER_STEP * tq
    npair = nh // 2
    pw = 2 * hd
    y = pl.pallas_call(
        functools.partial(_na_kernel, nblk=nblk, rows=rows),
        grid=(npair, bsz, nblk // NA_BLOCKS_PER_STEP),
        in_specs=[
            pl.BlockSpec((1, tstep, pw), lambda h, b, i: (b, i, h)),
            pl.BlockSpec((1, length, pw), lambda h, b, i: (b, 0, npair + h)),
            pl.BlockSpec((1, length, pw), lambda h, b, i: (b, 0, 2 * npair + h)),
            pl.BlockSpec((1, lc, pw), lambda h, b, i: (b, 0, npair + h)),
            pl.BlockSpec((1, lc, pw), lambda h, b, i: (b, 0, 2 * npair + h)),
            pl.BlockSpec((2, 3, tq, NA_KROWS * GRID_W), lambda h, b, i: (h, 0, 0, 0)),
        ],
        out_specs=pl.BlockSpec((1, tstep, pw), lambda h, b, i: (b, i, h)),
        out_shape=jax.ShapeDtypeStruct((bsz, length, d), BF16),
        compiler_params=_cparams("parallel", "parallel", "arbitrary"),
        name="na_local",
    )(qkv, qkv, qkv, qkv_c, qkv_c, bias)
    if cgate is None:
        return _proj_res(y, w_o, x, gate), None
    yc = _pair_attention((qkv_c, 0), [(qkv_c, d, qkv_c, 2 * d)], hd, nh)
    return _proj_res(y, w_o, x, gate), _proj_res(yc, w_o, ctx, cgate)


def _axial_rope(length):
    t = jnp.arange(length)
    row = (t // GRID_W).astype(F32)
    col = (t % GRID_W).astype(F32)
    n_freq = MLA_ROPE_DIM // 4
    inv = ROPE_BASE ** (-jnp.arange(n_freq, dtype=F32) / n_freq)
    ang = jnp.concatenate([row[:, None] * inv, col[:, None] * inv], axis=-1)
    return jnp.cos(ang), jnp.sin(ang)


def _rot_half_cols(w):
    half = w.shape[-1] // 2
    return jnp.concatenate([-w[..., half:], w[..., :half]], axis=-1)


def _mla_weights(w_in, w_q_b, w_kv_b):
    nh, nope, rope = MLA_HEADS, MLA_NOPE_DIM, MLA_ROPE_DIM
    off = MLA_Q_RANK + MLA_KV_RANK
    w_in_ext = jnp.concatenate([w_in, _rot_half_cols(w_in[:, off:])], axis=1)
    rq, rkv = w_q_b.shape[0], w_kv_b.shape[0]
    wq = w_q_b.reshape(rq, nh, nope + rope)
    pad = jnp.zeros((rq, nh, LANES - nope - rope), F32)
    wq_pad = jnp.concatenate([wq, pad], axis=-1).reshape(rq, nh * LANES)
    wq_rot = jnp.concatenate([jnp.zeros((rq, nh, nope), F32), _rot_half_cols(wq[..., nope:]), pad],
                             axis=-1).reshape(rq, nh * LANES)
    wkv = w_kv_b.reshape(rkv, nh, nope + MLA_V_DIM)
    wk_pad = jnp.concatenate([wkv[..., :nope], jnp.zeros((rkv, nh, LANES - nope), F32)],
                             axis=-1).reshape(rkv, nh * LANES)
    wv = wkv[..., nope:].reshape(rkv, nh * MLA_V_DIM)
    place = np.zeros((rope, nh, LANES), np.float32)
    place[np.arange(rope), :, nope + np.arange(rope)] = 1.0
    place = jnp.asarray(place.reshape(rope, nh * LANES))
    return (w_in_ext,) + tuple(a.astype(BF16) for a in (wq_pad, wq_rot, wk_pad, wv, place))


def _mla_rope_tables(length, rope_on):
    if rope_on:
        cos, sin = _axial_rope(length)
    else:
        cos = jnp.ones((length, MLA_ROPE_DIM // 2), F32)
        sin = jnp.zeros((length, MLA_ROPE_DIM // 2), F32)
    ck = jnp.concatenate([cos, cos], axis=-1)
    sk = jnp.concatenate([sin, sin], axis=-1)
    one = jnp.ones((length, MLA_NOPE_DIM), F32)
    tail = LANES - MLA_NOPE_DIM - MLA_ROPE_DIM
    cq = jnp.concatenate([one, ck, one[:, :tail]], axis=-1)
    sq = jnp.concatenate([0.0 * one, sk, 0.0 * one[:, :tail]], axis=-1)
    return cq, sq, ck, sk


def _mla_qkv_kernel(z_ref, gq_ref, gkv_ref, wq_ref, wqr_ref, wk_ref, wv_ref, place_ref,
                    cq_ref, sq_ref, ck_ref, sk_ref, q_ref, k_ref, v_ref):
    z = z_ref[0]
    off = MLA_Q_RANK + MLA_KV_RANK
    rope = MLA_ROPE_DIM

    def norm(a, g_ref):
        ms = jnp.mean(a * a, axis=-1, keepdims=True)
        return (a * lax.rsqrt(ms + NORM_EPS) * g_ref[...]).astype(BF16)

    cq = norm(z[:, :MLA_Q_RANK], gq_ref)
    ckv = norm(z[:, MLA_Q_RANK:off], gkv_ref)
    k_rope = (z[:, off:off + rope] * ck_ref[...] + z[:, off + rope:off + 2 * rope] * sk_ref[...]).astype(BF16)
    qa = jnp.dot(cq, wq_ref[...], preferred_element_type=F32)
    qb = jnp.dot(cq, wqr_ref[...], preferred_element_type=F32)
    cos_q = cq_ref[...]
    sin_q = sq_ref[...]
    for h in range(MLA_HEADS):
        cols = slice(h * LANES, (h + 1) * LANES)
        q_ref[0, :, cols] = (qa[:, cols] * cos_q + qb[:, cols] * sin_q).astype(BF16)
    k_ref[0] = (jnp.dot(ckv, wk_ref[...], preferred_element_type=F32)
                + jnp.dot(k_rope, place_ref[...], preferred_element_type=F32)).astype(BF16)
    v_ref[0] = jnp.dot(ckv, wv_ref[...], preferred_element_type=F32).astype(BF16)


def _mla_qkv(z, q_norm_g, kv_norm_g, weights, tables):
    bsz, length, zw = z.shape
    wq, wqr, wk, wv, place = weights
    nh = MLA_HEADS
    tm = _pick(length, (512, 256))
    full = lambda a: pl.BlockSpec(a.shape, lambda b, i: (0, 0))
    rows = lambda a: pl.BlockSpec((tm, a.shape[1]), lambda b, i: (i, 0))
    gq = q_norm_g.reshape(1, -1)
    gkv = kv_norm_g.reshape(1, -1)
    out = lambda w: pl.BlockSpec((1, tm, w), lambda b, i: (b, i, 0))
    return pl.pallas_call(
        _mla_qkv_kernel,
        grid=(bsz, length // tm),
        in_specs=[pl.BlockSpec((1, tm, zw), lambda b, i: (b, i, 0)), full(gq), full(gkv),
                  full(wq), full(wqr), full(wk), full(wv), full(place)] + [rows(t) for t in tables],
        out_specs=[out(nh * LANES), out(nh * LANES), out(nh * MLA_V_DIM)],
        out_shape=[jax.ShapeDtypeStruct((bsz, length, nh * LANES), BF16),
                   jax.ShapeDtypeStruct((bsz, length, nh * LANES), BF16),
                   jax.ShapeDtypeStruct((bsz, length, nh * MLA_V_DIM), BF16)],
        compiler_params=_cparams("parallel", "parallel"),
        name="mla_qkv",
    )(z, gq, gkv, wq, wqr, wk, wv, place, *tables)


def _mla_mixer(x, ctx, g, sh, sc, csh, csc, w_in, q_norm_g, w_q_b, kv_norm_g, w_kv_b, w_o, gate, cgate):
    nh = MLA_HEADS
    scale = (MLA_NOPE_DIM + MLA_ROPE_DIM) ** -0.5
    w_in_ext, *weights = _mla_weights(w_in, w_q_b, w_kv_b)

    def project(u, shift, scl, rope_on):
        z = _proj(u, w_in_ext, g, shift, scl)
        cos_q, sin_q, cos_k, sin_k = _mla_rope_tables(u.shape[1], rope_on)
        qs = scale * LOG2_E
        return _mla_qkv(z, q_norm_g, kv_norm_g, weights, (cos_q * qs, sin_q * qs, cos_k, sin_k))

    ql, kl, vl = project(x, sh, sc, True)
    qc, kc, vc = project(ctx, csh, csc, False)
    y = _pair_attention((ql, 0), [(kc, 0, vc, 0), (kl, 0, vl, 0)], LANES, nh)
    if cgate is None:
        return _proj_res(y, w_o, x, gate), None
    yc = _pair_attention((qc, 0), [(kc, 0, vc, 0)], LANES, nh)
    return _proj_res(y, w_o, x, gate), _proj_res(yc, w_o, ctx, cgate)


def _conv3(z, w_ref, b_ref):
    length = z.shape[0]
    t = lax.broadcasted_iota(jnp.int32, z.shape, 0)
    prev = jnp.where(t == 0, 0.0, pltpu.roll(z, 1, axis=0))
    nxt = jnp.where(t == length - 1, 0.0, pltpu.roll(z, length - 1, axis=0))
    return b_ref[...] + prev * w_ref[0:1, :] + z * w_ref[1:2, :] + nxt * w_ref[2:3, :]


def _hy_conv_kernel(z0_ref, z1_ref, z2_ref, w0_ref, w1_ref, w2_ref, b0_ref, b1_ref, b2_ref,
                    x0_ref, gg_ref):
    x0_ref[0] = _conv3(z0_ref[0], w0_ref, b0_ref)
    x1 = _conv3(z1_ref[0], w1_ref, b1_ref)
    v = _conv3(z2_ref[0], w2_ref, b2_ref)
    gg_ref[0] = v * x1


def _hy_conv(z, conv_w, conv_b, d):
    bsz, length, _ = z.shape
    tc = 256
    nb = d // tc
    zspec = lambda k: pl.BlockSpec((1, length, tc), lambda b, j: (b, 0, k * nb + j))
    wspec = lambda k: pl.BlockSpec((3, tc), lambda b, j: (0, k * nb + j))
    bspec = lambda k: pl.BlockSpec((1, tc), lambda b, j: (0, k * nb + j))
    ospec = pl.BlockSpec((1, length, tc), lambda b, j: (b, 0, j))
    cb = conv_b.reshape(1, 3 * d)
    return pl.pallas_call(
        _hy_conv_kernel,
        grid=(bsz, nb),
        in_specs=[zspec(0), zspec(1), zspec(2), wspec(0), wspec(1), wspec(2), bspec(0), bspec(1), bspec(2)],
        out_specs=[ospec, ospec],
        out_shape=[jax.ShapeDtypeStruct((bsz, length, d), F32)] * 2,
        compiler_params=_cparams("parallel", "parallel"),
        name="hy_conv",
    )(z, z, z, conv_w, conv_w, conv_w, cb, cb, cb)


def _dft_tables(length):
    n2 = 2 * length
    lo = GRID_W
    w = 2.0 * math.pi / n2
    f = jnp.arange(length, dtype=jnp.int32)[:, None]
    n_hi = jnp.arange(length // lo, dtype=jnp.int32)[None, :] * lo
    n_lo = jnp.arange(lo, dtype=jnp.int32)[None, :]
    ang_hi = ((f * n_hi) % n2).astype(F32) * w
    ang_lo = ((f * n_lo) % n2).astype(F32) * w
    c_hi, s_hi = jnp.cos(ang_hi)[:, :, None], jnp.sin(ang_hi)[:, :, None]
    c_lo, s_lo = jnp.cos(ang_lo)[:, None, :], jnp.sin(ang_lo)[:, None, :]
    cf = (c_hi * c_lo - s_hi * s_lo).reshape(length, length)
    sn = (s_hi * c_lo + c_hi * s_lo).reshape(length, length)
    n = jnp.arange(length, dtype=jnp.int32)[None, :]
    sgn_n = jnp.where(n % 2 == 0, 1.0, -1.0).astype(F32)
    sf = jnp.where(f == 0, sgn_n, -sn)
    return cf.astype(BF16), sf.astype(BF16), sf.T.astype(BF16)


def _hy_fwd_kernel(cf_ref, sf_ref, gg_ref, kre_ref, kim_ref, y_ref, gb_ref):
    j = pl.program_id(1)

    @pl.when(j == 0)
    def _():
        gb_ref[...] = gg_ref[0].astype(BF16)

    ure = jnp.dot(cf_ref[...], gb_ref[...], preferred_element_type=F32)
    uim = jnp.dot(sf_ref[...], gb_ref[...], preferred_element_type=F32)
    kre = kre_ref[...]
    kim = kim_ref[...]
    tf = ure.shape[0]
    row = lax.broadcasted_iota(jnp.int32, ure.shape, 0) + j * tf
    packed = row == 0
    yre = ure * kre - jnp.where(packed, 0.0, uim * kim)
    yim = uim * jnp.where(packed, kim, kre) + jnp.where(packed, 0.0, ure * kim)
    y_ref[0, 0] = yre.astype(BF16)
    y_ref[0, 1] = yim.astype(BF16)


def _hy_inv_kernel(ci_ref, si_ref, y_ref, gg_ref, x0_ref, skip_ref, o_ref):
    y = (jnp.dot(ci_ref[...], y_ref[0, 0], preferred_element_type=F32)
         + jnp.dot(si_ref[...], y_ref[0, 1], preferred_element_type=F32))
    o_ref[0] = ((y + gg_ref[0] * skip_ref[...]) * x0_ref[0]).astype(o_ref.dtype)


def _hy_long_conv_gate(gg, x0, kre, kim, skip, tables):
    bsz, length, d = gg.shape
    cf, sf, sft = tables
    tf = _pick(length, (512, 256))
    y = pl.pallas_call(
        _hy_fwd_kernel,
        grid=(bsz, length // tf),
        in_specs=[
            pl.BlockSpec((tf, length), lambda b, j: (j, 0)),
            pl.BlockSpec((tf, length), lambda b, j: (j, 0)),
            pl.BlockSpec((1, length, d), lambda b, j: (b, 0, 0)),
            pl.BlockSpec((tf, d), lambda b, j: (j, 0)),
            pl.BlockSpec((tf, d), lambda b, j: (j, 0)),
        ],
        out_specs=pl.BlockSpec((1, 2, tf, d), lambda b, j: (b, 0, j, 0)),
        out_shape=jax.ShapeDtypeStruct((bsz, 2, length, d), BF16),
        scratch_shapes=[pltpu.VMEM((length, d), BF16)],
        compiler_params=_cparams("parallel", "arbitrary"),
        name="hy_dft_fwd",
    )(cf, sf, gg, kre, kim)
    tt = _pick(length, (512, 256))
    return pl.pallas_call(
        _hy_inv_kernel,
        grid=(bsz, length // tt),
        in_specs=[
            pl.BlockSpec((tt, length), lambda b, i: (i, 0)),
            pl.BlockSpec((tt, length), lambda b, i: (i, 0)),
            pl.BlockSpec((1, 2, length, d), lambda b, i: (b, 0, 0, 0)),
            pl.BlockSpec((1, tt, d), lambda b, i: (b, i, 0)),
            pl.BlockSpec((1, tt, d), lambda b, i: (b, i, 0)),
            pl.BlockSpec((1, d), lambda b, i: (0, 0)),
        ],
        out_specs=pl.BlockSpec((1, tt, d), lambda b, i: (b, i, 0)),
        out_shape=jax.ShapeDtypeStruct((bsz, length, d), BF16),
        compiler_params=_cparams("parallel", "arbitrary"),
        name="hy_dft_inv",
    )(cf, sft, y, gg, x0, skip.reshape(1, d))


def _hyena_filter(length, w1, b1, w2, b2, w3, b3, sin_freq):
    t = jnp.linspace(0.0, 1.0, length, dtype=F32)[:, None]
    bands = (HY_EMB_DIM - 1) // 2
    w = (2.0 * math.pi / length) * jnp.arange(length, dtype=F32)[:, None]
    f = jnp.linspace(1e-4, bands - 1, bands, dtype=F32)[None, :]
    z = jnp.concatenate([t, jnp.cos(f * w), -jnp.sin(f * w)], axis=-1)
    hp = lax.Precision.HIGHEST
    h = jnp.sin(sin_freq[0] * (jnp.dot(z, w1, precision=hp) + b1))
    h = jnp.sin(sin_freq[1] * (jnp.dot(h, w2, precision=hp) + b2))
    h = jnp.dot(h, w3, precision=hp) + b3
    max_decay = math.log(HY_DECAY_TARGET) / HY_FAST_DECAY
    min_decay = math.log(HY_DECAY_TARGET) / HY_SLOW_DECAY
    deltas = jnp.abs(jnp.linspace(min_decay, max_decay, h.shape[-1] // 2, dtype=F32))
    deltas = jnp.tile(deltas, 2)
    return h * (jnp.exp(-t * deltas) + HY_MOD_SHIFT)


def _hy_filter_spectrum(filt, d, tables):
    length = filt.shape[0]
    cf, sf, _ = tables
    row0 = (jnp.arange(length) == 0)[:, None]
    hf = filt[:, :d]
    hb0 = jnp.where(row0, 0.0, filt[:, d:])
    tab = jnp.concatenate([cf, sf], axis=0)[None]
    spec = _proj(tab, jnp.concatenate([hf, hb0], axis=1))[0]
    wgt = jnp.where(row0, 1.0, 2.0).astype(F32) / (2 * length)
    kre = (spec[:length, :d] + spec[:length, d:]) * wgt
    kim = (spec[length:, :d] + jnp.where(row0, 1.0, -1.0) * spec[length:, d:]) * wgt
    return kre, kim


def _hyena_mixer(x, ctx, g, sh, sc, csh, csc, w_in, conv_w, conv_b, f_w1, f_b1, f_w2, f_b2, f_w3, f_b3,
                 sin_freq, skip, w_o, gate, cgate):
    d = x.shape[-1]

    def one(u, shift, scale, res, gt):
        length = u.shape[1]
        z = _proj(u, w_in, g, shift, scale)
        x0, gg = _hy_conv(z, conv_w, conv_b, d)
        tables = _dft_tables(length)
        filt = _hyena_filter(length, f_w1, f_b1, f_w2, f_b2, f_w3, f_b3, sin_freq)
        kre, kim = _hy_filter_spectrum(filt, d, tables)
        y = _hy_long_conv_gate(gg, x0, kre, kim, skip, tables)
        return _proj_res(y, w_o, res, gt)

    return one(x, sh, sc, x, gate), (one(ctx, csh, csc, ctx, cgate) if cgate is not None else None)


def _ml_conv_kernel(z_ref, w_ref, b_ref, o_ref, *t_refs, out_scale):
    y = _conv3(z_ref[0], w_ref, b_ref)
    y = y * jax.nn.sigmoid(y) * out_scale
    o_ref[0] = y
    for t_ref in t_refs:
        t_ref[0] = y.T


def _ml_conv(z, conv_w, conv_b, col0, width, out_scale, transposed):
    bsz, length, _ = z.shape
    tc = 256
    j0 = col0 // tc
    out_specs = [pl.BlockSpec((1, length, tc), lambda b, j: (b, 0, j))]
    out_shape = [jax.ShapeDtypeStruct((bsz, length, width), F32)]
    if transposed:
        out_specs.append(pl.BlockSpec((1, tc, length), lambda b, j: (b, j, 0)))
        out_shape.append(jax.ShapeDtypeStruct((bsz, width, length), F32))
    return pl.pallas_call(
        functools.partial(_ml_conv_kernel, out_scale=out_scale),
        grid=(bsz, width // tc),
        in_specs=[pl.BlockSpec((1, length, tc), lambda b, j: (b, 0, j0 + j)),
                  pl.BlockSpec((3, tc), lambda b, j: (0, j0 + j)),
                  pl.BlockSpec((1, tc), lambda b, j: (0, j0 + j))],
        out_specs=out_specs,
        out_shape=out_shape,
        compiler_params=_cparams("parallel", "parallel"),
        name="ml_conv",
    )(z, conv_w, conv_b.reshape(1, -1))


def _ml_gates_kernel(x_ref, g_ref, sh_ref, sc_ref, w_ref, b_ref, o_ref, t_ref):
    x = x_ref[0]
    ms = jnp.mean(x * x, axis=-1, keepdims=True)
    h = x * lax.rsqrt(ms + NORM_EPS) * g_ref[...]
    h = h * (1.0 + sc_ref[0]) + sh_ref[0]
    gates = jnp.dot(h.astype(BF16), w_ref[...], preferred_element_type=F32) + b_ref[...]
    o_ref[0] = gates
    t_ref[0] = gates.T


def _ml_gates(x, g, shift, scale, w_gate, gate_b):
    bsz, length, d = x.shape
    pad = LANES - w_gate.shape[1]
    w = jnp.pad(w_gate, ((0, 0), (0, pad))).astype(BF16)
    b = jnp.pad(gate_b, (0, pad)).reshape(1, LANES)
    tm = _pick(length, (1024, 256))
    return pl.pallas_call(
        _ml_gates_kernel,
        grid=(bsz, length // tm),
        in_specs=[
            pl.BlockSpec((1, tm, d), lambda b_, i: (b_, i, 0)),
            pl.BlockSpec((1, d), lambda b_, i: (0, 0)),
            pl.BlockSpec((1, 1, d), lambda b_, i: (b_, 0, 0)),
            pl.BlockSpec((1, 1, d), lambda b_, i: (b_, 0, 0)),
            pl.BlockSpec((d, LANES), lambda b_, i: (0, 0)),
            pl.BlockSpec((1, LANES), lambda b_, i: (0, 0)),
        ],
        out_specs=[pl.BlockSpec((1, tm, LANES), lambda b_, i: (b_, i, 0)),
                   pl.BlockSpec((1, LANES, tm), lambda b_, i: (b_, 0, i))],
        out_shape=[jax.ShapeDtypeStruct((bsz, length, LANES), F32),
                   jax.ShapeDtypeStruct((bsz, LANES, length), F32)],
        compiler_params=_cparams("parallel", "parallel"),
        name="ml_gates",
    )(x, g.reshape(1, d), shift.reshape(bsz, 1, d), scale.reshape(bsz, 1, d), w, b)


def _log_sigmoid(x):
    return jnp.minimum(x, 0.0) - jnp.log1p(jnp.exp(-jnp.abs(x)))


def _ml_chunk_pair(q2, kt2, k2, v2, gates, i_rows, f_rows, f_lanes, state, reverse):
    ct2, n2, ms = state
    tlen = q2.shape[0]
    dk, dv = ML_QK_DIM, ML_V_DIM
    ti = lax.broadcasted_iota(jnp.int32, (tlen, tlen), 0)
    si = lax.broadcasted_iota(jnp.int32, (tlen, tlen), 1)
    causal = (si >= ti) if reverse else (si <= ti)
    causal_t = (ti >= si) if reverse else (ti <= si)
    lane = lax.broadcasted_iota(jnp.int32, (1, LANES), 1)
    sub = lax.broadcasted_iota(jnp.int32, (LANES, 1), 0)
    kt_b = kt2.astype(BF16)
    k_b = k2.astype(BF16)
    v_b = v2.astype(BF16)
    ct_b = ct2.astype(BF16)
    hs, ws, decays, m_news = [], [], [], []
    for j in range(2):
        own = (lane >= dk) if j else (lane < dk)
        lf_row = _log_sigmoid(f_rows[j])
        f_col = jnp.where(lane == f_lanes[j], gates, 0.0).sum(axis=-1, keepdims=True)
        lf_col = _log_sigmoid(f_col)
        b_col = jnp.where(causal, lf_row, 0.0).sum(axis=-1, keepdims=True)
        b_row = jnp.where(causal_t, lf_col, 0.0).sum(axis=0, keepdims=True)
        log_d = jnp.where(causal, b_col - b_row + i_rows[j], -jnp.inf)
        m_inter = b_col + ms[j]
        m_t = jnp.maximum(log_d.max(axis=-1, keepdims=True), m_inter)
        qj = jnp.where(own, q2, 0.0)
        qj_b = qj.astype(BF16)
        s = jnp.dot(qj_b, kt_b, preferred_element_type=F32) * jnp.exp(log_d - m_t)
        inter = jnp.exp(m_inter - m_t)
        sv = jnp.dot(s.astype(BF16), v_b, preferred_element_type=F32)[:, j * dv:(j + 1) * dv]
        num = sv + inter * jnp.dot(qj_b, ct_b, preferred_element_type=F32)
        folded = inter * (qj * n2)
        for c in range(0, tlen, LANES):
            folded = folded + s[:, c:c + LANES]
        qn = folded.sum(axis=-1, keepdims=True)
        hs.append(num / jnp.maximum(jnp.abs(qn), jnp.exp(-m_t)))
        b_end = b_row[:, 0:1] if reverse else b_row[:, tlen - 1:tlen]
        w_log = b_end - b_row + i_rows[j]
        m_new = jnp.maximum(b_end + ms[j], w_log.max(axis=-1, keepdims=True))
        ws.append(jnp.exp(w_log - m_new))
        decays.append(jnp.exp(b_end + ms[j] - m_new))
        m_news.append(m_new)
    head0_rows = sub < dk
    head0_lanes = lane < dk
    kw = (kt2 * jnp.where(head0_rows, ws[0], ws[1])).astype(BF16)
    upd = jnp.dot(kw, v_b, preferred_element_type=F32)
    ct_new = (jnp.where(head0_rows, decays[0], decays[1]) * ct2
              + jnp.where(head0_rows, upd[:, :dv], upd[:, dv:]))
    wk = [jnp.dot(w.astype(BF16), k_b, preferred_element_type=F32) for w in ws]
    n_new = jnp.where(head0_lanes, decays[0], decays[1]) * n2 + jnp.where(head0_lanes, wk[0], wk[1])
    return jnp.concatenate(hs, axis=-1), (ct_new, n_new, tuple(m_news))


def _mlstm_kernel(ql_ref, kl_ref, ktl_ref, vl_ref, ol_ref, gl_ref, gtl_ref,
                  qc_ref, kc_ref, ktc_ref, vc_ref, oc_ref, gc_ref, gtc_ref, ng_ref,
                  yl_ref, yc_ref, hfl_ref, hbl_ref, hfc_ref, hbc_ref):
    tlen = ML_CHUNK
    dv, nh = ML_V_DIM, ML_HEADS
    n_lat = ql_ref.shape[1] // tlen
    n_ctx = qc_ref.shape[1] // tlen
    head0 = 2 * pl.program_id(1)

    def run(refs, c, state, direction):
        q_ref, k_ref, kt_ref, v_ref, g_ref, gt_ref, hf_ref, hb_ref = refs
        rows = pl.ds(pl.multiple_of(c * tlen, tlen), tlen)
        gate_i, gate_f = 2 * direction * nh, (2 * direction + 1) * nh
        i_rows = [gt_ref[0, pl.ds(gate_i + head0 + j, 1), rows] for j in range(2)]
        f_rows = [gt_ref[0, pl.ds(gate_f + head0 + j, 1), rows] for j in range(2)]
        f_lanes = [gate_f + head0 + j for j in range(2)]
        h, state = _ml_chunk_pair(q_ref[0, rows, :], kt_ref[0, :, rows], k_ref[0, rows, :], v_ref[0, rows, :],
                                  g_ref[0, rows, :], i_rows, f_rows, f_lanes, state, direction == 1)
        (hb_ref if direction else hf_ref)[rows, :] = h
        return state

    def sweep(refs, n_chunks, states):
        def body(c, sts):
            return run(refs, c, sts[0], 0), run(refs, n_chunks - 1 - c, sts[1], 1)
        return lax.fori_loop(0, n_chunks, body, states)

    zero = (jnp.zeros((LANES, dv), F32), jnp.zeros((1, LANES), F32),
            (jnp.zeros((1, 1), F32), jnp.zeros((1, 1), F32)))
    lat = (ql_ref, kl_ref, ktl_ref, vl_ref, gl_ref, gtl_ref, hfl_ref, hbl_ref)
    ctx = (qc_ref, kc_ref, ktc_ref, vc_ref, gc_ref, gtc_ref, hfc_ref, hbc_ref)
    sweep(lat, n_lat, sweep(ctx, n_ctx, (zero, zero)))

    def finish(hf_ref, hb_ref, o_ref, y_ref):
        for j in range(2):
            vs = slice(j * dv, (j + 1) * dv)
            h = hf_ref[:, vs] + hb_ref[:, vs]
            h = h * lax.rsqrt(jnp.mean(h * h, axis=-1, keepdims=True) + NORM_EPS) * ng_ref[:, vs]
            y_ref[0, :, vs] = (h * jax.nn.sigmoid(o_ref[0, :, vs])).astype(y_ref.dtype)

    finish(hfl_ref, hbl_ref, ol_ref, yl_ref)
    finish(hfc_ref, hbc_ref, oc_ref, yc_ref)


def _mlstm_mixer(x, ctx, g, sh, sc, csh, csc, w_in, conv_w, conv_b, gate_b, out_norm_g, w_o, gate, cgate):
    nh, dk, dv = ML_HEADS, ML_QK_DIM, ML_V_DIM
    nqk = 2 * nh * dk
    wide = nqk + 2 * nh * dv
    w_main = w_in[:, :wide].astype(BF16)
    w_gate = w_in[:, wide:]

    def project(u, shift, scale):
        z = _proj(u, w_main, g, shift, scale)
        gates, gates_t = _ml_gates(u, g, shift, scale, w_gate, gate_b)
        (q,) = _ml_conv(z, conv_w, conv_b, 0, nh * dk, 1.0, False)
        k, kt = _ml_conv(z, conv_w, conv_b, nh * dk, nh * dk, dk ** -0.5, True)
        return z, q, k, kt, gates, gates_t

    zl, ql, kl, ktl, gl, gtl = project(x, sh, sc)
    zc, qc, kc, ktc, gc, gtc = project(ctx, csh, csc)
    bsz, length, d = x.shape
    lc = ctx.shape[1]
    npair = nh // 2
    qw, vw = 2 * dk, 2 * dv
    v_off, o_off = nqk // vw, (nqk + nh * dv) // vw

    def specs(n):
        return [
            pl.BlockSpec((1, n, qw), lambda b, h: (b, 0, h)),
            pl.BlockSpec((1, n, qw), lambda b, h: (b, 0, h)),
            pl.BlockSpec((1, qw, n), lambda b, h: (b, h, 0)),
            pl.BlockSpec((1, n, vw), lambda b, h: (b, 0, v_off + h)),
            pl.BlockSpec((1, n, vw), lambda b, h: (b, 0, o_off + h)),
            pl.BlockSpec((1, n, LANES), lambda b, h: (b, 0, 0)),
            pl.BlockSpec((1, LANES, n), lambda b, h: (b, 0, 0)),
        ]

    yl, yc = pl.pallas_call(
        _mlstm_kernel,
        grid=(bsz, npair),
        in_specs=specs(length) + specs(lc) + [pl.BlockSpec((1, vw), lambda b, h: (0, h))],
        out_specs=[pl.BlockSpec((1, length, vw), lambda b, h: (b, 0, h)),
                   pl.BlockSpec((1, lc, vw), lambda b, h: (b, 0, h))],
        out_shape=[jax.ShapeDtypeStruct((bsz, length, nh * dv), BF16),
                   jax.ShapeDtypeStruct((bsz, lc, nh * dv), BF16)],
        scratch_shapes=[pltpu.VMEM((length, vw), F32), pltpu.VMEM((length, vw), F32),
                        pltpu.VMEM((lc, vw), F32), pltpu.VMEM((lc, vw), F32)],
        compiler_params=_cparams("parallel", "parallel"),
        name="mlstm",
    )(ql, kl, ktl, zl, zl, gl, gtl, qc, kc, ktc, zc, zc, gc, gtc, out_norm_g.reshape(1, nh * dv))
    return _proj_res(yl, w_o, x, gate), (_proj_res(yc, w_o, ctx, cgate) if cgate is not None else None)


def _router_kernel(x_ref, g_ref, sh_ref, sc_ref, rw_ref, h_ref, aff_ref):
    x = x_ref[0]
    ms = jnp.mean(x * x, axis=-1, keepdims=True)
    h = x * lax.rsqrt(ms + NORM_EPS) * g_ref[...]
    h = h * (1.0 + sc_ref[0]) + sh_ref[0]
    h_ref[0] = h.astype(BF16)
    logits = lax.dot_general(rw_ref[...], h, (((1,), (1,)), ((), ())), precision=HIGHEST,
                             preferred_element_type=F32)
    mx = logits.max(axis=0, keepdims=True)
    p = jnp.exp(logits - mx)
    aff_ref[0] = p / p.sum(axis=0, keepdims=True)


def _router(x, g, shift, scale, router_w):
    bsz, length, d = x.shape
    ne = router_w.shape[1]
    tm = _pick(length, (1024, 256))
    return pl.pallas_call(
        _router_kernel,
        grid=(bsz, length // tm),
        in_specs=[
            pl.BlockSpec((1, tm, d), lambda b, i: (b, i, 0)),
            pl.BlockSpec((1, d), lambda b, i: (0, 0)),
            pl.BlockSpec((1, 1, d), lambda b, i: (b, 0, 0)),
            pl.BlockSpec((1, 1, d), lambda b, i: (b, 0, 0)),
            pl.BlockSpec((ne, d), lambda b, i: (0, 0)),
        ],
        out_specs=[pl.BlockSpec((1, tm, d), lambda b, i: (b, i, 0)),
                   pl.BlockSpec((1, ne, tm), lambda b, i: (b, 0, i))],
        out_shape=[jax.ShapeDtypeStruct((bsz, length, d), BF16),
                   jax.ShapeDtypeStruct((bsz, ne, length), F32)],
        compiler_params=_cparams("parallel", "parallel"),
        name="moe_router",
    )(x, g.reshape(1, d), shift.reshape(bsz, 1, d), scale.reshape(bsz, 1, d), router_w.T)


def _excl_cumsum_lanes(flags):
    rows, length = flags.shape
    cw = min(length, 256)
    si = lax.broadcasted_iota(jnp.int32, (cw, cw), 0)
    ti = lax.broadcasted_iota(jnp.int32, (cw, cw), 1)
    upper = jnp.where(si < ti, 1.0, 0.0).astype(BF16)
    carry = jnp.zeros((rows, 1), F32)
    parts = []
    for c in range(length // cw):
        blk = flags[:, c * cw:(c + 1) * cw]
        parts.append(jnp.dot(blk.astype(BF16), upper, preferred_element_type=F32) + carry)
        carry = carry + blk.sum(axis=-1, keepdims=True)
    return parts[0] if len(parts) == 1 else jnp.concatenate(parts, axis=-1)


def _select_kernel(aff_ref, pos_ref, *, cap):
    a = aff_ref[...]
    rows = a.shape[0]
    capf = float(cap)

    def bisect(_, lohi):
        lo, hi = lohi
        mid = 0.5 * (lo + hi)
        cnt = jnp.where(a >= mid, 1.0, 0.0).sum(axis=-1, keepdims=True)
        ge = cnt >= capf
        return jnp.where(ge, mid, lo), jnp.where(ge, hi, mid)

    lo, hi = lax.fori_loop(0, SELECT_BISECTIONS, bisect,
                           (jnp.zeros((rows, 1), F32), jnp.full((rows, 1), 2.0, F32)))
    above = a >= hi
    tie = jnp.logical_and(a >= lo, jnp.logical_not(above))
    n_above = jnp.where(above, 1.0, 0.0).sum(axis=-1, keepdims=True)
    tie_rank = _excl_cumsum_lanes(jnp.where(tie, 1.0, 0.0))
    sel = jnp.logical_or(above, jnp.logical_and(tie, tie_rank < capf - n_above))
    slot = _excl_cumsum_lanes(jnp.where(sel, 1.0, 0.0))
    pos_ref[...] = jnp.where(sel, slot, -1.0).astype(jnp.int32)


def _select(aff2d, cap):
    return pl.pallas_call(
        functools.partial(_select_kernel, cap=cap),
        out_shape=jax.ShapeDtypeStruct(aff2d.shape, jnp.int32),
        compiler_params=pltpu.CompilerParams(vmem_limit_bytes=V7X_VMEM_LIMIT_BYTES),
        name="moe_select",
    )(aff2d)


GATHER_EXPERTS_PER_STEP = 4


def _gather_kernel(h_ref, pos_ref, aff_ref, xg_ref, gs_ref, *, cap):
    length = pos_ref.shape[3]
    slot = lax.broadcasted_iota(jnp.int32, (cap, length), 0)
    hits = [slot == pos_ref[0, k] for k in range(GATHER_EXPERTS_PER_STEP)]
    onehot = jnp.concatenate([jnp.where(hit, 1.0, 0.0).astype(BF16) for hit in hits], axis=0)
    xg = jnp.dot(onehot, h_ref[0], preferred_element_type=F32).astype(BF16)
    for k, hit in enumerate(hits):
        xg_ref[k] = xg[k * cap:(k + 1) * cap]
        gs_ref[k] = jnp.where(hit, aff_ref[0, k], 0.0).sum(axis=-1, keepdims=True)


def _gather(h, pos, aff, cap):
    bsz, length, d = h.shape
    ne = pos.shape[1]
    per = GATHER_EXPERTS_PER_STEP
    return pl.pallas_call(
        functools.partial(_gather_kernel, cap=cap),
        grid=(bsz, ne // per),
        in_specs=[
            pl.BlockSpec((1, length, d), lambda b, e: (b, 0, 0)),
            pl.BlockSpec((1, per, 1, length), lambda b, e: (b, e, 0, 0)),
            pl.BlockSpec((1, per, 1, length), lambda b, e: (b, e, 0, 0)),
        ],
        out_specs=[pl.BlockSpec((per, cap, d), lambda b, e: (e, b, 0)),
                   pl.BlockSpec((per, cap, 1), lambda b, e: (e, b, 0))],
        out_shape=[jax.ShapeDtypeStruct((ne, bsz * cap, d), BF16),
                   jax.ShapeDtypeStruct((ne, bsz * cap, 1), F32)],
        compiler_params=_cparams("parallel", "arbitrary"),
        name="moe_gather",
    )(h, pos.reshape(bsz, ne, 1, length), aff.reshape(bsz, ne, 1, length))


def _ffn_kernel(*refs, n_groups, n_up, tf):
    xg_refs = refs[:n_groups]
    gs_refs = refs[n_groups:2 * n_groups]
    wg_ref, wu_ref, wd_ref = refs[2 * n_groups:2 * n_groups + 3]
    y_refs = refs[2 * n_groups + 3:3 * n_groups + 3]
    z_refs = refs[3 * n_groups + 3:]
    step = pl.program_id(1)

    @pl.when(step < n_up)
    def _():
        wg = wg_ref[0, 0].astype(BF16)
        wu = wu_ref[0, 0].astype(BF16)
        cols = pl.ds(pl.multiple_of(step * tf, tf), tf)
        for xg_ref, z_ref in zip(xg_refs, z_refs):
            xg = xg_ref[0]
            a = jnp.dot(xg, wg, preferred_element_type=F32)
            u = jnp.dot(xg, wu, preferred_element_type=F32)
            z_ref[:, cols] = (a * jax.nn.sigmoid(a) * u).astype(BF16)

    @pl.when(step >= n_up)
    def _():
        wd = wd_ref[0, 0].astype(BF16)
        for gs_ref, y_ref, z_ref in zip(gs_refs, y_refs, z_refs):
            y = jnp.dot(z_ref[...], wd, preferred_element_type=F32)
            y_ref[0] = (y * gs_ref[0]).astype(BF16)


def _expert_ffn(xgs, gss, w_gate, w_up, w_down, layer):
    ne, _, d = xgs[0].shape
    ff = w_gate.shape[3]
    tf = 256
    tn = 256
    n_up, n_down = ff // tf, d // tn
    n = len(xgs)
    tok = lambda a: pl.BlockSpec((1,) + a.shape[1:], lambda e, s: (e, 0, 0))
    up_tile = lambda e, s: (layer, e, 0, jnp.minimum(s, n_up - 1))
    down_tile = lambda e, s: (layer, e, 0, jnp.maximum(s - n_up, 0))
    return pl.pallas_call(
        functools.partial(_ffn_kernel, n_groups=n, n_up=n_up, tf=tf),
        grid=(ne, n_up + n_down),
        in_specs=[tok(a) for a in xgs] + [tok(a) for a in gss] + [
            pl.BlockSpec((1, 1, d, tf), up_tile),
            pl.BlockSpec((1, 1, d, tf), up_tile),
            pl.BlockSpec((1, 1, ff, tn), down_tile),
        ],
        out_specs=[pl.BlockSpec((1, a.shape[1], tn), lambda e, s: (e, 0, jnp.maximum(s - n_up, 0)))
                   for a in xgs],
        out_shape=[jax.ShapeDtypeStruct(a.shape, BF16) for a in xgs],
        scratch_shapes=[pltpu.VMEM((a.shape[1], ff), BF16) for a in xgs],
        compiler_params=_cparams("parallel", "arbitrary"),
        name="moe_ffn",
    )(*xgs, *gss, w_gate, w_up, w_down)


def _combine_kernel(post_ref, y_ref, x_ref, gt_ref, fg_ref, o_ref, *, cap, final_norm):
    post = post_ref[0]
    tl, ne = post.shape
    d = y_ref.shape[2]
    slot = lax.broadcasted_iota(jnp.int32, (tl, cap), 1)
    hits = [jnp.where(post[:, e:e + 1] == slot, 1.0, 0.0).astype(BF16) for e in range(ne)]
    if cap % LANES == 0:
        acc = jnp.dot(jnp.concatenate(hits, axis=1), y_ref[...].reshape(ne * cap, d),
                      preferred_element_type=F32)
    else:
        acc = jnp.zeros((tl, d), F32)
        for e in range(ne):
            acc = acc + jnp.dot(hits[e], y_ref[e], preferred_element_type=F32)
    out = x_ref[0] + gt_ref[0] * acc
    if final_norm:
        ms = jnp.mean(out * out, axis=-1, keepdims=True)
        out = out * lax.rsqrt(ms + NORM_EPS) * fg_ref[...]
    o_ref[0] = out


def _combine(pos_t, y, x, gate, cap, final_g=None):
    bsz, length, d = x.shape
    ne = y.shape[0]
    tl = _pick(length, (512, 256))
    final_norm = final_g is not None
    fg = (final_g if final_norm else jnp.ones((d,), F32)).reshape(1, d)
    return pl.pallas_call(
        functools.partial(_combine_kernel, cap=cap, final_norm=final_norm),
        grid=(bsz, length // tl),
        in_specs=[
            pl.BlockSpec((1, tl, ne), lambda b, i: (b, i, 0)),
            pl.BlockSpec((ne, cap, d), lambda b, i: (0, b, 0)),
            pl.BlockSpec((1, tl, d), lambda b, i: (b, i, 0)),
            pl.BlockSpec((1, 1, d), lambda b, i: (b, 0, 0)),
            pl.BlockSpec((1, d), lambda b, i: (0, 0)),
        ],
        out_specs=pl.BlockSpec((1, tl, d), lambda b, i: (b, i, 0)),
        out_shape=jax.ShapeDtypeStruct(x.shape, F32),
        compiler_params=_cparams("parallel", "arbitrary"),
        name="moe_combine",
    )(pos_t, y, x, gate.reshape(bsz, 1, d), fg)


def _route(x, g, shift, scale, router_w):
    bsz, length, _ = x.shape
    ne = router_w.shape[1]
    cap = max(1, EC_CAPACITY_FACTOR * length // ne)
    h, aff = _router(x, g, shift, scale, router_w)
    pos = _select(aff.reshape(bsz * ne, length), cap).reshape(bsz, ne, length)
    xg, gs = _gather(h, pos, aff, cap)
    return xg, gs, pos.transpose(0, 2, 1), cap


def _moe(x, ctx, g, sh, sc, gate, csh, csc, cgate, router_w, w_gate, w_up, w_down, layer, final_g=None):
    xg, gs, pos_t, cap = _route(x, g, sh, sc, router_w)
    if ctx is None:
        (y,) = _expert_ffn([xg], [gs], w_gate, w_up, w_down, layer)
        return _combine(pos_t, y, x, gate, cap, final_g), None
    xg_c, gs_c, pos_tc, cap_c = _route(ctx, g, csh, csc, router_w)
    y, y_c = _expert_ffn([xg, xg_c], [gs, gs_c], w_gate, w_up, w_down, layer)
    return _combine(pos_t, y, x, gate, cap), _combine(pos_tc, y_c, ctx, cgate, cap_c)


def kernel(x, c, ctx, c_ctx, mod_w, mod_b, norm_mix_g, norm_ffn_g, router_w, moe_w_gate, moe_w_up, moe_w_down, na_w_qkv, na_rpb, na_w_o, mla_w_in, mla_q_norm_g, mla_w_q_b, mla_kv_norm_g, mla_w_kv_b, mla_w_o, hy_w_in, hy_conv_w, hy_conv_b, hy_f_w1, hy_f_b1, hy_f_w2, hy_f_b2, hy_f_w3, hy_f_b3, hy_sin_freq, hy_skip, hy_w_o, ml_w_in, ml_conv_w, ml_conv_b, ml_gate_b, ml_out_norm_g, ml_w_o, final_norm_g):
    bsz, _, d = x.shape
    depth = mod_w.shape[0]
    n_mixers = 4
    cond = jnp.concatenate([c, c_ctx[None, :]], axis=0)
    cond = jnp.pad(cond, ((0, (-cond.shape[0]) % 8), (0, 0)))
    mods = _modulation(cond, mod_w, mod_b)
    for i in range(depth):
        last = i == depth - 1
        mod = mods[i]
        sh1, sc1, g1, sh2, sc2, g2 = [mod[:bsz, k * d:(k + 1) * d] for k in range(6)]
        bc = lambda v: jnp.broadcast_to(v[None, :], (bsz, d))
        csh1, csc1, cg1, csh2, csc2, cg2 = [bc(mod[bsz, k * d:(k + 1) * d]) for k in range(6)]
        if last:
            cg1 = None
        kind, j = i % n_mixers, i // n_mixers
        gm = norm_mix_g[i]
        if kind == 0:
            x, ctx_new = _na_mixer(x, ctx, gm, sh1, sc1, csh1, csc1, na_w_qkv[j], na_rpb[j], na_w_o[j], g1, cg1)
        elif kind == 1:
            x, ctx_new = _mla_mixer(x, ctx, gm, sh1, sc1, csh1, csc1, mla_w_in[j], mla_q_norm_g[j],
                                    mla_w_q_b[j], mla_kv_norm_g[j], mla_w_kv_b[j], mla_w_o[j], g1, cg1)
        elif kind == 2:
            x, ctx_new = _hyena_mixer(x, ctx, gm, sh1, sc1, csh1, csc1, hy_w_in[j], hy_conv_w[j], hy_conv_b[j],
                                      hy_f_w1[j], hy_f_b1[j], hy_f_w2[j], hy_f_b2[j], hy_f_w3[j], hy_f_b3[j],
                                      hy_sin_freq[j], hy_skip[j], hy_w_o[j], g1, cg1)
        else:
            x, ctx_new = _mlstm_mixer(x, ctx, gm, sh1, sc1, csh1, csc1, ml_w_in[j], ml_conv_w[j], ml_conv_b[j],
                                      ml_gate_b[j], ml_out_norm_g[j], ml_w_o[j], g1, cg1)
        if last:
            x, _ = _moe(x, None, norm_ffn_g[i], sh2, sc2, g2, None, None, None,
                        router_w[i], moe_w_gate, moe_w_up, moe_w_down, i, final_norm_g)
        else:
            x, ctx = _moe(x, ctx_new, norm_ffn_g[i], sh2, sc2, g2, csh2, csc2, cg2,
                          router_w[i], moe_w_gate, moe_w_up, moe_w_down, i)
    return x
```

```python
import functools
import math

import numpy as np
import jax
import jax.numpy as jnp
from jax import lax
from jax.experimental import pallas as pl
from jax.experimental.pallas import tpu as pltpu

F32 = jnp.float32
BF16 = jnp.bfloat16
HIGHEST = lax.Precision.HIGHEST

V7X_VMEM_LIMIT_BYTES = 56 * 1024 * 1024
LANES = 128

GRID_W = 64
NORM_EPS = 1e-6
ROPE_BASE = 10000.0

NA_HEADS = 16
NA_HEAD_DIM = 64
NA_WIN_H = 8
NA_WIN_W = 16
NA_QROWS = 4
NA_KROWS = NA_QROWS - 1 + NA_WIN_H

MLA_HEADS = 16
MLA_Q_RANK = 384
MLA_KV_RANK = 256
MLA_NOPE_DIM = 64
MLA_ROPE_DIM = 32
MLA_V_DIM = 64

HY_EMB_DIM = 33
HY_DECAY_TARGET = 1e-2
HY_FAST_DECAY = 0.3
HY_SLOW_DECAY = 1.5
HY_MOD_SHIFT = 0.05

ML_HEADS = 8
ML_V_DIM = 128
ML_QK_DIM = 64
ML_CHUNK = 256

N_EXPERTS = 16
EC_CAPACITY_FACTOR = 2
SELECT_BISECTIONS = 64


def _cparams(*sem):
    return pltpu.CompilerParams(dimension_semantics=sem, vmem_limit_bytes=V7X_VMEM_LIMIT_BYTES)


def _pick(n, prefs):
    for p in prefs:
        if n % p == 0:
            return p
    return n


def _proj_kernel(x_ref, g_ref, sh_ref, sc_ref, w_ref, o_ref, h_ref, *, norm):
    @pl.when(pl.program_id(2) == 0)
    def _():
        x = x_ref[0].astype(F32)
        if norm:
            ms = jnp.mean(x * x, axis=-1, keepdims=True)
            x = x * lax.rsqrt(ms + NORM_EPS) * g_ref[...]
            x = x * (1.0 + sc_ref[0]) + sh_ref[0]
        h_ref[...] = x.astype(BF16)

    o_ref[0] = jnp.dot(h_ref[...], w_ref[...].astype(BF16),
                       preferred_element_type=F32).astype(o_ref.dtype)


def _proj(x, w, g=None, shift=None, scale=None, out_dtype=F32):
    bsz, length, kdim = x.shape
    n = w.shape[1]
    norm = g is not None
    if not norm:
        g = jnp.ones((kdim,), F32)
    if shift is None:
        shift = jnp.zeros((bsz, kdim), F32)
        scale = jnp.zeros((bsz, kdim), F32)
    tm = _pick(length, (1024, 512, 256))
    tn = _pick(n, (1024, 768, 512, 384, 256, 128))
    return pl.pallas_call(
        functools.partial(_proj_kernel, norm=norm),
        grid=(bsz, length // tm, n // tn),
        in_specs=[
            pl.BlockSpec((1, tm, kdim), lambda b, i, j: (b, i, 0)),
            pl.BlockSpec((1, kdim), lambda b, i, j: (0, 0)),
            pl.BlockSpec((1, 1, kdim), lambda b, i, j: (b, 0, 0)),
            pl.BlockSpec((1, 1, kdim), lambda b, i, j: (b, 0, 0)),
            pl.BlockSpec((kdim, tn), lambda b, i, j: (0, j)),
        ],
        out_specs=pl.BlockSpec((1, tm, tn), lambda b, i, j: (b, i, j)),
        out_shape=jax.ShapeDtypeStruct((bsz, length, n), out_dtype),
        scratch_shapes=[pltpu.VMEM((tm, kdim), BF16)],
        compiler_params=_cparams("parallel", "parallel", "arbitrary"),
        name="proj",
    )(x, g.reshape(1, kdim).astype(F32), shift.reshape(bsz, 1, kdim), scale.reshape(bsz, 1, kdim),
      w.astype(BF16))


def _proj_res_kernel(y_ref, w_ref, r_ref, gt_ref, o_ref):
    acc = jnp.dot(y_ref[0].astype(BF16), w_ref[...].astype(BF16), preferred_element_type=F32)
    o_ref[0] = r_ref[0] + gt_ref[0] * acc


def _proj_res(y, w, res, gate):
    bsz, length, kdim = y.shape
    n = w.shape[1]
    tm = _pick(length, (1024, 512, 256))
    tn = _pick(n, (1024, 512, 256, 128))
    return pl.pallas_call(
        _proj_res_kernel,
        grid=(bsz, length // tm, n // tn),
        in_specs=[
            pl.BlockSpec((1, tm, kdim), lambda b, i, j: (b, i, 0)),
            pl.BlockSpec((kdim, tn), lambda b, i, j: (0, j)),
            pl.BlockSpec((1, tm, tn), lambda b, i, j: (b, i, j)),
            pl.BlockSpec((1, 1, tn), lambda b, i, j: (b, 0, j)),
        ],
        out_specs=pl.BlockSpec((1, tm, tn), lambda b, i, j: (b, i, j)),
        out_shape=jax.ShapeDtypeStruct((bsz, length, n), F32),
        compiler_params=_cparams("parallel", "parallel", "arbitrary"),
        name="proj_res",
    )(y, w.astype(BF16), res, gate.reshape(bsz, 1, n))


def _mod_kernel(c_ref, w_ref, b_ref, o_ref):
    c = c_ref[...]
    s = c * jax.nn.sigmoid(c)
    o_ref[0] = jnp.dot(s.astype(BF16), w_ref[0].astype(BF16), preferred_element_type=F32) + b_ref[0]


def _modulation(cc, w, b):
    rows, d = cc.shape
    depth, _, n = w.shape
    tn = 512
    return pl.pallas_call(
        _mod_kernel,
        grid=(depth, n // tn),
        in_specs=[
            pl.BlockSpec((rows, d), lambda l, j: (0, 0)),
            pl.BlockSpec((1, d, tn), lambda l, j: (l, 0, j)),
            pl.BlockSpec((1, 1, tn), lambda l, j: (l, 0, j)),
        ],
        out_specs=pl.BlockSpec((1, rows, tn), lambda l, j: (l, 0, j)),
        out_shape=jax.ShapeDtypeStruct((depth, rows, n), F32),
        compiler_params=_cparams("parallel", "parallel"),
        name="modulation",
    )(cc, w, b.reshape(depth, 1, n))


ATTN_KEY_CHUNK = 512
LOG2_E = math.log2(math.e)


def _softmax_pv(s_parts, v_parts):
    m = den = acc = None
    for s, v in zip(s_parts, v_parts):
        m_part = s.max(axis=-1, keepdims=True)
        m_new = m_part if m is None else jnp.maximum(m, m_part)
        p = jnp.exp2(s - m_new)
        l = p.sum(axis=-1, keepdims=True)
        o = jnp.dot(p.astype(BF16), v, preferred_element_type=F32)
        if m is None:
            den, acc = l, o
        else:
            alpha = jnp.exp2(m - m_new)
            den, acc = alpha * den + l, alpha * acc + o
        m = m_new
    return acc / den


def _key_chunks(n):
    step = ATTN_KEY_CHUNK if n % ATTN_KEY_CHUNK == 0 else n
    return [slice(c, c + step) for c in range(0, n, step)]


HEAD_V = 64


def _pair_queries(q2, hw):
    if hw == LANES:
        return [q2[:, :LANES], q2[:, LANES:]]
    lane = lax.broadcasted_iota(jnp.int32, (1, LANES), 1)
    return [jnp.where(lane < hw, q2, jnp.zeros_like(q2)), jnp.where(lane >= hw, q2, jnp.zeros_like(q2))]


def _pair_keys(k_ref, hw, j, rows=None):
    rows = slice(None) if rows is None else rows
    return k_ref[0, rows, j * LANES:(j + 1) * LANES] if hw == LANES else k_ref[0, rows, :]


def _pair_select(o0, o1):
    lane = lax.broadcasted_iota(jnp.int32, (1, LANES), 1)
    return jnp.where(lane < HEAD_V, o0, o1)


def _pair_attn_kernel(*refs, nparts, hw):
    q_ref, o_ref = refs[0], refs[-1]
    k_refs = refs[1:1 + nparts]
    v_refs = refs[1 + nparts:1 + 2 * nparts]
    nt = (((1,), (1,)), ((), ()))
    outs = []
    for j, q in enumerate(_pair_queries(q_ref[0], hw)):
        s_parts, v_parts = [], []
        for k_ref, v_ref in zip(k_refs, v_refs):
            for rows in _key_chunks(k_ref.shape[1]):
                s_parts.append(lax.dot_general(q, _pair_keys(k_ref, hw, j, rows), nt,
                                               preferred_element_type=F32))
                v_parts.append(v_ref[0, rows, :])
        outs.append(_softmax_pv(s_parts, v_parts))
    o_ref[0] = _pair_select(*outs).astype(o_ref.dtype)


def _pair_attention(q_src, kv_srcs, hw, nh):
    q, q_off = q_src
    bsz, lq = q.shape[0], q.shape[1]
    tq = _pick(lq, (1024, 512, 256))
    qw = 2 * hw
    in_specs = [pl.BlockSpec((1, tq, qw), lambda b, h, i: (b, i, q_off // qw + h))]
    args = [q]
    for k_arr, k_off, _, _ in kv_srcs:
        in_specs.append(pl.BlockSpec((1, k_arr.shape[1], qw), lambda b, h, i, o=k_off // qw: (b, 0, o + h)))
        args.append(k_arr)
    for _, _, v_arr, v_off in kv_srcs:
        in_specs.append(pl.BlockSpec((1, v_arr.shape[1], LANES), lambda b, h, i, o=v_off // LANES: (b, 0, o + h)))
        args.append(v_arr)
    return pl.pallas_call(
        functools.partial(_pair_attn_kernel, nparts=len(kv_srcs), hw=hw),
        grid=(bsz, nh // 2, lq // tq),
        in_specs=in_specs,
        out_specs=pl.BlockSpec((1, tq, LANES), lambda b, h, i: (b, i, h)),
        out_shape=jax.ShapeDtypeStruct((bsz, lq, nh * HEAD_V), BF16),
        compiler_params=_cparams("parallel", "parallel", "arbitrary"),
        name="attention",
    )(*args)


def _na_bias_table(rpb, rows):
    nh, n_dr, n_dc = rpb.shape
    kh = min(NA_WIN_H, rows)
    nblk = rows // NA_QROWS
    w = GRID_W
    qc = np.arange(w)[:, None]
    kc = np.arange(w)[None, :]
    cs = np.clip(qc - NA_WIN_W // 2, 0, w - NA_WIN_W)
    col_ok = (kc >= cs) & (kc < cs + NA_WIN_W)
    dc = np.clip(kc - qc, -(NA_WIN_W - 1), NA_WIN_W - 1) + NA_WIN_W - 1
    pick_dc = (dc.reshape(1, w * w) == np.arange(n_dc)[:, None]).astype(np.float32)
    tile = jnp.dot(rpb.reshape(nh * n_dr, n_dc).astype(F32), jnp.asarray(pick_dc), precision=HIGHEST)
    tile = jnp.where(col_ok, tile.reshape(nh, n_dr, w, w), -1e30)
    masked = jnp.full((nh, w, w), -1e30, F32)
    tables = []
    for blk in (0, 1, nblk - 1):
        start = int(np.clip(blk * NA_QROWS - NA_WIN_H // 2, 0, rows - NA_KROWS))
        q_rows = []
        for qr in range(NA_QROWS):
            r = blk * NA_QROWS + qr
            rs = int(np.clip(r - kh // 2, 0, rows - kh))
            tiles = [tile[:, kr - r + NA_WIN_H - 1] if rs <= kr < rs + kh else masked
                     for kr in range(start, start + NA_KROWS)]
            q_rows.append(jnp.concatenate(tiles, axis=-1))
        tables.append(jnp.concatenate(q_rows, axis=1))
    return jnp.stack(tables, axis=1)


NA_BLOCKS_PER_STEP = 8


def _na_kernel(q_ref, kl_ref, vl_ref, kc_ref, vc_ref, bias_ref, o_ref, *, nblk, rows):
    nt = (((1,), (1,)), ((), ()))
    hw = NA_HEAD_DIM
    tq = NA_QROWS * GRID_W
    for sub in range(NA_BLOCKS_PER_STEP):
        i = pl.program_id(2) * NA_BLOCKS_PER_STEP + sub
        start = jnp.clip(i * NA_QROWS - NA_WIN_H // 2, 0, rows - NA_KROWS) * GRID_W
        start = pl.multiple_of(start, GRID_W)
        pat = jnp.where(i == 0, 0, jnp.where(i == nblk - 1, 2, 1))
        local = pl.ds(start, NA_KROWS * GRID_W)
        qrows = slice(sub * tq, (sub + 1) * tq)
        outs = []
        for j, q in enumerate(_pair_queries(q_ref[0, qrows, :], hw)):
            s_ctx = lax.dot_general(q, _pair_keys(kc_ref, hw, j), nt, preferred_element_type=F32)
            s_loc = (lax.dot_general(q, _pair_keys(kl_ref, hw, j, local), nt, preferred_element_type=F32)
                     + bias_ref[j, pat])
            outs.append(_softmax_pv([s_ctx, s_loc], [vc_ref[0], vl_ref[0, local, :]]))
        o_ref[0, qrows, :] = _pair_select(*outs).astype(o_ref.dtype)


def _na_mixer(x, ctx, g, sh, sc, csh, csc, w_qkv, rpb, w_o, gate, cgate):
    bsz, length, d = x.shape
    lc = ctx.shape[1]
    rows = length // GRID_W
    nblk = rows // NA_QROWS
    nh, hd = NA_HEADS, NA_HEAD_DIM
    w = jnp.concatenate([w_qkv[:, :d] * (hd ** -0.5 * LOG2_E), w_qkv[:, d:]], axis=1).astype(BF16)
    qkv = _proj(x, w, g, sh, sc, out_dtype=BF16)
    qkv_c = _proj(ctx, w, g, csh, csc, out_dtype=BF16)
    bias = _na_bias_table(rpb * LOG2_E, rows)
    tq = NA_QROWS * GRID_W
    tstep = NA_BLOCKS_PER_STEP * tq
    npair = nh // 2
    pw = 2 * hd
    y = pl.pallas_call(
        functools.partial(_na_kernel, nblk=nblk, rows=rows),
        grid=(npair, bsz, nblk // NA_BLOCKS_PER_STEP),
        in_specs=[
            pl.BlockSpec((1, tstep, pw), lambda h, b, i: (b, i, h)),
            pl.BlockSpec((1, length, pw), lambda h, b, i: (b, 0, npair + h)),
            pl.BlockSpec((1, length, pw), lambda h, b, i: (b, 0, 2 * npair + h)),
            pl.BlockSpec((1, lc, pw), lambda h, b, i: (b, 0, npair + h)),
            pl.BlockSpec((1, lc, pw), lambda h, b, i: (b, 0, 2 * npair + h)),
            pl.BlockSpec((2, 3, tq, NA_KROWS * GRID_W), lambda h, b, i: (h, 0, 0, 0)),
        ],
        out_specs=pl.BlockSpec((1, tstep, pw), lambda h, b, i: (b, i, h)),
        out_shape=jax.ShapeDtypeStruct((bsz, length, d), BF16),
        compiler_params=_cparams("parallel", "parallel", "arbitrary"),
        name="na_local",
    )(qkv, qkv, qkv, qkv_c, qkv_c, bias)
    if cgate is None:
        return _proj_res(y, w_o, x, gate), None
    yc = _pair_attention((qkv_c, 0), [(qkv_c, d, qkv_c, 2 * d)], hd, nh)
    return _proj_res(y, w_o, x, gate), _proj_res(yc, w_o, ctx, cgate)


def _axial_rope(length):
    t = jnp.arange(length)
    row = (t // GRID_W).astype(F32)
    col = (t % GRID_W).astype(F32)
    n_freq = MLA_ROPE_DIM // 4
    inv = ROPE_BASE ** (-jnp.arange(n_freq, dtype=F32) / n_freq)
    ang = jnp.concatenate([row[:, None] * inv, col[:, None] * inv], axis=-1)
    return jnp.cos(ang), jnp.sin(ang)


def _rot_half_cols(w):
    half = w.shape[-1] // 2
    return jnp.concatenate([-w[..., half:], w[..., :half]], axis=-1)


def _mla_weights(w_in, w_q_b, w_kv_b):
    nh, nope, rope = MLA_HEADS, MLA_NOPE_DIM, MLA_ROPE_DIM
    off = MLA_Q_RANK + MLA_KV_RANK
    w_in_ext = jnp.concatenate([w_in, _rot_half_cols(w_in[:, off:])], axis=1)
    rq, rkv = w_q_b.shape[0], w_kv_b.shape[0]
    wq = w_q_b.reshape(rq, nh, nope + rope)
    pad = jnp.zeros((rq, nh, LANES - nope - rope), F32)
    wq_pad = jnp.concatenate([wq, pad], axis=-1).reshape(rq, nh * LANES)
    wq_rot = jnp.concatenate([jnp.zeros((rq, nh, nope), F32), _rot_half_cols(wq[..., nope:]), pad],
                             axis=-1).reshape(rq, nh * LANES)
    wkv = w_kv_b.reshape(rkv, nh, nope + MLA_V_DIM)
    wk_pad = jnp.concatenate([wkv[..., :nope], jnp.zeros((rkv, nh, LANES - nope), F32)],
                             axis=-1).reshape(rkv, nh * LANES)
    wv = wkv[..., nope:].reshape(rkv, nh * MLA_V_DIM)
    place = np.zeros((rope, nh, LANES), np.float32)
    place[np.arange(rope), :, nope + np.arange(rope)] = 1.0
    place = jnp.asarray(place.reshape(rope, nh * LANES))
    return (w_in_ext,) + tuple(a.astype(BF16) for a in (wq_pad, wq_rot, wk_pad, wv, place))


def _mla_rope_tables(length, rope_on):
    if rope_on:
        cos, sin = _axial_rope(length)
    else:
        cos = jnp.ones((length, MLA_ROPE_DIM // 2), F32)
        sin = jnp.zeros((length, MLA_ROPE_DIM // 2), F32)
    ck = jnp.concatenate([cos, cos], axis=-1)
    sk = jnp.concatenate([sin, sin], axis=-1)
    one = jnp.ones((length, MLA_NOPE_DIM), F32)
    tail = LANES - MLA_NOPE_DIM - MLA_ROPE_DIM
    cq = jnp.concatenate([one, ck, one[:, :tail]], axis=-1)
    sq = jnp.concatenate([0.0 * one, sk, 0.0 * one[:, :tail]], axis=-1)
    return cq, sq, ck, sk


def _mla_qkv_kernel(z_ref, gq_ref, gkv_ref, wq_ref, wqr_ref, wk_ref, wv_ref, place_ref,
                    cq_ref, sq_ref, ck_ref, sk_ref, q_ref, k_ref, v_ref):
    z = z_ref[0]
    off = MLA_Q_RANK + MLA_KV_RANK
    rope = MLA_ROPE_DIM

    def norm(a, g_ref):
        ms = jnp.mean(a * a, axis=-1, keepdims=True)
        return (a * lax.rsqrt(ms + NORM_EPS) * g_ref[...]).astype(BF16)

    cq = norm(z[:, :MLA_Q_RANK], gq_ref)
    ckv = norm(z[:, MLA_Q_RANK:off], gkv_ref)
    k_rope = (z[:, off:off + rope] * ck_ref[...] + z[:, off + rope:off + 2 * rope] * sk_ref[...]).astype(BF16)
    qa = jnp.dot(cq, wq_ref[...], preferred_element_type=F32)
    qb = jnp.dot(cq, wqr_ref[...], preferred_element_type=F32)
    cos_q = cq_ref[...]
    sin_q = sq_ref[...]
    for h in range(MLA_HEADS):
        cols = slice(h * LANES, (h + 1) * LANES)
        q_ref[0, :, cols] = (qa[:, cols] * cos_q + qb[:, cols] * sin_q).astype(BF16)
    k_ref[0] = (jnp.dot(ckv, wk_ref[...], preferred_element_type=F32)
                + jnp.dot(k_rope, place_ref[...], preferred_element_type=F32)).astype(BF16)
    v_ref[0] = jnp.dot(ckv, wv_ref[...], preferred_element_type=F32).astype(BF16)


def _mla_qkv(z, q_norm_g, kv_norm_g, weights, tables):
    bsz, length, zw = z.shape
    wq, wqr, wk, wv, place = weights
    nh = MLA_HEADS
    tm = _pick(length, (512, 256))
    full = lambda a: pl.BlockSpec(a.shape, lambda b, i: (0, 0))
    rows = lambda a: pl.BlockSpec((tm, a.shape[1]), lambda b, i: (i, 0))
    gq = q_norm_g.reshape(1, -1)
    gkv = kv_norm_g.reshape(1, -1)
    out = lambda w: pl.BlockSpec((1, tm, w), lambda b, i: (b, i, 0))
    return pl.pallas_call(
        _mla_qkv_kernel,
        grid=(bsz, length // tm),
        in_specs=[pl.BlockSpec((1, tm, zw), lambda b, i: (b, i, 0)), full(gq), full(gkv),
                  full(wq), full(wqr), full(wk), full(wv), full(place)] + [rows(t) for t in tables],
        out_specs=[out(nh * LANES), out(nh * LANES), out(nh * MLA_V_DIM)],
        out_shape=[jax.ShapeDtypeStruct((bsz, length, nh * LANES), BF16),
                   jax.ShapeDtypeStruct((bsz, length, nh * LANES), BF16),
                   jax.ShapeDtypeStruct((bsz, length, nh * MLA_V_DIM), BF16)],
        compiler_params=_cparams("parallel", "parallel"),
        name="mla_qkv",
    )(z, gq, gkv, wq, wqr, wk, wv, place, *tables)


def _mla_mixer(x, ctx, g, sh, sc, csh, csc, w_in, q_norm_g, w_q_b, kv_norm_g, w_kv_b, w_o, gate, cgate):
    nh = MLA_HEADS
    scale = (MLA_NOPE_DIM + MLA_ROPE_DIM) ** -0.5
    w_in_ext, *weights = _mla_weights(w_in, w_q_b, w_kv_b)

    def project(u, shift, scl, rope_on):
        z = _proj(u, w_in_ext, g, shift, scl)
        cos_q, sin_q, cos_k, sin_k = _mla_rope_tables(u.shape[1], rope_on)
        qs = scale * LOG2_E
        return _mla_qkv(z, q_norm_g, kv_norm_g, weights, (cos_q * qs, sin_q * qs, cos_k, sin_k))

    ql, kl, vl = project(x, sh, sc, True)
    qc, kc, vc = project(ctx, csh, csc, False)
    y = _pair_attention((ql, 0), [(kc, 0, vc, 0), (kl, 0, vl, 0)], LANES, nh)
    if cgate is None:
        return _proj_res(y, w_o, x, gate), None
    yc = _pair_attention((qc, 0), [(kc, 0, vc, 0)], LANES, nh)
    return _proj_res(y, w_o, x, gate), _proj_res(yc, w_o, ctx, cgate)


def _conv3(z, w_ref, b_ref):
    length = z.shape[0]
    t = lax.broadcasted_iota(jnp.int32, z.shape, 0)
    prev = jnp.where(t == 0, 0.0, pltpu.roll(z, 1, axis=0))
    nxt = jnp.where(t == length - 1, 0.0, pltpu.roll(z, length - 1, axis=0))
    return b_ref[...] + prev * w_ref[0:1, :] + z * w_ref[1:2, :] + nxt * w_ref[2:3, :]


def _hy_conv_kernel(z0_ref, z1_ref, z2_ref, w0_ref, w1_ref, w2_ref, b0_ref, b1_ref, b2_ref,
                    x0_ref, gg_ref):
    x0_ref[0] = _conv3(z0_ref[0], w0_ref, b0_ref)
    x1 = _conv3(z1_ref[0], w1_ref, b1_ref)
    v = _conv3(z2_ref[0], w2_ref, b2_ref)
    gg_ref[0] = v * x1


def _hy_conv(z, conv_w, conv_b, d):
    bsz, length, _ = z.shape
    tc = 256
    nb = d // tc
    zspec = lambda k: pl.BlockSpec((1, length, tc), lambda b, j: (b, 0, k * nb + j))
    wspec = lambda k: pl.BlockSpec((3, tc), lambda b, j: (0, k * nb + j))
    bspec = lambda k: pl.BlockSpec((1, tc), lambda b, j: (0, k * nb + j))
    ospec = pl.BlockSpec((1, length, tc), lambda b, j: (b, 0, j))
    cb = conv_b.reshape(1, 3 * d)
    return pl.pallas_call(
        _hy_conv_kernel,
        grid=(bsz, nb),
        in_specs=[zspec(0), zspec(1), zspec(2), wspec(0), wspec(1), wspec(2), bspec(0), bspec(1), bspec(2)],
        out_specs=[ospec, ospec],
        out_shape=[jax.ShapeDtypeStruct((bsz, length, d), F32)] * 2,
        compiler_params=_cparams("parallel", "parallel"),
        name="hy_conv",
    )(z, z, z, conv_w, conv_w, conv_w, cb, cb, cb)


def _dft_tables(length):
    n2 = 2 * length
    lo = GRID_W
    w = 2.0 * math.pi / n2
    f = jnp.arange(length, dtype=jnp.int32)[:, None]
    n_hi = jnp.arange(length // lo, dtype=jnp.int32)[None, :] * lo
    n_lo = jnp.arange(lo, dtype=jnp.int32)[None, :]
    ang_hi = ((f * n_hi) % n2).astype(F32) * w
    ang_lo = ((f * n_lo) % n2).astype(F32) * w
    c_hi, s_hi = jnp.cos(ang_hi)[:, :, None], jnp.sin(ang_hi)[:, :, None]
    c_lo, s_lo = jnp.cos(ang_lo)[:, None, :], jnp.sin(ang_lo)[:, None, :]
    cf = (c_hi * c_lo - s_hi * s_lo).reshape(length, length)
    sn = (s_hi * c_lo + c_hi * s_lo).reshape(length, length)
    n = jnp.arange(length, dtype=jnp.int32)[None, :]
    sgn_n = jnp.where(n % 2 == 0, 1.0, -1.0).astype(F32)
    sf = jnp.where(f == 0, sgn_n, -sn)
    return cf.astype(BF16), sf.astype(BF16), sf.T.astype(BF16)


def _hy_fwd_kernel(cf_ref, sf_ref, gg_ref, kre_ref, kim_ref, y_ref, gb_ref):
    j = pl.program_id(1)

    @pl.when(j == 0)
    def _():
        gb_ref[...] = gg_ref[0].astype(BF16)

    ure = jnp.dot(cf_ref[...], gb_ref[...], preferred_element_type=F32)
    uim = jnp.dot(sf_ref[...], gb_ref[...], preferred_element_type=F32)
    kre = kre_ref[...]
    kim = kim_ref[...]
    tf = ure.shape[0]
    row = lax.broadcasted_iota(jnp.int32, ure.shape, 0) + j * tf
    packed = row == 0
    yre = ure * kre - jnp.where(packed, 0.0, uim * kim)
    yim = uim * jnp.where(packed, kim, kre) + jnp.where(packed, 0.0, ure * kim)
    y_ref[0, 0] = yre.astype(BF16)
    y_ref[0, 1] = yim.astype(BF16)


def _hy_inv_kernel(ci_ref, si_ref, y_ref, gg_ref, x0_ref, skip_ref, o_ref):
    y = (jnp.dot(ci_ref[...], y_ref[0, 0], preferred_element_type=F32)
         + jnp.dot(si_ref[...], y_ref[0, 1], preferred_element_type=F32))
    o_ref[0] = ((y + gg_ref[0] * skip_ref[...]) * x0_ref[0]).astype(o_ref.dtype)


def _hy_long_conv_gate(gg, x0, kre, kim, skip, tables):
    bsz, length, d = gg.shape
    cf, sf, sft = tables
    tf = _pick(length, (512, 256))
    y = pl.pallas_call(
        _hy_fwd_kernel,
        grid=(bsz, length // tf),
        in_specs=[
            pl.BlockSpec((tf, length), lambda b, j: (j, 0)),
            pl.BlockSpec((tf, length), lambda b, j: (j, 0)),
            pl.BlockSpec((1, length, d), lambda b, j: (b, 0, 0)),
            pl.BlockSpec((tf, d), lambda b, j: (j, 0)),
            pl.BlockSpec((tf, d), lambda b, j: (j, 0)),
        ],
        out_specs=pl.BlockSpec((1, 2, tf, d), lambda b, j: (b, 0, j, 0)),
        out_shape=jax.ShapeDtypeStruct((bsz, 2, length, d), BF16),
        scratch_shapes=[pltpu.VMEM((length, d), BF16)],
        compiler_params=_cparams("parallel", "arbitrary"),
        name="hy_dft_fwd",
    )(cf, sf, gg, kre, kim)
    tt = _pick(length, (512, 256))
    return pl.pallas_call(
        _hy_inv_kernel,
        grid=(bsz, length // tt),
        in_specs=[
            pl.BlockSpec((tt, length), lambda b, i: (i, 0)),
            pl.BlockSpec((tt, length), lambda b, i: (i, 0)),
            pl.BlockSpec((1, 2, length, d), lambda b, i: (b, 0, 0, 0)),
            pl.BlockSpec((1, tt, d), lambda b, i: (b, i, 0)),
            pl.BlockSpec((1, tt, d), lambda b, i: (b, i, 0)),
            pl.BlockSpec((1, d), lambda b, i: (0, 0)),
        ],
        out_specs=pl.BlockSpec((1, tt, d), lambda b, i: (b, i, 0)),
        out_shape=jax.ShapeDtypeStruct((bsz, length, d), BF16),
        compiler_params=_cparams("parallel", "arbitrary"),
        name="hy_dft_inv",
    )(cf, sft, y, gg, x0, skip.reshape(1, d))


def _hyena_filter(length, w1, b1, w2, b2, w3, b3, sin_freq):
    t = jnp.linspace(0.0, 1.0, length, dtype=F32)[:, None]
    bands = (HY_EMB_DIM - 1) // 2
    w = (2.0 * math.pi / length) * jnp.arange(length, dtype=F32)[:, None]
    f = jnp.linspace(1e-4, bands - 1, bands, dtype=F32)[None, :]
    z = jnp.concatenate([t, jnp.cos(f * w), -jnp.sin(f * w)], axis=-1)
    hp = lax.Precision.HIGHEST
    h = jnp.sin(sin_freq[0] * (jnp.dot(z, w1, precision=hp) + b1))
    h = jnp.sin(sin_freq[1] * (jnp.dot(h, w2, precision=hp) + b2))
    h = jnp.dot(h, w3, precision=hp) + b3
    max_decay = math.log(HY_DECAY_TARGET) / HY_FAST_DECAY
    min_decay = math.log(HY_DECAY_TARGET) / HY_SLOW_DECAY
    deltas = jnp.abs(jnp.linspace(min_decay, max_decay, h.shape[-1] // 2, dtype=F32))
    deltas = jnp.tile(deltas, 2)
    return h * (jnp.exp(-t * deltas) + HY_MOD_SHIFT)


def _hy_filter_spectrum(filt, d, tables):
    length = filt.shape[0]
    cf, sf, _ = tables
    row0 = (jnp.arange(length) == 0)[:, None]
    hf = filt[:, :d]
    hb0 = jnp.where(row0, 0.0, filt[:, d:])
    tab = jnp.concatenate([cf, sf], axis=0)[None]
    spec = _proj(tab, jnp.concatenate([hf, hb0], axis=1))[0]
    wgt = jnp.where(row0, 1.0, 2.0).astype(F32) / (2 * length)
    kre = (spec[:length, :d] + spec[:length, d:]) * wgt
    kim = (spec[length:, :d] + jnp.where(row0, 1.0, -1.0) * spec[length:, d:]) * wgt
    return kre, kim


def _hyena_mixer(x, ctx, g, sh, sc, csh, csc, w_in, conv_w, conv_b, f_w1, f_b1, f_w2, f_b2, f_w3, f_b3,
                 sin_freq, skip, w_o, gate, cgate):
    d = x.shape[-1]

    def one(u, shift, scale, res, gt):
        length = u.shape[1]
        z = _proj(u, w_in, g, shift, scale)
        x0, gg = _hy_conv(z, conv_w, conv_b, d)
        tables = _dft_tables(length)
        filt = _hyena_filter(length, f_w1, f_b1, f_w2, f_b2, f_w3, f_b3, sin_freq)
        kre, kim = _hy_filter_spectrum(filt, d, tables)
        y = _hy_long_conv_gate(gg, x0, kre, kim, skip, tables)
        return _proj_res(y, w_o, res, gt)

    return one(x, sh, sc, x, gate), (one(ctx, csh, csc, ctx, cgate) if cgate is not None else None)


def _ml_conv_kernel(z_ref, w_ref, b_ref, o_ref, *t_refs, out_scale):
    y = _conv3(z_ref[0], w_ref, b_ref)
    y = y * jax.nn.sigmoid(y) * out_scale
    o_ref[0] = y
    for t_ref in t_refs:
        t_ref[0] = y.T


def _ml_conv(z, conv_w, conv_b, col0, width, out_scale, transposed):
    bsz, length, _ = z.shape
    tc = 256
    j0 = col0 // tc
    out_specs = [pl.BlockSpec((1, length, tc), lambda b, j: (b, 0, j))]
    out_shape = [jax.ShapeDtypeStruct((bsz, length, width), F32)]
    if transposed:
        out_specs.append(pl.BlockSpec((1, tc, length), lambda b, j: (b, j, 0)))
        out_shape.append(jax.ShapeDtypeStruct((bsz, width, length), F32))
    return pl.pallas_call(
        functools.partial(_ml_conv_kernel, out_scale=out_scale),
        grid=(bsz, width // tc),
        in_specs=[pl.BlockSpec((1, length, tc), lambda b, j: (b, 0, j0 + j)),
                  pl.BlockSpec((3, tc), lambda b, j: (0, j0 + j)),
                  pl.BlockSpec((1, tc), lambda b, j: (0, j0 + j))],
        out_specs=out_specs,
        out_shape=out_shape,
        compiler_params=_cparams("parallel", "parallel"),
        name="ml_conv",
    )(z, conv_w, conv_b.reshape(1, -1))


def _ml_gates_kernel(x_ref, g_ref, sh_ref, sc_ref, w_ref, b_ref, o_ref, t_ref):
    x = x_ref[0]
    ms = jnp.mean(x * x, axis=-1, keepdims=True)
    h = x * lax.rsqrt(ms + NORM_EPS) * g_ref[...]
    h = h * (1.0 + sc_ref[0]) + sh_ref[0]
    gates = jnp.dot(h.astype(BF16), w_ref[...], preferred_element_type=F32) + b_ref[...]
    o_ref[0] = gates
    t_ref[0] = gates.T


def _ml_gates(x, g, shift, scale, w_gate, gate_b):
    bsz, length, d = x.shape
    pad = LANES - w_gate.shape[1]
    w = jnp.pad(w_gate, ((0, 0), (0, pad))).astype(BF16)
    b = jnp.pad(gate_b, (0, pad)).reshape(1, LANES)
    tm = _pick(length, (1024, 256))
    return pl.pallas_call(
        _ml_gates_kernel,
        grid=(bsz, length // tm),
        in_specs=[
            pl.BlockSpec((1, tm, d), lambda b_, i: (b_, i, 0)),
            pl.BlockSpec((1, d), lambda b_, i: (0, 0)),
            pl.BlockSpec((1, 1, d), lambda b_, i: (b_, 0, 0)),
            pl.BlockSpec((1, 1, d), lambda b_, i: (b_, 0, 0)),
            pl.BlockSpec((d, LANES), lambda b_, i: (0, 0)),
            pl.BlockSpec((1, LANES), lambda b_, i: (0, 0)),
        ],
        out_specs=[pl.BlockSpec((1, tm, LANES), lambda b_, i: (b_, i, 0)),
                   pl.BlockSpec((1, LANES, tm), lambda b_, i: (b_, 0, i))],
        out_shape=[jax.ShapeDtypeStruct((bsz, length, LANES), F32),
                   jax.ShapeDtypeStruct((bsz, LANES, length), F32)],
        compiler_params=_cparams("parallel", "parallel"),
        name="ml_gates",
    )(x, g.reshape(1, d), shift.reshape(bsz, 1, d), scale.reshape(bsz, 1, d), w, b)


def _log_sigmoid(x):
    return jnp.minimum(x, 0.0) - jnp.log1p(jnp.exp(-jnp.abs(x)))


def _ml_chunk_pair(q2, kt2, k2, v2, gates, i_rows, f_rows, f_lanes, state, reverse):
    ct2, n2, ms = state
    tlen = q2.shape[0]
    dk, dv = ML_QK_DIM, ML_V_DIM
    ti = lax.broadcasted_iota(jnp.int32, (tlen, tlen), 0)
    si = lax.broadcasted_iota(jnp.int32, (tlen, tlen), 1)
    causal = (si >= ti) if reverse else (si <= ti)
    causal_t = (ti >= si) if reverse else (ti <= si)
    lane = lax.broadcasted_iota(jnp.int32, (1, LANES), 1)
    sub = lax.broadcasted_iota(jnp.int32, (LANES, 1), 0)
    kt_b = kt2.astype(BF16)
    k_b = k2.astype(BF16)
    v_b = v2.astype(BF16)
    ct_b = ct2.astype(BF16)
    hs, ws, decays, m_news = [], [], [], []
    for j in range(2):
        own = (lane >= dk) if j else (lane < dk)
        lf_row = _log_sigmoid(f_rows[j])
        f_col = jnp.where(lane == f_lanes[j], gates, 0.0).sum(axis=-1, keepdims=True)
        lf_col = _log_sigmoid(f_col)
        b_col = jnp.where(causal, lf_row, 0.0).sum(axis=-1, keepdims=True)
        b_row = jnp.where(causal_t, lf_col, 0.0).sum(axis=0, keepdims=True)
        log_d = jnp.where(causal, b_col - b_row + i_rows[j], -jnp.inf)
        m_inter = b_col + ms[j]
        m_t = jnp.maximum(log_d.max(axis=-1, keepdims=True), m_inter)
        qj = jnp.where(own, q2, 0.0)
        qj_b = qj.astype(BF16)
        s = jnp.dot(qj_b, kt_b, preferred_element_type=F32) * jnp.exp(log_d - m_t)
        inter = jnp.exp(m_inter - m_t)
        sv = jnp.dot(s.astype(BF16), v_b, preferred_element_type=F32)[:, j * dv:(j + 1) * dv]
        num = sv + inter * jnp.dot(qj_b, ct_b, preferred_element_type=F32)
        folded = inter * (qj * n2)
        for c in range(0, tlen, LANES):
            folded = folded + s[:, c:c + LANES]
        qn = folded.sum(axis=-1, keepdims=True)
        hs.append(num / jnp.maximum(jnp.abs(qn), jnp.exp(-m_t)))
        b_end = b_row[:, 0:1] if reverse else b_row[:, tlen - 1:tlen]
        w_log = b_end - b_row + i_rows[j]
        m_new = jnp.maximum(b_end + ms[j], w_log.max(axis=-1, keepdims=True))
        ws.append(jnp.exp(w_log - m_new))
        decays.append(jnp.exp(b_end + ms[j] - m_new))
        m_news.append(m_new)
    head0_rows = sub < dk
    head0_lanes = lane < dk
    kw = (kt2 * jnp.where(head0_rows, ws[0], ws[1])).astype(BF16)
    upd = jnp.dot(kw, v_b, preferred_element_type=F32)
    ct_new = (jnp.where(head0_rows, decays[0], decays[1]) * ct2
              + jnp.where(head0_rows, upd[:, :dv], upd[:, dv:]))
    wk = [jnp.dot(w.astype(BF16), k_b, preferred_element_type=F32) for w in ws]
    n_new = jnp.where(head0_lanes, decays[0], decays[1]) * n2 + jnp.where(head0_lanes, wk[0], wk[1])
    return jnp.concatenate(hs, axis=-1), (ct_new, n_new, tuple(m_news))


def _mlstm_kernel(ql_ref, kl_ref, ktl_ref, vl_ref, ol_ref, gl_ref, gtl_ref,
                  qc_ref, kc_ref, ktc_ref, vc_ref, oc_ref, gc_ref, gtc_ref, ng_ref,
                  yl_ref, yc_ref, hfl_ref, hbl_ref, hfc_ref, hbc_ref):
    tlen = ML_CHUNK
    dv, nh = ML_V_DIM, ML_HEADS
    n_lat = ql_ref.shape[1] // tlen
    n_ctx = qc_ref.shape[1] // tlen
    head0 = 2 * pl.program_id(1)

    def run(refs, c, state, direction):
        q_ref, k_ref, kt_ref, v_ref, g_ref, gt_ref, hf_ref, hb_ref = refs
        rows = pl.ds(pl.multiple_of(c * tlen, tlen), tlen)
        gate_i, gate_f = 2 * direction * nh, (2 * direction + 1) * nh
        i_rows = [gt_ref[0, pl.ds(gate_i + head0 + j, 1), rows] for j in range(2)]
        f_rows = [gt_ref[0, pl.ds(gate_f + head0 + j, 1), rows] for j in range(2)]
        f_lanes = [gate_f + head0 + j for j in range(2)]
        h, state = _ml_chunk_pair(q_ref[0, rows, :], kt_ref[0, :, rows], k_ref[0, rows, :], v_ref[0, rows, :],
                                  g_ref[0, rows, :], i_rows, f_rows, f_lanes, state, direction == 1)
        (hb_ref if direction else hf_ref)[rows, :] = h
        return state

    def sweep(refs, n_chunks, states):
        def body(c, sts):
            return run(refs, c, sts[0], 0), run(refs, n_chunks - 1 - c, sts[1], 1)
        return lax.fori_loop(0, n_chunks, body, states)

    zero = (jnp.zeros((LANES, dv), F32), jnp.zeros((1, LANES), F32),
            (jnp.zeros((1, 1), F32), jnp.zeros((1, 1), F32)))
    lat = (ql_ref, kl_ref, ktl_ref, vl_ref, gl_ref, gtl_ref, hfl_ref, hbl_ref)
    ctx = (qc_ref, kc_ref, ktc_ref, vc_ref, gc_ref, gtc_ref, hfc_ref, hbc_ref)
    sweep(lat, n_lat, sweep(ctx, n_ctx, (zero, zero)))

    def finish(hf_ref, hb_ref, o_ref, y_ref):
        for j in range(2):
            vs = slice(j * dv, (j + 1) * dv)
            h = hf_ref[:, vs] + hb_ref[:, vs]
            h = h * lax.rsqrt(jnp.mean(h * h, axis=-1, keepdims=True) + NORM_EPS) * ng_ref[:, vs]
            y_ref[0, :, vs] = (h * jax.nn.sigmoid(o_ref[0, :, vs])).astype(y_ref.dtype)

    finish(hfl_ref, hbl_ref, ol_ref, yl_ref)
    finish(hfc_ref, hbc_ref, oc_ref, yc_ref)


def _mlstm_mixer(x, ctx, g, sh, sc, csh, csc, w_in, conv_w, conv_b, gate_b, out_norm_g, w_o, gate, cgate):
    nh, dk, dv = ML_HEADS, ML_QK_DIM, ML_V_DIM
    nqk = 2 * nh * dk
    wide = nqk + 2 * nh * dv
    w_main = w_in[:, :wide].astype(BF16)
    w_gate = w_in[:, wide:]

    def project(u, shift, scale):
        z = _proj(u, w_main, g, shift, scale)
        gates, gates_t = _ml_gates(u, g, shift, scale, w_gate, gate_b)
        (q,) = _ml_conv(z, conv_w, conv_b, 0, nh * dk, 1.0, False)
        k, kt = _ml_conv(z, conv_w, conv_b, nh * dk, nh * dk, dk ** -0.5, True)
        return z, q, k, kt, gates, gates_t

    zl, ql, kl, ktl, gl, gtl = project(x, sh, sc)
    zc, qc, kc, ktc, gc, gtc = project(ctx, csh, csc)
    bsz, length, d = x.shape
    lc = ctx.shape[1]
    npair = nh // 2
    qw, vw = 2 * dk, 2 * dv
    v_off, o_off = nqk // vw, (nqk + nh * dv) // vw

    def specs(n):
        return [
            pl.BlockSpec((1, n, qw), lambda b, h: (b, 0, h)),
            pl.BlockSpec((1, n, qw), lambda b, h: (b, 0, h)),
            pl.BlockSpec((1, qw, n), lambda b, h: (b, h, 0)),
            pl.BlockSpec((1, n, vw), lambda b, h: (b, 0, v_off + h)),
            pl.BlockSpec((1, n, vw), lambda b, h: (b, 0, o_off + h)),
            pl.BlockSpec((1, n, LANES), lambda b, h: (b, 0, 0)),
            pl.BlockSpec((1, LANES, n), lambda b, h: (b, 0, 0)),
        ]

    yl, yc = pl.pallas_call(
        _mlstm_kernel,
        grid=(bsz, npair),
        in_specs=specs(length) + specs(lc) + [pl.BlockSpec((1, vw), lambda b, h: (0, h))],
        out_specs=[pl.BlockSpec((1, length, vw), lambda b, h: (b, 0, h)),
                   pl.BlockSpec((1, lc, vw), lambda b, h: (b, 0, h))],
        out_shape=[jax.ShapeDtypeStruct((bsz, length, nh * dv), BF16),
                   jax.ShapeDtypeStruct((bsz, lc, nh * dv), BF16)],
        scratch_shapes=[pltpu.VMEM((length, vw), F32), pltpu.VMEM((length, vw), F32),
                        pltpu.VMEM((lc, vw), F32), pltpu.VMEM((lc, vw), F32)],
        compiler_params=_cparams("parallel", "parallel"),
        name="mlstm",
    )(ql, kl, ktl, zl, zl, gl, gtl, qc, kc, ktc, zc, zc, gc, gtc, out_norm_g.reshape(1, nh * dv))
    return _proj_res(yl, w_o, x, gate), (_proj_res(yc, w_o, ctx, cgate) if cgate is not None else None)


def _router_kernel(x_ref, g_ref, sh_ref, sc_ref, rw_ref, h_ref, aff_ref):
    x = x_ref[0]
    ms = jnp.mean(x * x, axis=-1, keepdims=True)
    h = x * lax.rsqrt(ms + NORM_EPS) * g_ref[...]
    h = h * (1.0 + sc_ref[0]) + sh_ref[0]
    h_ref[0] = h.astype(BF16)
    logits = lax.dot_general(rw_ref[...], h, (((1,), (1,)), ((), ())), precision=HIGHEST,
                             preferred_element_type=F32)
    mx = logits.max(axis=0, keepdims=True)
    p = jnp.exp(logits - mx)
    aff_ref[0] = p / p.sum(axis=0, keepdims=True)


def _router(x, g, shift, scale, router_w):
    bsz, length, d = x.shape
    ne = router_w.shape[1]
    tm = _pick(length, (1024, 256))
    return pl.pallas_call(
        _router_kernel,
        grid=(bsz, length // tm),
        in_specs=[
            pl.BlockSpec((1, tm, d), lambda b, i: (b, i, 0)),
            pl.BlockSpec((1, d), lambda b, i: (0, 0)),
            pl.BlockSpec((1, 1, d), lambda b, i: (b, 0, 0)),
            pl.BlockSpec((1, 1, d), lambda b, i: (b, 0, 0)),
            pl.BlockSpec((ne, d), lambda b, i: (0, 0)),
        ],
        out_specs=[pl.BlockSpec((1, tm, d), lambda b, i: (b, i, 0)),
                   pl.BlockSpec((1, ne, tm), lambda b, i: (b, 0, i))],
        out_shape=[jax.ShapeDtypeStruct((bsz, length, d), BF16),
                   jax.ShapeDtypeStruct((bsz, ne, length), F32)],
        compiler_params=_cparams("parallel", "parallel"),
        name="moe_router",
    )(x, g.reshape(1, d), shift.reshape(bsz, 1, d), scale.reshape(bsz, 1, d), router_w.T)


def _excl_cumsum_lanes(flags):
    rows, length = flags.shape
    cw = min(length, 256)
    si = lax.broadcasted_iota(jnp.int32, (cw, cw), 0)
    ti = lax.broadcasted_iota(jnp.int32, (cw, cw), 1)
    upper = jnp.where(si < ti, 1.0, 0.0).astype(BF16)
    carry = jnp.zeros((rows, 1), F32)
    parts = []
    for c in range(length // cw):
        blk = flags[:, c * cw:(c + 1) * cw]
        parts.append(jnp.dot(blk.astype(BF16), upper, preferred_element_type=F32) + carry)
        carry = carry + blk.sum(axis=-1, keepdims=True)
    return parts[0] if len(parts) == 1 else jnp.concatenate(parts, axis=-1)


def _select_kernel(aff_ref, pos_ref, *, cap):
    a = aff_ref[...]
    rows = a.shape[0]
    capf = float(cap)

    def bisect(_, lohi):
        lo, hi = lohi
        mid = 0.5 * (lo + hi)
        cnt = jnp.where(a >= mid, 1.0, 0.0).sum(axis=-1, keepdims=True)
        ge = cnt >= capf
        return jnp.where(ge, mid, lo), jnp.where(ge, hi, mid)

    lo, hi = lax.fori_loop(0, SELECT_BISECTIONS, bisect,
                           (jnp.zeros((rows, 1), F32), jnp.full((rows, 1), 2.0, F32)))
    above = a >= hi
    tie = jnp.logical_and(a >= lo, jnp.logical_not(above))
    n_above = jnp.where(above, 1.0, 0.0).sum(axis=-1, keepdims=True)
    tie_rank = _excl_cumsum_lanes(jnp.where(tie, 1.0, 0.0))
    sel = jnp.logical_or(above, jnp.logical_and(tie, tie_rank < capf - n_above))
    slot = _excl_cumsum_lanes(jnp.where(sel, 1.0, 0.0))
    pos_ref[...] = jnp.where(sel, slot, -1.0).astype(jnp.int32)


def _select(aff2d, cap):
    return pl.pallas_call(
        functools.partial(_select_kernel, cap=cap),
        out_shape=jax.ShapeDtypeStruct(aff2d.shape, jnp.int32),
        compiler_params=pltpu.CompilerParams(vmem_limit_bytes=V7X_VMEM_LIMIT_BYTES),
        name="moe_select",
    )(aff2d)


GATHER_EXPERTS_PER_STEP = 4


def _gather_kernel(h_ref, pos_ref, aff_ref, xg_ref, gs_ref, *, cap):
    length = pos_ref.shape[3]
    slot = lax.broadcasted_iota(jnp.int32, (cap, length), 0)
    hits = [slot == pos_ref[0, k] for k in range(GATHER_EXPERTS_PER_STEP)]
    onehot = jnp.concatenate([jnp.where(hit, 1.0, 0.0).astype(BF16) for hit in hits], axis=0)
    xg = jnp.dot(onehot, h_ref[0], preferred_element_type=F32).astype(BF16)
    for k, hit in enumerate(hits):
        xg_ref[k] = xg[k * cap:(k + 1) * cap]
        gs_ref[k] = jnp.where(hit, aff_ref[0, k], 0.0).sum(axis=-1, keepdims=True)


def _gather(h, pos, aff, cap):
    bsz, length, d = h.shape
    ne = pos.shape[1]
    per = GATHER_EXPERTS_PER_STEP
    return pl.pallas_call(
        functools.partial(_gather_kernel, cap=cap),
        grid=(bsz, ne // per),
        in_specs=[
            pl.BlockSpec((1, length, d), lambda b, e: (b, 0, 0)),
            pl.BlockSpec((1, per, 1, length), lambda b, e: (b, e, 0, 0)),
            pl.BlockSpec((1, per, 1, length), lambda b, e: (b, e, 0, 0)),
        ],
        out_specs=[pl.BlockSpec((per, cap, d), lambda b, e: (e, b, 0)),
                   pl.BlockSpec((per, cap, 1), lambda b, e: (e, b, 0))],
        out_shape=[jax.ShapeDtypeStruct((ne, bsz * cap, d), BF16),
                   jax.ShapeDtypeStruct((ne, bsz * cap, 1), F32)],
        compiler_params=_cparams("parallel", "arbitrary"),
        name="moe_gather",
    )(h, pos.reshape(bsz, ne, 1, length), aff.reshape(bsz, ne, 1, length))


FFN_TILES_PER_STEP = 2


def _ffn_kernel(*refs, n_groups, n_up, n_tiles, tf):
    per = FFN_TILES_PER_STEP
    xg_refs = refs[:n_groups]
    gs_refs = refs[n_groups:2 * n_groups]
    w_refs = refs[2 * n_groups:2 * n_groups + 2 * per + 1]
    wg_refs, wu_refs, wd_ref = w_refs[:per], w_refs[per:2 * per], w_refs[2 * per]
    y_refs = refs[2 * n_groups + 2 * per + 1:3 * n_groups + 2 * per + 1]
    z_refs = refs[3 * n_groups + 2 * per + 1:]
    step = pl.program_id(1)

    for k in range(per):
        tile = per * step + k

        @pl.when(tile < n_tiles)
        def _():
            wg = wg_refs[k][0, 0].astype(BF16)
            wu = wu_refs[k][0, 0].astype(BF16)
            cols = pl.ds(pl.multiple_of(tile * tf, tf), tf)
            for xg_ref, z_ref in zip(xg_refs, z_refs):
                xg = xg_ref[0]
                a = jnp.dot(xg, wg, preferred_element_type=F32)
                u = jnp.dot(xg, wu, preferred_element_type=F32)
                z_ref[:, cols] = (a * jax.nn.sigmoid(a) * u).astype(BF16)

    @pl.when(step >= n_up)
    def _():
        wd = wd_ref[0, 0].astype(BF16)
        for gs_ref, y_ref, z_ref in zip(gs_refs, y_refs, z_refs):
            y = jnp.dot(z_ref[...], wd, preferred_element_type=F32)
            y_ref[0] = (y * gs_ref[0]).astype(BF16)


def _expert_ffn(xgs, gss, w_gate, w_up, w_down, layer):
    ne, _, d = xgs[0].shape
    ff = w_gate.shape[3]
    tf = 256
    tn = 256
    per = FFN_TILES_PER_STEP
    n_tiles, n_down = ff // tf, d // tn
    n_up = -(-n_tiles // per)
    n = len(xgs)
    tok = lambda a: pl.BlockSpec((1,) + a.shape[1:], lambda e, s: (e, 0, 0))
    up_specs = [pl.BlockSpec((1, 1, d, tf),
                             lambda e, s, k=k: (layer, e, 0, jnp.minimum(per * s + k, n_tiles - 1)))
                for k in range(per)]
    down_tile = lambda e, s: (layer, e, 0, jnp.maximum(s - n_up, 0))
    return pl.pallas_call(
        functools.partial(_ffn_kernel, n_groups=n, n_up=n_up, n_tiles=n_tiles, tf=tf),
        grid=(ne, n_up + n_down),
        in_specs=[tok(a) for a in xgs] + [tok(a) for a in gss] + up_specs + up_specs + [
            pl.BlockSpec((1, 1, ff, tn), down_tile),
        ],
        out_specs=[pl.BlockSpec((1, a.shape[1], tn), lambda e, s: (e, 0, jnp.maximum(s - n_up, 0)))
                   for a in xgs],
        out_shape=[jax.ShapeDtypeStruct(a.shape, BF16) for a in xgs],
        scratch_shapes=[pltpu.VMEM((a.shape[1], ff), BF16) for a in xgs],
        compiler_params=_cparams("parallel", "arbitrary"),
        name="moe_ffn",
    )(*xgs, *gss, *([w_gate] * per), *([w_up] * per), w_down)


def _combine_kernel(post_ref, y_ref, x_ref, gt_ref, fg_ref, o_ref, *, cap, final_norm):
    post = post_ref[0]
    tl, ne = post.shape
    d = y_ref.shape[2]
    slot = lax.broadcasted_iota(jnp.int32, (tl, cap), 1)
    hits = [jnp.where(post[:, e:e + 1] == slot, 1.0, 0.0).astype(BF16) for e in range(ne)]
    if cap % LANES == 0:
        acc = jnp.dot(jnp.concatenate(hits, axis=1), y_ref[...].reshape(ne * cap, d),
                      preferred_element_type=F32)
    else:
        acc = jnp.zeros((tl, d), F32)
        for e in range(ne):
            acc = acc + jnp.dot(hits[e], y_ref[e], preferred_element_type=F32)
    out = x_ref[0] + gt_ref[0] * acc
    if final_norm:
        ms = jnp.mean(out * out, axis=-1, keepdims=True)
        out = out * lax.rsqrt(ms + NORM_EPS) * fg_ref[...]
    o_ref[0] = out


def _combine(pos_t, y, x, gate, cap, final_g=None):
    bsz, length, d = x.shape
    ne = y.shape[0]
    tl = _pick(length, (512, 256))
    final_norm = final_g is not None
    fg = (final_g if final_norm else jnp.ones((d,), F32)).reshape(1, d)
    return pl.pallas_call(
        functools.partial(_combine_kernel, cap=cap, final_norm=final_norm),
        grid=(bsz, length // tl),
        in_specs=[
            pl.BlockSpec((1, tl, ne), lambda b, i: (b, i, 0)),
            pl.BlockSpec((ne, cap, d), lambda b, i: (0, b, 0)),
            pl.BlockSpec((1, tl, d), lambda b, i: (b, i, 0)),
            pl.BlockSpec((1, 1, d), lambda b, i: (b, 0, 0)),
            pl.BlockSpec((1, d), lambda b, i: (0, 0)),
        ],
        out_specs=pl.BlockSpec((1, tl, d), lambda b, i: (b, i, 0)),
        out_shape=jax.ShapeDtypeStruct(x.shape, F32),
        compiler_params=_cparams("parallel", "arbitrary"),
        name="moe_combine",
    )(pos_t, y, x, gate.reshape(bsz, 1, d), fg)


def _route(x, g, shift, scale, router_w):
    bsz, length, _ = x.shape
    ne = router_w.shape[1]
    cap = max(1, EC_CAPACITY_FACTOR * length // ne)
    h, aff = _router(x, g, shift, scale, router_w)
    pos = _select(aff.reshape(bsz * ne, length), cap).reshape(bsz, ne, length)
    xg, gs = _gather(h, pos, aff, cap)
    return xg, gs, pos.transpose(0, 2, 1), cap


def _moe(x, ctx, g, sh, sc, gate, csh, csc, cgate, router_w, w_gate, w_up, w_down, layer, final_g=None):
    xg, gs, pos_t, cap = _route(x, g, sh, sc, router_w)
    if ctx is None:
        (y,) = _expert_ffn([xg], [gs], w_gate, w_up, w_down, layer)
        return _combine(pos_t, y, x, gate, cap, final_g), None
    xg_c, gs_c, pos_tc, cap_c = _route(ctx, g, csh, csc, router_w)
    y, y_c = _expert_ffn([xg, xg_c], [gs, gs_c], w_gate, w_up, w_down, layer)
    return _combine(pos_t, y, x, gate, cap), _combine(pos_tc, y_c, ctx, cgate, cap_c)


def kernel(x, c, ctx, c_ctx, mod_w, mod_b, norm_mix_g, norm_ffn_g, router_w, moe_w_gate, moe_w_up, moe_w_down, na_w_qkv, na_rpb, na_w_o, mla_w_in, mla_q_norm_g, mla_w_q_b, mla_kv_norm_g, mla_w_kv_b, mla_w_o, hy_w_in, hy_conv_w, hy_conv_b, hy_f_w1, hy_f_b1, hy_f_w2, hy_f_b2, hy_f_w3, hy_f_b3, hy_sin_freq, hy_skip, hy_w_o, ml_w_in, ml_conv_w, ml_conv_b, ml_gate_b, ml_out_norm_g, ml_w_o, final_norm_g):
    bsz, _, d = x.shape
    depth = mod_w.shape[0]
    n_mixers = 4
    cond = jnp.concatenate([c, c_ctx[None, :]], axis=0)
    cond = jnp.pad(cond, ((0, (-cond.shape[0]) % 8), (0, 0)))
    mods = _modulation(cond, mod_w, mod_b)
    for i in range(depth):
        last = i == depth - 1
        mod = mods[i]
        sh1, sc1, g1, sh2, sc2, g2 = [mod[:bsz, k * d:(k + 1) * d] for k in range(6)]
        bc = lambda v: jnp.broadcast_to(v[None, :], (bsz, d))
        csh1, csc1, cg1, csh2, csc2, cg2 = [bc(mod[bsz, k * d:(k + 1) * d]) for k in range(6)]
        if last:
            cg1 = None
        kind, j = i % n_mixers, i // n_mixers
        gm = norm_mix_g[i]
        if kind == 0:
            x, ctx_new = _na_mixer(x, ctx, gm, sh1, sc1, csh1, csc1, na_w_qkv[j], na_rpb[j], na_w_o[j], g1, cg1)
        elif kind == 1:
            x, ctx_new = _mla_mixer(x, ctx, gm, sh1, sc1, csh1, csc1, mla_w_in[j], mla_q_norm_g[j],
                                    mla_w_q_b[j], mla_kv_norm_g[j], mla_w_kv_b[j], mla_w_o[j], g1, cg1)
        elif kind == 2:
            x, ctx_new = _hyena_mixer(x, ctx, gm, sh1, sc1, csh1, csc1, hy_w_in[j], hy_conv_w[j], hy_conv_b[j],
                                      hy_f_w1[j], hy_f_b1[j], hy_f_w2[j], hy_f_b2[j], hy_f_w3[j], hy_f_b3[j],
                                      hy_sin_freq[j], hy_skip[j], hy_w_o[j], g1, cg1)
        else:
            x, ctx_new = _mlstm_mixer(x, ctx, gm, sh1, sc1, csh1, csc1, ml_w_in[j], ml_conv_w[j], ml_conv_b[j],
                                      ml_gate_b[j], ml_out_norm_g[j], ml_w_o[j], g1, cg1)
        if last:
            x, _ = _moe(x, None, norm_ffn_g[i], sh2, sc2, g2, None, None, None,
                        router_w[i], moe_w_gate, moe_w_up, moe_w_down, i, final_norm_g)
        else:
            x, ctx = _moe(x, ctx_new, norm_ffn_g[i], sh2, sc2, g2, csh2, csc2, cg2,
                          router_w[i], moe_w_gate, moe_w_up, moe_w_down, i)
    return x
```

```python
import functools
import math

import numpy as np
import jax
import jax.numpy as jnp
from jax import lax
from jax.experimental import pallas as pl
from jax.experimental.pallas import tpu as pltpu

F32 = jnp.float32
BF16 = jnp.bfloat16
HIGHEST = lax.Precision.HIGHEST

V7X_VMEM_LIMIT_BYTES = 56 * 1024 * 1024
LANES = 128

GRID_W = 64
NORM_EPS = 1e-6
ROPE_BASE = 10000.0

NA_HEADS = 16
NA_HEAD_DIM = 64
NA_WIN_H = 8
NA_WIN_W = 16
NA_QROWS = 4
NA_KROWS = NA_QROWS - 1 + NA_WIN_H

MLA_HEADS = 16
MLA_Q_RANK = 384
MLA_KV_RANK = 256
MLA_NOPE_DIM = 64
MLA_ROPE_DIM = 32
MLA_V_DIM = 64

HY_EMB_DIM = 33
HY_DECAY_TARGET = 1e-2
HY_FAST_DECAY = 0.3
HY_SLOW_DECAY = 1.5
HY_MOD_SHIFT = 0.05

ML_HEADS = 8
ML_V_DIM = 128
ML_QK_DIM = 64
ML_CHUNK = 256

N_EXPERTS = 16
EC_CAPACITY_FACTOR = 2
SELECT_BISECTIONS = 64


def _cparams(*sem):
    return pltpu.CompilerParams(dimension_semantics=sem, vmem_limit_bytes=V7X_VMEM_LIMIT_BYTES)


def _pick(n, prefs):
    for p in prefs:
        if n % p == 0:
            return p
    return n


def _proj_kernel(x_ref, g_ref, sh_ref, sc_ref, w_ref, o_ref, h_ref, *, norm):
    @pl.when(pl.program_id(2) == 0)
    def _():
        x = x_ref[0].astype(F32)
        if norm:
            ms = jnp.mean(x * x, axis=-1, keepdims=True)
            x = x * lax.rsqrt(ms + NORM_EPS) * g_ref[...]
            x = x * (1.0 + sc_ref[0]) + sh_ref[0]
        h_ref[...] = x.astype(BF16)

    o_ref[0] = jnp.dot(h_ref[...], w_ref[...].astype(BF16),
                       preferred_element_type=F32).astype(o_ref.dtype)


def _proj(x, w, g=None, shift=None, scale=None, out_dtype=F32):
    bsz, length, kdim = x.shape
    n = w.shape[1]
    norm = g is not None
    if not norm:
        g = jnp.ones((kdim,), F32)
    if shift is None:
        shift = jnp.zeros((bsz, kdim), F32)
        scale = jnp.zeros((bsz, kdim), F32)
    tm = _pick(length, (1024, 512, 256))
    tn = _pick(n, (1024, 768, 512, 384, 256, 128))
    return pl.pallas_call(
        functools.partial(_proj_kernel, norm=norm),
        grid=(bsz, length // tm, n // tn),
        in_specs=[
            pl.BlockSpec((1, tm, kdim), lambda b, i, j: (b, i, 0)),
            pl.BlockSpec((1, kdim), lambda b, i, j: (0, 0)),
            pl.BlockSpec((1, 1, kdim), lambda b, i, j: (b, 0, 0)),
            pl.BlockSpec((1, 1, kdim), lambda b, i, j: (b, 0, 0)),
            pl.BlockSpec((kdim, tn), lambda b, i, j: (0, j)),
        ],
        out_specs=pl.BlockSpec((1, tm, tn), lambda b, i, j: (b, i, j)),
        out_shape=jax.ShapeDtypeStruct((bsz, length, n), out_dtype),
        scratch_shapes=[pltpu.VMEM((tm, kdim), BF16)],
        compiler_params=_cparams("parallel", "parallel", "arbitrary"),
        name="proj",
    )(x, g.reshape(1, kdim).astype(F32), shift.reshape(bsz, 1, kdim), scale.reshape(bsz, 1, kdim),
      w.astype(BF16))


def _proj_res_kernel(y_ref, w_ref, r_ref, gt_ref, o_ref):
    acc = jnp.dot(y_ref[0].astype(BF16), w_ref[...].astype(BF16), preferred_element_type=F32)
    o_ref[0] = r_ref[0] + gt_ref[0] * acc


def _proj_res(y, w, res, gate):
    bsz, length, kdim = y.shape
    n = w.shape[1]
    tm = _pick(length, (1024, 512, 256))
    tn = _pick(n, (1024, 512, 256, 128))
    return pl.pallas_call(
        _proj_res_kernel,
        grid=(bsz, length // tm, n // tn),
        in_specs=[
            pl.BlockSpec((1, tm, kdim), lambda b, i, j: (b, i, 0)),
            pl.BlockSpec((kdim, tn), lambda b, i, j: (0, j)),
            pl.BlockSpec((1, tm, tn), lambda b, i, j: (b, i, j)),
            pl.BlockSpec((1, 1, tn), lambda b, i, j: (b, 0, j)),
        ],
        out_specs=pl.BlockSpec((1, tm, tn), lambda b, i, j: (b, i, j)),
        out_shape=jax.ShapeDtypeStruct((bsz, length, n), F32),
        compiler_params=_cparams("parallel", "parallel", "arbitrary"),
        name="proj_res",
    )(y, w.astype(BF16), res, gate.reshape(bsz, 1, n))


def _mod_kernel(c_ref, w_ref, b_ref, o_ref):
    c = c_ref[...]
    s = c * jax.nn.sigmoid(c)
    o_ref[0] = jnp.dot(s.astype(BF16), w_ref[0].astype(BF16), preferred_element_type=F32) + b_ref[0]


def _modulation(cc, w, b):
    rows, d = cc.shape
    depth, _, n = w.shape
    tn = 512
    return pl.pallas_call(
        _mod_kernel,
        grid=(depth, n // tn),
        in_specs=[
            pl.BlockSpec((rows, d), lambda l, j: (0, 0)),
            pl.BlockSpec((1, d, tn), lambda l, j: (l, 0, j)),
            pl.BlockSpec((1, 1, tn), lambda l, j: (l, 0, j)),
        ],
        out_specs=pl.BlockSpec((1, rows, tn), lambda l, j: (l, 0, j)),
        out_shape=jax.ShapeDtypeStruct((depth, rows, n), F32),
        compiler_params=_cparams("parallel", "parallel"),
        name="modulation",
    )(cc, w, b.reshape(depth, 1, n))


ATTN_KEY_CHUNK = 512
LOG2_E = math.log2(math.e)


def _softmax_pv(s_parts, v_parts):
    m = den = acc = None
    for s, v in zip(s_parts, v_parts):
        m_part = s.max(axis=-1, keepdims=True)
        m_new = m_part if m is None else jnp.maximum(m, m_part)
        p = jnp.exp2(s - m_new)
        l = p.sum(axis=-1, keepdims=True)
        o = jnp.dot(p.astype(BF16), v, preferred_element_type=F32)
        if m is None:
            den, acc = l, o
        else:
            alpha = jnp.exp2(m - m_new)
            den, acc = alpha * den + l, alpha * acc + o
        m = m_new
    return acc / den


def _key_chunks(n):
    step = ATTN_KEY_CHUNK if n % ATTN_KEY_CHUNK == 0 else n
    return [slice(c, c + step) for c in range(0, n, step)]


HEAD_V = 64


def _pair_queries(q2, hw):
    if hw == LANES:
        return [q2[:, :LANES], q2[:, LANES:]]
    lane = lax.broadcasted_iota(jnp.int32, (1, LANES), 1)
    return [jnp.where(lane < hw, q2, jnp.zeros_like(q2)), jnp.where(lane >= hw, q2, jnp.zeros_like(q2))]


def _pair_keys(k_ref, hw, j, rows=None):
    rows = slice(None) if rows is None else rows
    return k_ref[0, rows, j * LANES:(j + 1) * LANES] if hw == LANES else k_ref[0, rows, :]


def _pair_select(o0, o1):
    lane = lax.broadcasted_iota(jnp.int32, (1, LANES), 1)
    return jnp.where(lane < HEAD_V, o0, o1)


def _pair_attn_kernel(*refs, nparts, hw):
    q_ref, o_ref = refs[0], refs[-1]
    k_refs = refs[1:1 + nparts]
    v_refs = refs[1 + nparts:1 + 2 * nparts]
    nt = (((1,), (1,)), ((), ()))
    outs = []
    for j, q in enumerate(_pair_queries(q_ref[0], hw)):
        s_parts, v_parts = [], []
        for k_ref, v_ref in zip(k_refs, v_refs):
            for rows in _key_chunks(k_ref.shape[1]):
                s_parts.append(lax.dot_general(q, _pair_keys(k_ref, hw, j, rows), nt,
                                               preferred_element_type=F32))
                v_parts.append(v_ref[0, rows, :])
        outs.append(_softmax_pv(s_parts, v_parts))
    o_ref[0] = _pair_select(*outs).astype(o_ref.dtype)


def _pair_attention(q_src, kv_srcs, hw, nh):
    q, q_off = q_src
    bsz, lq = q.shape[0], q.shape[1]
    tq = _pick(lq, (1024, 512, 256))
    qw = 2 * hw
    in_specs = [pl.BlockSpec((1, tq, qw), lambda b, h, i: (b, i, q_off // qw + h))]
    args = [q]
    for k_arr, k_off, _, _ in kv_srcs:
        in_specs.append(pl.BlockSpec((1, k_arr.shape[1], qw), lambda b, h, i, o=k_off // qw: (b, 0, o + h)))
        args.append(k_arr)
    for _, _, v_arr, v_off in kv_srcs:
        in_specs.append(pl.BlockSpec((1, v_arr.shape[1], LANES), lambda b, h, i, o=v_off // LANES: (b, 0, o + h)))
        args.append(v_arr)
    return pl.pallas_call(
        functools.partial(_pair_attn_kernel, nparts=len(kv_srcs), hw=hw),
        grid=(bsz, nh // 2, lq // tq),
        in_specs=in_specs,
        out_specs=pl.BlockSpec((1, tq, LANES), lambda b, h, i: (b, i, h)),
        out_shape=jax.ShapeDtypeStruct((bsz, lq, nh * HEAD_V), BF16),
        compiler_params=_cparams("parallel", "parallel", "arbitrary"),
        name="attention",
    )(*args)


def _na_bias_table(rpb, rows):
    nh, n_dr, n_dc = rpb.shape
    kh = min(NA_WIN_H, rows)
    nblk = rows // NA_QROWS
    w = GRID_W
    qc = np.arange(w)[:, None]
    kc = np.arange(w)[None, :]
    cs = np.clip(qc - NA_WIN_W // 2, 0, w - NA_WIN_W)
    col_ok = (kc >= cs) & (kc < cs + NA_WIN_W)
    dc = np.clip(kc - qc, -(NA_WIN_W - 1), NA_WIN_W - 1) + NA_WIN_W - 1
    pick_dc = (dc.reshape(1, w * w) == np.arange(n_dc)[:, None]).astype(np.float32)
    tile = jnp.dot(rpb.reshape(nh * n_dr, n_dc).astype(F32), jnp.asarray(pick_dc), precision=HIGHEST)
    tile = jnp.where(col_ok, tile.reshape(nh, n_dr, w, w), -1e30)
    masked = jnp.full((nh, w, w), -1e30, F32)
    tables = []
    for blk in (0, 1, nblk - 1):
        start = int(np.clip(blk * NA_QROWS - NA_WIN_H // 2, 0, rows - NA_KROWS))
        q_rows = []
        for qr in range(NA_QROWS):
            r = blk * NA_QROWS + qr
            rs = int(np.clip(r - kh // 2, 0, rows - kh))
            tiles = [tile[:, kr - r + NA_WIN_H - 1] if rs <= kr < rs + kh else masked
                     for kr in range(start, start + NA_KROWS)]
            q_rows.append(jnp.concatenate(tiles, axis=-1))
        tables.append(jnp.concatenate(q_rows, axis=1))
    return jnp.stack(tables, axis=1)


NA_BLOCKS_PER_STEP = 4


def _na_kernel(q_ref, kl_ref, vl_ref, kc_ref, vc_ref, bias_ref, o_ref, *, nblk, rows):
    nt = (((1,), (1,)), ((), ()))
    hw = NA_HEAD_DIM
    tq = NA_QROWS * GRID_W
    for sub in range(NA_BLOCKS_PER_STEP):
        i = pl.program_id(2) * NA_BLOCKS_PER_STEP + sub
        start = jnp.clip(i * NA_QROWS - NA_WIN_H // 2, 0, rows - NA_KROWS) * GRID_W
        start = pl.multiple_of(start, GRID_W)
        pat = jnp.where(i == 0, 0, jnp.where(i == nblk - 1, 2, 1))
        local = pl.ds(start, NA_KROWS * GRID_W)
        qrows = slice(sub * tq, (sub + 1) * tq)
        outs = []
        for j, q in enumerate(_pair_queries(q_ref[0, qrows, :], hw)):
            s_ctx = lax.dot_general(q, _pair_keys(kc_ref, hw, j), nt, preferred_element_type=F32)
            s_loc = (lax.dot_general(q, _pair_keys(kl_ref, hw, j, local), nt, preferred_element_type=F32)
                     + bias_ref[j, pat])
            outs.append(_softmax_pv([s_ctx, s_loc], [vc_ref[0], vl_ref[0, local, :]]))
        o_ref[0, qrows, :] = _pair_select(*outs).astype(o_ref.dtype)


def _na_mixer(x, ctx, g, sh, sc, csh, csc, w_qkv, rpb, w_o, gate, cgate):
    bsz, length, d = x.shape
    lc = ctx.shape[1]
    rows = length // GRID_W
    nblk = rows // NA_QROWS
    nh, hd = NA_HEADS, NA_HEAD_DIM
    w = jnp.concatenate([w_qkv[:, :d] * (hd ** -0.5 * LOG2_E), w_qkv[:, d:]], axis=1).astype(BF16)
    qkv = _proj(x, w, g, sh, sc, out_dtype=BF16)
    qkv_c = _proj(ctx, w, g, csh, csc, out_dtype=BF16)
    bias = _na_bias_table(rpb * LOG2_E, rows)
    tq = NA_QROWS * GRID_W
    tstep = NA_BLOCKS_PER_STEP * tq
    npair = nh // 2
    pw = 2 * hd
    y = pl.pallas_call(
        functools.partial(_na_kernel, nblk=nblk, rows=rows),
        grid=(npair, bsz, nblk // NA_BLOCKS_PER_STEP),
        in_specs=[
            pl.BlockSpec((1, tstep, pw), lambda h, b, i: (b, i, h)),
            pl.BlockSpec((1, length, pw), lambda h, b, i: (b, 0, npair + h)),
            pl.BlockSpec((1, length, pw), lambda h, b, i: (b, 0, 2 * npair + h)),
            pl.BlockSpec((1, lc, pw), lambda h, b, i: (b, 0, npair + h)),
            pl.BlockSpec((1, lc, pw), lambda h, b, i: (b, 0, 2 * npair + h)),
            pl.BlockSpec((2, 3, tq, NA_KROWS * GRID_W), lambda h, b, i: (h, 0, 0, 0)),
        ],
        out_specs=pl.BlockSpec((1, tstep, pw), lambda h, b, i: (b, i, h)),
        out_shape=jax.ShapeDtypeStruct((bsz, length, d), BF16),
        compiler_params=_cparams("parallel", "parallel", "arbitrary"),
        name="na_local",
    )(qkv, qkv, qkv, qkv_c, qkv_c, bias)
    if cgate is None:
        return _proj_res(y, w_o, x, gate), None
    yc = _pair_attention((qkv_c, 0), [(qkv_c, d, qkv_c, 2 * d)], hd, nh)
    return _proj_res(y, w_o, x, gate), _proj_res(yc, w_o, ctx, cgate)


def _axial_rope(length):
    t = jnp.arange(length)
    row = (t // GRID_W).astype(F32)
    col = (t % GRID_W).astype(F32)
    n_freq = MLA_ROPE_DIM // 4
    inv = ROPE_BASE ** (-jnp.arange(n_freq, dtype=F32) / n_freq)
    ang = jnp.concatenate([row[:, None] * inv, col[:, None] * inv], axis=-1)
    return jnp.cos(ang), jnp.sin(ang)


def _rot_half_cols(w):
    half = w.shape[-1] // 2
    return jnp.concatenate([-w[..., half:], w[..., :half]], axis=-1)


def _mla_weights(w_in, w_q_b, w_kv_b):
    nh, nope, rope = MLA_HEADS, MLA_NOPE_DIM, MLA_ROPE_DIM
    off = MLA_Q_RANK + MLA_KV_RANK
    w_in_ext = jnp.concatenate([w_in, _rot_half_cols(w_in[:, off:])], axis=1)
    rq, rkv = w_q_b.shape[0], w_kv_b.shape[0]
    wq = w_q_b.reshape(rq, nh, nope + rope)
    pad = jnp.zeros((rq, nh, LANES - nope - rope), F32)
    wq_pad = jnp.concatenate([wq, pad], axis=-1).reshape(rq, nh * LANES)
    wq_rot = jnp.concatenate([jnp.zeros((rq, nh, nope), F32), _rot_half_cols(wq[..., nope:]), pad],
                             axis=-1).reshape(rq, nh * LANES)
    wkv = w_kv_b.reshape(rkv, nh, nope + MLA_V_DIM)
    wk_pad = jnp.concatenate([wkv[..., :nope], jnp.zeros((rkv, nh, LANES - nope), F32)],
                             axis=-1).reshape(rkv, nh * LANES)
    wv = wkv[..., nope:].reshape(rkv, nh * MLA_V_DIM)
    place = np.zeros((rope, nh, LANES), np.float32)
    place[np.arange(rope), :, nope + np.arange(rope)] = 1.0
    place = jnp.asarray(place.reshape(rope, nh * LANES))
    return (w_in_ext,) + tuple(a.astype(BF16) for a in (wq_pad, wq_rot, wk_pad, wv, place))


def _mla_rope_tables(length, rope_on):
    if rope_on:
        cos, sin = _axial_rope(length)
    else:
        cos = jnp.ones((length, MLA_ROPE_DIM // 2), F32)
        sin = jnp.zeros((length, MLA_ROPE_DIM // 2), F32)
    ck = jnp.concatenate([cos, cos], axis=-1)
    sk = jnp.concatenate([sin, sin], axis=-1)
    one = jnp.ones((length, MLA_NOPE_DIM), F32)
    tail = LANES - MLA_NOPE_DIM - MLA_ROPE_DIM
    cq = jnp.concatenate([one, ck, one[:, :tail]], axis=-1)
    sq = jnp.concatenate([0.0 * one, sk, 0.0 * one[:, :tail]], axis=-1)
    return cq, sq, ck, sk


def _mla_qkv_kernel(z_ref, gq_ref, gkv_ref, wq_ref, wqr_ref, wk_ref, wv_ref, place_ref,
                    cq_ref, sq_ref, ck_ref, sk_ref, q_ref, k_ref, v_ref):
    z = z_ref[0]
    off = MLA_Q_RANK + MLA_KV_RANK
    rope = MLA_ROPE_DIM

    def norm(a, g_ref):
        ms = jnp.mean(a * a, axis=-1, keepdims=True)
        return (a * lax.rsqrt(ms + NORM_EPS) * g_ref[...]).astype(BF16)

    cq = norm(z[:, :MLA_Q_RANK], gq_ref)
    ckv = norm(z[:, MLA_Q_RANK:off], gkv_ref)
    k_rope = (z[:, off:off + rope] * ck_ref[...] + z[:, off + rope:off + 2 * rope] * sk_ref[...]).astype(BF16)
    qa = jnp.dot(cq, wq_ref[...], preferred_element_type=F32)
    qb = jnp.dot(cq, wqr_ref[...], preferred_element_type=F32)
    cos_q = cq_ref[...]
    sin_q = sq_ref[...]
    for h in range(MLA_HEADS):
        cols = slice(h * LANES, (h + 1) * LANES)
        q_ref[0, :, cols] = (qa[:, cols] * cos_q + qb[:, cols] * sin_q).astype(BF16)
    k_ref[0] = (jnp.dot(ckv, wk_ref[...], preferred_element_type=F32)
                + jnp.dot(k_rope, place_ref[...], preferred_element_type=F32)).astype(BF16)
    v_ref[0] = jnp.dot(ckv, wv_ref[...], preferred_element_type=F32).astype(BF16)


def _mla_qkv(z, q_norm_g, kv_norm_g, weights, tables):
    bsz, length, zw = z.shape
    wq, wqr, wk, wv, place = weights
    nh = MLA_HEADS
    tm = _pick(length, (512, 256))
    full = lambda a: pl.BlockSpec(a.shape, lambda b, i: (0, 0))
    rows = lambda a: pl.BlockSpec((tm, a.shape[1]), lambda b, i: (i, 0))
    gq = q_norm_g.reshape(1, -1)
    gkv = kv_norm_g.reshape(1, -1)
    out = lambda w: pl.BlockSpec((1, tm, w), lambda b, i: (b, i, 0))
    return pl.pallas_call(
        _mla_qkv_kernel,
        grid=(bsz, length // tm),
        in_specs=[pl.BlockSpec((1, tm, zw), lambda b, i: (b, i, 0)), full(gq), full(gkv),
                  full(wq), full(wqr), full(wk), full(wv), full(place)] + [rows(t) for t in tables],
        out_specs=[out(nh * LANES), out(nh * LANES), out(nh * MLA_V_DIM)],
        out_shape=[jax.ShapeDtypeStruct((bsz, length, nh * LANES), BF16),
                   jax.ShapeDtypeStruct((bsz, length, nh * LANES), BF16),
                   jax.ShapeDtypeStruct((bsz, length, nh * MLA_V_DIM), BF16)],
        compiler_params=_cparams("parallel", "parallel"),
        name="mla_qkv",
    )(z, gq, gkv, wq, wqr, wk, wv, place, *tables)


def _mla_mixer(x, ctx, g, sh, sc, csh, csc, w_in, q_norm_g, w_q_b, kv_norm_g, w_kv_b, w_o, gate, cgate):
    nh = MLA_HEADS
    scale = (MLA_NOPE_DIM + MLA_ROPE_DIM) ** -0.5
    w_in_ext, *weights = _mla_weights(w_in, w_q_b, w_kv_b)

    def project(u, shift, scl, rope_on):
        z = _proj(u, w_in_ext, g, shift, scl)
        cos_q, sin_q, cos_k, sin_k = _mla_rope_tables(u.shape[1], rope_on)
        qs = scale * LOG2_E
        return _mla_qkv(z, q_norm_g, kv_norm_g, weights, (cos_q * qs, sin_q * qs, cos_k, sin_k))

    ql, kl, vl = project(x, sh, sc, True)
    qc, kc, vc = project(ctx, csh, csc, False)
    y = _pair_attention((ql, 0), [(kc, 0, vc, 0), (kl, 0, vl, 0)], LANES, nh)
    if cgate is None:
        return _proj_res(y, w_o, x, gate), None
    yc = _pair_attention((qc, 0), [(kc, 0, vc, 0)], LANES, nh)
    return _proj_res(y, w_o, x, gate), _proj_res(yc, w_o, ctx, cgate)


def _conv3(z, w_ref, b_ref):
    length = z.shape[0]
    t = lax.broadcasted_iota(jnp.int32, z.shape, 0)
    prev = jnp.where(t == 0, 0.0, pltpu.roll(z, 1, axis=0))
    nxt = jnp.where(t == length - 1, 0.0, pltpu.roll(z, length - 1, axis=0))
    return b_ref[...] + prev * w_ref[0:1, :] + z * w_ref[1:2, :] + nxt * w_ref[2:3, :]


def _hy_conv_kernel(z0_ref, z1_ref, z2_ref, w0_ref, w1_ref, w2_ref, b0_ref, b1_ref, b2_ref,
                    x0_ref, gg_ref):
    x0_ref[0] = _conv3(z0_ref[0], w0_ref, b0_ref)
    x1 = _conv3(z1_ref[0], w1_ref, b1_ref)
    v = _conv3(z2_ref[0], w2_ref, b2_ref)
    gg_ref[0] = v * x1


def _hy_conv(z, conv_w, conv_b, d):
    bsz, length, _ = z.shape
    tc = 256
    nb = d // tc
    zspec = lambda k: pl.BlockSpec((1, length, tc), lambda b, j: (b, 0, k * nb + j))
    wspec = lambda k: pl.BlockSpec((3, tc), lambda b, j: (0, k * nb + j))
    bspec = lambda k: pl.BlockSpec((1, tc), lambda b, j: (0, k * nb + j))
    ospec = pl.BlockSpec((1, length, tc), lambda b, j: (b, 0, j))
    cb = conv_b.reshape(1, 3 * d)
    return pl.pallas_call(
        _hy_conv_kernel,
        grid=(bsz, nb),
        in_specs=[zspec(0), zspec(1), zspec(2), wspec(0), wspec(1), wspec(2), bspec(0), bspec(1), bspec(2)],
        out_specs=[ospec, ospec],
        out_shape=[jax.ShapeDtypeStruct((bsz, length, d), F32)] * 2,
        compiler_params=_cparams("parallel", "parallel"),
        name="hy_conv",
    )(z, z, z, conv_w, conv_w, conv_w, cb, cb, cb)


def _dft_tables(length):
    n2 = 2 * length
    lo = GRID_W
    w = 2.0 * math.pi / n2
    f = jnp.arange(length, dtype=jnp.int32)[:, None]
    n_hi = jnp.arange(length // lo, dtype=jnp.int32)[None, :] * lo
    n_lo = jnp.arange(lo, dtype=jnp.int32)[None, :]
    ang_hi = ((f * n_hi) % n2).astype(F32) * w
    ang_lo = ((f * n_lo) % n2).astype(F32) * w
    c_hi, s_hi = jnp.cos(ang_hi)[:, :, None], jnp.sin(ang_hi)[:, :, None]
    c_lo, s_lo = jnp.cos(ang_lo)[:, None, :], jnp.sin(ang_lo)[:, None, :]
    cf = (c_hi * c_lo - s_hi * s_lo).reshape(length, length)
    sn = (s_hi * c_lo + c_hi * s_lo).reshape(length, length)
    n = jnp.arange(length, dtype=jnp.int32)[None, :]
    sgn_n = jnp.where(n % 2 == 0, 1.0, -1.0).astype(F32)
    sf = jnp.where(f == 0, sgn_n, -sn)
    return cf.astype(BF16), sf.astype(BF16), sf.T.astype(BF16)


def _hy_fwd_kernel(cf_ref, sf_ref, gg_ref, kre_ref, kim_ref, y_ref, gb_ref):
    j = pl.program_id(1)

    @pl.when(j == 0)
    def _():
        gb_ref[...] = gg_ref[0].astype(BF16)

    ure = jnp.dot(cf_ref[...], gb_ref[...], preferred_element_type=F32)
    uim = jnp.dot(sf_ref[...], gb_ref[...], preferred_element_type=F32)
    kre = kre_ref[...]
    kim = kim_ref[...]
    tf = ure.shape[0]
    row = lax.broadcasted_iota(jnp.int32, ure.shape, 0) + j * tf
    packed = row == 0
    yre = ure * kre - jnp.where(packed, 0.0, uim * kim)
    yim = uim * jnp.where(packed, kim, kre) + jnp.where(packed, 0.0, ure * kim)
    y_ref[0, 0] = yre.astype(BF16)
    y_ref[0, 1] = yim.astype(BF16)


def _hy_inv_kernel(ci_ref, si_ref, y_ref, gg_ref, x0_ref, skip_ref, o_ref):
    y = (jnp.dot(ci_ref[...], y_ref[0, 0], preferred_element_type=F32)
         + jnp.dot(si_ref[...], y_ref[0, 1], preferred_element_type=F32))
    o_ref[0] = ((y + gg_ref[0] * skip_ref[...]) * x0_ref[0]).astype(o_ref.dtype)


def _hy_long_conv_gate(gg, x0, kre, kim, skip, tables):
    bsz, length, d = gg.shape
    cf, sf, sft = tables
    tf = _pick(length, (512, 256))
    y = pl.pallas_call(
        _hy_fwd_kernel,
        grid=(bsz, length // tf),
        in_specs=[
            pl.BlockSpec((tf, length), lambda b, j: (j, 0)),
            pl.BlockSpec((tf, length), lambda b, j: (j, 0)),
            pl.BlockSpec((1, length, d), lambda b, j: (b, 0, 0)),
            pl.BlockSpec((tf, d), lambda b, j: (j, 0)),
            pl.BlockSpec((tf, d), lambda b, j: (j, 0)),
        ],
        out_specs=pl.BlockSpec((1, 2, tf, d), lambda b, j: (b, 0, j, 0)),
        out_shape=jax.ShapeDtypeStruct((bsz, 2, length, d), BF16),
        scratch_shapes=[pltpu.VMEM((length, d), BF16)],
        compiler_params=_cparams("parallel", "arbitrary"),
        name="hy_dft_fwd",
    )(cf, sf, gg, kre, kim)
    tt = _pick(length, (512, 256))
    return pl.pallas_call(
        _hy_inv_kernel,
        grid=(bsz, length // tt),
        in_specs=[
            pl.BlockSpec((tt, length), lambda b, i: (i, 0)),
            pl.BlockSpec((tt, length), lambda b, i: (i, 0)),
            pl.BlockSpec((1, 2, length, d), lambda b, i: (b, 0, 0, 0)),
            pl.BlockSpec((1, tt, d), lambda b, i: (b, i, 0)),
            pl.BlockSpec((1, tt, d), lambda b, i: (b, i, 0)),
            pl.BlockSpec((1, d), lambda b, i: (0, 0)),
        ],
        out_specs=pl.BlockSpec((1, tt, d), lambda b, i: (b, i, 0)),
        out_shape=jax.ShapeDtypeStruct((bsz, length, d), BF16),
        compiler_params=_cparams("parallel", "arbitrary"),
        name="hy_dft_inv",
    )(cf, sft, y, gg, x0, skip.reshape(1, d))


def _hyena_filter(length, w1, b1, w2, b2, w3, b3, sin_freq):
    t = jnp.linspace(0.0, 1.0, length, dtype=F32)[:, None]
    bands = (HY_EMB_DIM - 1) // 2
    w = (2.0 * math.pi / length) * jnp.arange(length, dtype=F32)[:, None]
    f = jnp.linspace(1e-4, bands - 1, bands, dtype=F32)[None, :]
    z = jnp.concatenate([t, jnp.cos(f * w), -jnp.sin(f * w)], axis=-1)
    hp = lax.Precision.HIGHEST
    h = jnp.sin(sin_freq[0] * (jnp.dot(z, w1, precision=hp) + b1))
    h = jnp.sin(sin_freq[1] * (jnp.dot(h, w2, precision=hp) + b2))
    h = jnp.dot(h, w3, precision=hp) + b3
    max_decay = math.log(HY_DECAY_TARGET) / HY_FAST_DECAY
    min_decay = math.log(HY_DECAY_TARGET) / HY_SLOW_DECAY
    deltas = jnp.abs(jnp.linspace(min_decay, max_decay, h.shape[-1] // 2, dtype=F32))
    deltas = jnp.tile(deltas, 2)
    return h * (jnp.exp(-t * deltas) + HY_MOD_SHIFT)


def _hy_filter_spectrum(filt, d, tables):
    length = filt.shape[0]
    cf, sf, _ = tables
    row0 = (jnp.arange(length) == 0)[:, None]
    hf = filt[:, :d]
    hb0 = jnp.where(row0, 0.0, filt[:, d:])
    tab = jnp.concatenate([cf, sf], axis=0)[None]
    spec = _proj(tab, jnp.concatenate([hf, hb0], axis=1))[0]
    wgt = jnp.where(row0, 1.0, 2.0).astype(F32) / (2 * length)
    kre = (spec[:length, :d] + spec[:length, d:]) * wgt
    kim = (spec[length:, :d] + jnp.where(row0, 1.0, -1.0) * spec[length:, d:]) * wgt
    return kre, kim


def _hyena_mixer(x, ctx, g, sh, sc, csh, csc, w_in, conv_w, conv_b, f_w1, f_b1, f_w2, f_b2, f_w3, f_b3,
                 sin_freq, skip, w_o, gate, cgate):
    d = x.shape[-1]

    def one(u, shift, scale, res, gt):
        length = u.shape[1]
        z = _proj(u, w_in, g, shift, scale)
        x0, gg = _hy_conv(z, conv_w, conv_b, d)
        tables = _dft_tables(length)
        filt = _hyena_filter(length, f_w1, f_b1, f_w2, f_b2, f_w3, f_b3, sin_freq)
        kre, kim = _hy_filter_spectrum(filt, d, tables)
        y = _hy_long_conv_gate(gg, x0, kre, kim, skip, tables)
        return _proj_res(y, w_o, res, gt)

    return one(x, sh, sc, x, gate), (one(ctx, csh, csc, ctx, cgate) if cgate is not None else None)


def _ml_conv_kernel(z_ref, w_ref, b_ref, o_ref, *t_refs, out_scale):
    y = _conv3(z_ref[0], w_ref, b_ref)
    y = y * jax.nn.sigmoid(y) * out_scale
    o_ref[0] = y
    for t_ref in t_refs:
        t_ref[0] = y.T


def _ml_conv(z, conv_w, conv_b, col0, width, out_scale, transposed):
    bsz, length, _ = z.shape
    tc = 256
    j0 = col0 // tc
    out_specs = [pl.BlockSpec((1, length, tc), lambda b, j: (b, 0, j))]
    out_shape = [jax.ShapeDtypeStruct((bsz, length, width), F32)]
    if transposed:
        out_specs.append(pl.BlockSpec((1, tc, length), lambda b, j: (b, j, 0)))
        out_shape.append(jax.ShapeDtypeStruct((bsz, width, length), F32))
    return pl.pallas_call(
        functools.partial(_ml_conv_kernel, out_scale=out_scale),
        grid=(bsz, width // tc),
        in_specs=[pl.BlockSpec((1, length, tc), lambda b, j: (b, 0, j0 + j)),
                  pl.BlockSpec((3, tc), lambda b, j: (0, j0 + j)),
                  pl.BlockSpec((1, tc), lambda b, j: (0, j0 + j))],
        out_specs=out_specs,
        out_shape=out_shape,
        compiler_params=_cparams("parallel", "parallel"),
        name="ml_conv",
    )(z, conv_w, conv_b.reshape(1, -1))


def _ml_gates_kernel(x_ref, g_ref, sh_ref, sc_ref, w_ref, b_ref, o_ref, t_ref):
    x = x_ref[0]
    ms = jnp.mean(x * x, axis=-1, keepdims=True)
    h = x * lax.rsqrt(ms + NORM_EPS) * g_ref[...]
    h = h * (1.0 + sc_ref[0]) + sh_ref[0]
    gates = jnp.dot(h.astype(BF16), w_ref[...], preferred_element_type=F32) + b_ref[...]
    o_ref[0] = gates
    t_ref[0] = gates.T


def _ml_gates(x, g, shift, scale, w_gate, gate_b):
    bsz, length, d = x.shape
    pad = LANES - w_gate.shape[1]
    w = jnp.pad(w_gate, ((0, 0), (0, pad))).astype(BF16)
    b = jnp.pad(gate_b, (0, pad)).reshape(1, LANES)
    tm = _pick(length, (1024, 256))
    return pl.pallas_call(
        _ml_gates_kernel,
        grid=(bsz, length // tm),
        in_specs=[
            pl.BlockSpec((1, tm, d), lambda b_, i: (b_, i, 0)),
            pl.BlockSpec((1, d), lambda b_, i: (0, 0)),
            pl.BlockSpec((1, 1, d), lambda b_, i: (b_, 0, 0)),
            pl.BlockSpec((1, 1, d), lambda b_, i: (b_, 0, 0)),
            pl.BlockSpec((d, LANES), lambda b_, i: (0, 0)),
            pl.BlockSpec((1, LANES), lambda b_, i: (0, 0)),
        ],
        out_specs=[pl.BlockSpec((1, tm, LANES), lambda b_, i: (b_, i, 0)),
                   pl.BlockSpec((1, LANES, tm), lambda b_, i: (b_, 0, i))],
        out_shape=[jax.ShapeDtypeStruct((bsz, length, LANES), F32),
                   jax.ShapeDtypeStruct((bsz, LANES, length), F32)],
        compiler_params=_cparams("parallel", "parallel"),
        name="ml_gates",
    )(x, g.reshape(1, d), shift.reshape(bsz, 1, d), scale.reshape(bsz, 1, d), w, b)


def _log_sigmoid(x):
    return jnp.minimum(x, 0.0) - jnp.log1p(jnp.exp(-jnp.abs(x)))


def _ml_chunk_pair(q2, kt2, k2, v2, gates, i_rows, f_rows, f_lanes, state, reverse):
    ct2, n2, ms = state
    tlen = q2.shape[0]
    dk, dv = ML_QK_DIM, ML_V_DIM
    ti = lax.broadcasted_iota(jnp.int32, (tlen, tlen), 0)
    si = lax.broadcasted_iota(jnp.int32, (tlen, tlen), 1)
    causal = (si >= ti) if reverse else (si <= ti)
    causal_t = (ti >= si) if reverse else (ti <= si)
    lane = lax.broadcasted_iota(jnp.int32, (1, LANES), 1)
    sub = lax.broadcasted_iota(jnp.int32, (LANES, 1), 0)
    kt_b = kt2.astype(BF16)
    k_b = k2.astype(BF16)
    v_b = v2.astype(BF16)
    ct_b = ct2.astype(BF16)
    hs, ws, decays, m_news = [], [], [], []
    for j in range(2):
        own = (lane >= dk) if j else (lane < dk)
        lf_row = _log_sigmoid(f_rows[j])
        f_col = jnp.where(lane == f_lanes[j], gates, 0.0).sum(axis=-1, keepdims=True)
        lf_col = _log_sigmoid(f_col)
        b_col = jnp.where(causal, lf_row, 0.0).sum(axis=-1, keepdims=True)
        b_row = jnp.where(causal_t, lf_col, 0.0).sum(axis=0, keepdims=True)
        log_d = jnp.where(causal, b_col - b_row + i_rows[j], -jnp.inf)
        m_inter = b_col + ms[j]
        m_t = jnp.maximum(log_d.max(axis=-1, keepdims=True), m_inter)
        qj = jnp.where(own, q2, 0.0)
        qj_b = qj.astype(BF16)
        s = jnp.dot(qj_b, kt_b, preferred_element_type=F32) * jnp.exp(log_d - m_t)
        inter = jnp.exp(m_inter - m_t)
        sv = jnp.dot(s.astype(BF16), v_b, preferred_element_type=F32)[:, j * dv:(j + 1) * dv]
        num = sv + inter * jnp.dot(qj_b, ct_b, preferred_element_type=F32)
        folded = inter * (qj * n2)
        for c in range(0, tlen, LANES):
            folded = folded + s[:, c:c + LANES]
        qn = folded.sum(axis=-1, keepdims=True)
        hs.append(num / jnp.maximum(jnp.abs(qn), jnp.exp(-m_t)))
        b_end = b_row[:, 0:1] if reverse else b_row[:, tlen - 1:tlen]
        w_log = b_end - b_row + i_rows[j]
        m_new = jnp.maximum(b_end + ms[j], w_log.max(axis=-1, keepdims=True))
        ws.append(jnp.exp(w_log - m_new))
        decays.append(jnp.exp(b_end + ms[j] - m_new))
        m_news.append(m_new)
    head0_rows = sub < dk
    head0_lanes = lane < dk
    kw = (kt2 * jnp.where(head0_rows, ws[0], ws[1])).astype(BF16)
    upd = jnp.dot(kw, v_b, preferred_element_type=F32)
    ct_new = (jnp.where(head0_rows, decays[0], decays[1]) * ct2
              + jnp.where(head0_rows, upd[:, :dv], upd[:, dv:]))
    wk = [jnp.dot(w.astype(BF16), k_b, preferred_element_type=F32) for w in ws]
    n_new = jnp.where(head0_lanes, decays[0], decays[1]) * n2 + jnp.where(head0_lanes, wk[0], wk[1])
    return jnp.concatenate(hs, axis=-1), (ct_new, n_new, tuple(m_news))


def _mlstm_kernel(ql_ref, kl_ref, ktl_ref, vl_ref, ol_ref, gl_ref, gtl_ref,
                  qc_ref, kc_ref, ktc_ref, vc_ref, oc_ref, gc_ref, gtc_ref, ng_ref,
                  yl_ref, yc_ref, hfl_ref, hbl_ref, hfc_ref, hbc_ref):
    tlen = ML_CHUNK
    dv, nh = ML_V_DIM, ML_HEADS
    n_lat = ql_ref.shape[1] // tlen
    n_ctx = qc_ref.shape[1] // tlen
    head0 = 2 * pl.program_id(1)

    def run(refs, c, state, direction):
        q_ref, k_ref, kt_ref, v_ref, g_ref, gt_ref, hf_ref, hb_ref = refs
        rows = pl.ds(pl.multiple_of(c * tlen, tlen), tlen)
        gate_i, gate_f = 2 * direction * nh, (2 * direction + 1) * nh
        i_rows = [gt_ref[0, pl.ds(gate_i + head0 + j, 1), rows] for j in range(2)]
        f_rows = [gt_ref[0, pl.ds(gate_f + head0 + j, 1), rows] for j in range(2)]
        f_lanes = [gate_f + head0 + j for j in range(2)]
        h, state = _ml_chunk_pair(q_ref[0, rows, :], kt_ref[0, :, rows], k_ref[0, rows, :], v_ref[0, rows, :],
                                  g_ref[0, rows, :], i_rows, f_rows, f_lanes, state, direction == 1)
        (hb_ref if direction else hf_ref)[rows, :] = h
        return state

    def sweep(refs, n_chunks, states):
        def body(c, sts):
            return run(refs, c, sts[0], 0), run(refs, n_chunks - 1 - c, sts[1], 1)
        return lax.fori_loop(0, n_chunks, body, states)

    zero = (jnp.zeros((LANES, dv), F32), jnp.zeros((1, LANES), F32),
            (jnp.zeros((1, 1), F32), jnp.zeros((1, 1), F32)))
    lat = (ql_ref, kl_ref, ktl_ref, vl_ref, gl_ref, gtl_ref, hfl_ref, hbl_ref)
    ctx = (qc_ref, kc_ref, ktc_ref, vc_ref, gc_ref, gtc_ref, hfc_ref, hbc_ref)
    sweep(lat, n_lat, sweep(ctx, n_ctx, (zero, zero)))

    def finish(hf_ref, hb_ref, o_ref, y_ref):
        for j in range(2):
            vs = slice(j * dv, (j + 1) * dv)
            h = hf_ref[:, vs] + hb_ref[:, vs]
            h = h * lax.rsqrt(jnp.mean(h * h, axis=-1, keepdims=True) + NORM_EPS) * ng_ref[:, vs]
            y_ref[0, :, vs] = (h * jax.nn.sigmoid(o_ref[0, :, vs])).astype(y_ref.dtype)

    finish(hfl_ref, hbl_ref, ol_ref, yl_ref)
    finish(hfc_ref, hbc_ref, oc_ref, yc_ref)


def _mlstm_mixer(x, ctx, g, sh, sc, csh, csc, w_in, conv_w, conv_b, gate_b, out_norm_g, w_o, gate, cgate):
    nh, dk, dv = ML_HEADS, ML_QK_DIM, ML_V_DIM
    nqk = 2 * nh * dk
    wide = nqk + 2 * nh * dv
    w_main = w_in[:, :wide].astype(BF16)
    w_gate = w_in[:, wide:]

    def project(u, shift, scale):
        z = _proj(u, w_main, g, shift, scale)
        gates, gates_t = _ml_gates(u, g, shift, scale, w_gate, gate_b)
        (q,) = _ml_conv(z, conv_w, conv_b, 0, nh * dk, 1.0, False)
        k, kt = _ml_conv(z, conv_w, conv_b, nh * dk, nh * dk, dk ** -0.5, True)
        return z, q, k, kt, gates, gates_t

    zl, ql, kl, ktl, gl, gtl = project(x, sh, sc)
    zc, qc, kc, ktc, gc, gtc = project(ctx, csh, csc)
    bsz, length, d = x.shape
    lc = ctx.shape[1]
    npair = nh // 2
    qw, vw = 2 * dk, 2 * dv
    v_off, o_off = nqk // vw, (nqk + nh * dv) // vw

    def specs(n):
        return [
            pl.BlockSpec((1, n, qw), lambda b, h: (b, 0, h)),
            pl.BlockSpec((1, n, qw), lambda b, h: (b, 0, h)),
            pl.BlockSpec((1, qw, n), lambda b, h: (b, h, 0)),
            pl.BlockSpec((1, n, vw), lambda b, h: (b, 0, v_off + h)),
            pl.BlockSpec((1, n, vw), lambda b, h: (b, 0, o_off + h)),
            pl.BlockSpec((1, n, LANES), lambda b, h: (b, 0, 0)),
            pl.BlockSpec((1, LANES, n), lambda b, h: (b, 0, 0)),
        ]

    yl, yc = pl.pallas_call(
        _mlstm_kernel,
        grid=(bsz, npair),
        in_specs=specs(length) + specs(lc) + [pl.BlockSpec((1, vw), lambda b, h: (0, h))],
        out_specs=[pl.BlockSpec((1, length, vw), lambda b, h: (b, 0, h)),
                   pl.BlockSpec((1, lc, vw), lambda b, h: (b, 0, h))],
        out_shape=[jax.ShapeDtypeStruct((bsz, length, nh * dv), BF16),
                   jax.ShapeDtypeStruct((bsz, lc, nh * dv), BF16)],
        scratch_shapes=[pltpu.VMEM((length, vw), F32), pltpu.VMEM((length, vw), F32),
                        pltpu.VMEM((lc, vw), F32), pltpu.VMEM((lc, vw), F32)],
        compiler_params=_cparams("parallel", "parallel"),
        name="mlstm",
    )(ql, kl, ktl, zl, zl, gl, gtl, qc, kc, ktc, zc, zc, gc, gtc, out_norm_g.reshape(1, nh * dv))
    return _proj_res(yl, w_o, x, gate), (_proj_res(yc, w_o, ctx, cgate) if cgate is not None else None)


def _router_kernel(x_ref, g_ref, sh_ref, sc_ref, rw_ref, h_ref, aff_ref):
    x = x_ref[0]
    ms = jnp.mean(x * x, axis=-1, keepdims=True)
    h = x * lax.rsqrt(ms + NORM_EPS) * g_ref[...]
    h = h * (1.0 + sc_ref[0]) + sh_ref[0]
    hb = h.astype(BF16)
    h_ref[0] = hb
    ne = aff_ref.shape[1]
    logits = jnp.dot(hb, rw_ref[...], preferred_element_type=F32).T[:ne]
    mx = logits.max(axis=0, keepdims=True)
    p = jnp.exp(logits - mx)
    aff_ref[0] = p / p.sum(axis=0, keepdims=True)


def _router(x, g, shift, scale, router_w):
    bsz, length, d = x.shape
    ne = router_w.shape[1]
    tm = _pick(length, (1024, 256))
    return pl.pallas_call(
        _router_kernel,
        grid=(bsz, length // tm),
        in_specs=[
            pl.BlockSpec((1, tm, d), lambda b, i: (b, i, 0)),
            pl.BlockSpec((1, d), lambda b, i: (0, 0)),
            pl.BlockSpec((1, 1, d), lambda b, i: (b, 0, 0)),
            pl.BlockSpec((1, 1, d), lambda b, i: (b, 0, 0)),
            pl.BlockSpec((d, LANES), lambda b, i: (0, 0)),
        ],
        out_specs=[pl.BlockSpec((1, tm, d), lambda b, i: (b, i, 0)),
                   pl.BlockSpec((1, ne, tm), lambda b, i: (b, 0, i))],
        out_shape=[jax.ShapeDtypeStruct((bsz, length, d), BF16),
                   jax.ShapeDtypeStruct((bsz, ne, length), F32)],
        compiler_params=_cparams("parallel", "parallel"),
        name="moe_router",
    )(x, g.reshape(1, d), shift.reshape(bsz, 1, d), scale.reshape(bsz, 1, d),
      jnp.pad(router_w, ((0, 0), (0, LANES - ne))).astype(BF16))


def _excl_cumsum_lanes(flags):
    rows, length = flags.shape
    cw = min(length, 256)
    si = lax.broadcasted_iota(jnp.int32, (cw, cw), 0)
    ti = lax.broadcasted_iota(jnp.int32, (cw, cw), 1)
    upper = jnp.where(si < ti, 1.0, 0.0).astype(BF16)
    carry = jnp.zeros((rows, 1), F32)
    parts = []
    for c in range(length // cw):
        blk = flags[:, c * cw:(c + 1) * cw]
        parts.append(jnp.dot(blk.astype(BF16), upper, preferred_element_type=F32) + carry)
        carry = carry + blk.sum(axis=-1, keepdims=True)
    return parts[0] if len(parts) == 1 else jnp.concatenate(parts, axis=-1)


def _select_kernel(aff_ref, pos_ref, *, cap):
    a = aff_ref[...]
    rows = a.shape[0]
    capf = float(cap)

    def bisect(_, lohi):
        lo, hi = lohi
        mid = 0.5 * (lo + hi)
        cnt = jnp.where(a >= mid, 1.0, 0.0).sum(axis=-1, keepdims=True)
        ge = cnt >= capf
        return jnp.where(ge, mid, lo), jnp.where(ge, hi, mid)

    lo, hi = lax.fori_loop(0, SELECT_BISECTIONS, bisect,
                           (jnp.zeros((rows, 1), F32), jnp.full((rows, 1), 2.0, F32)))
    above = a >= hi
    tie = jnp.logical_and(a >= lo, jnp.logical_not(above))
    n_above = jnp.where(above, 1.0, 0.0).sum(axis=-1, keepdims=True)
    tie_rank = _excl_cumsum_lanes(jnp.where(tie, 1.0, 0.0))
    sel = jnp.logical_or(above, jnp.logical_and(tie, tie_rank < capf - n_above))
    slot = _excl_cumsum_lanes(jnp.where(sel, 1.0, 0.0))
    pos_ref[...] = jnp.where(sel, slot, -1.0).astype(jnp.int32)


def _select(aff2d, cap):
    return pl.pallas_call(
        functools.partial(_select_kernel, cap=cap),
        out_shape=jax.ShapeDtypeStruct(aff2d.shape, jnp.int32),
        compiler_params=pltpu.CompilerParams(vmem_limit_bytes=V7X_VMEM_LIMIT_BYTES),
        name="moe_select",
    )(aff2d)


GATHER_EXPERTS_PER_STEP = 4


def _gather_kernel(h_ref, pos_ref, aff_ref, xg_ref, gs_ref, *, cap):
    length = pos_ref.shape[3]
    slot = lax.broadcasted_iota(jnp.int32, (cap, length), 0)
    hits = [slot == pos_ref[0, k] for k in range(GATHER_EXPERTS_PER_STEP)]
    onehot = jnp.concatenate([jnp.where(hit, 1.0, 0.0).astype(BF16) for hit in hits], axis=0)
    xg = jnp.dot(onehot, h_ref[0], preferred_element_type=F32).astype(BF16)
    for k, hit in enumerate(hits):
        xg_ref[k] = xg[k * cap:(k + 1) * cap]
        gs_ref[k] = jnp.where(hit, aff_ref[0, k], 0.0).sum(axis=-1, keepdims=True)


def _gather(h, pos, aff, cap):
    bsz, length, d = h.shape
    ne = pos.shape[1]
    per = GATHER_EXPERTS_PER_STEP
    return pl.pallas_call(
        functools.partial(_gather_kernel, cap=cap),
        grid=(bsz, ne // per),
        in_specs=[
            pl.BlockSpec((1, length, d), lambda b, e: (b, 0, 0)),
            pl.BlockSpec((1, per, 1, length), lambda b, e: (b, e, 0, 0)),
            pl.BlockSpec((1, per, 1, length), lambda b, e: (b, e, 0, 0)),
        ],
        out_specs=[pl.BlockSpec((per, cap, d), lambda b, e: (e, b, 0)),
                   pl.BlockSpec((per, cap, 1), lambda b, e: (e, b, 0))],
        out_shape=[jax.ShapeDtypeStruct((ne, bsz * cap, d), BF16),
                   jax.ShapeDtypeStruct((ne, bsz * cap, 1), F32)],
        compiler_params=_cparams("parallel", "arbitrary"),
        name="moe_gather",
    )(h, pos.reshape(bsz, ne, 1, length), aff.reshape(bsz, ne, 1, length))


def _ffn_kernel(*refs, n_groups, n_up, tf):
    xg_refs = refs[:n_groups]
    gs_refs = refs[n_groups:2 * n_groups]
    wg_ref, wu_ref, wd_ref = refs[2 * n_groups:2 * n_groups + 3]
    y_refs = refs[2 * n_groups + 3:3 * n_groups + 3]
    z_refs = refs[3 * n_groups + 3:]
    step = pl.program_id(1)

    @pl.when(step < n_up)
    def _():
        wg = wg_ref[0, 0].astype(BF16)
        wu = wu_ref[0, 0].astype(BF16)
        cols = pl.ds(pl.multiple_of(step * tf, tf), tf)
        for xg_ref, z_ref in zip(xg_refs, z_refs):
            xg = xg_ref[0]
            a = jnp.dot(xg, wg, preferred_element_type=F32)
            u = jnp.dot(xg, wu, preferred_element_type=F32)
            z_ref[:, cols] = (a * jax.nn.sigmoid(a) * u).astype(BF16)

    @pl.when(step >= n_up)
    def _():
        wd = wd_ref[0, 0].astype(BF16)
        for gs_ref, y_ref, z_ref in zip(gs_refs, y_refs, z_refs):
            y = jnp.dot(z_ref[...], wd, preferred_element_type=F32)
            y_ref[0] = (y * gs_ref[0]).astype(BF16)


def _expert_ffn(xgs, gss, w_gate, w_up, w_down, layer):
    ne, _, d = xgs[0].shape
    ff = w_gate.shape[3]
    tf = 256
    tn = 256
    n_up, n_down = ff // tf, d // tn
    n = len(xgs)
    tok = lambda a: pl.BlockSpec((1,) + a.shape[1:], lambda e, s: (e, 0, 0))
    up_tile = lambda e, s: (layer, e, 0, jnp.minimum(s, n_up - 1))
    down_tile = lambda e, s: (layer, e, 0, jnp.maximum(s - n_up, 0))
    return pl.pallas_call(
        functools.partial(_ffn_kernel, n_groups=n, n_up=n_up, tf=tf),
        grid=(ne, n_up + n_down),
        in_specs=[tok(a) for a in xgs] + [tok(a) for a in gss] + [
            pl.BlockSpec((1, 1, d, tf), up_tile),
            pl.BlockSpec((1, 1, d, tf), up_tile),
            pl.BlockSpec((1, 1, ff, tn), down_tile),
        ],
        out_specs=[pl.BlockSpec((1, a.shape[1], tn), lambda e, s: (e, 0, jnp.maximum(s - n_up, 0)))
                   for a in xgs],
        out_shape=[jax.ShapeDtypeStruct(a.shape, BF16) for a in xgs],
        scratch_shapes=[pltpu.VMEM((a.shape[1], ff), BF16) for a in xgs],
        compiler_params=_cparams("parallel", "arbitrary"),
        name="moe_ffn",
    )(*xgs, *gss, w_gate, w_up, w_down)


def _combine_kernel(post_ref, y_ref, x_ref, gt_ref, fg_ref, o_ref, *, cap, final_norm):
    post = post_ref[0]
    tl, ne = post.shape
    d = y_ref.shape[2]
    slot = lax.broadcasted_iota(jnp.int32, (tl, cap), 1)
    hits = [jnp.where(post[:, e:e + 1] == slot, 1.0, 0.0).astype(BF16) for e in range(ne)]
    if cap % LANES == 0:
        acc = jnp.dot(jnp.concatenate(hits, axis=1), y_ref[...].reshape(ne * cap, d),
                      preferred_element_type=F32)
    else:
        acc = jnp.zeros((tl, d), F32)
        for e in range(ne):
            acc = acc + jnp.dot(hits[e], y_ref[e], preferred_element_type=F32)
    out = x_ref[0] + gt_ref[0] * acc
    if final_norm:
        ms = jnp.mean(out * out, axis=-1, keepdims=True)
        out = out * lax.rsqrt(ms + NORM_EPS) * fg_ref[...]
    o_ref[0] = out


def _combine(pos_t, y, x, gate, cap, final_g=None):
    bsz, length, d = x.shape
    ne = y.shape[0]
    tl = _pick(length, (512, 256))
    final_norm = final_g is not None
    fg = (final_g if final_norm else jnp.ones((d,), F32)).reshape(1, d)
    return pl.pallas_call(
        functools.partial(_combine_kernel, cap=cap, final_norm=final_norm),
        grid=(bsz, length // tl),
        in_specs=[
            pl.BlockSpec((1, tl, ne), lambda b, i: (b, i, 0)),
            pl.BlockSpec((ne, cap, d), lambda b, i: (0, b, 0)),
            pl.BlockSpec((1, tl, d), lambda b, i: (b, i, 0)),
            pl.BlockSpec((1, 1, d), lambda b, i: (b, 0, 0)),
            pl.BlockSpec((1, d), lambda b, i: (0, 0)),
        ],
        out_specs=pl.BlockSpec((1, tl, d), lambda b, i: (b, i, 0)),
        out_shape=jax.ShapeDtypeStruct(x.shape, F32),
        compiler_params=_cparams("parallel", "arbitrary"),
        name="moe_combine",
    )(pos_t, y, x, gate.reshape(bsz, 1, d), fg)


def _route(x, g, shift, scale, router_w):
    bsz, length, _ = x.shape
    ne = router_w.shape[1]
    cap = max(1, EC_CAPACITY_FACTOR * length // ne)
    h, aff = _router(x, g, shift, scale, router_w)
    pos = _select(aff.reshape(bsz * ne, length), cap).reshape(bsz, ne, length)
    xg, gs = _gather(h, pos, aff, cap)
    return xg, gs, pos.transpose(0, 2, 1), cap


def _moe(x, ctx, g, sh, sc, gate, csh, csc, cgate, router_w, w_gate, w_up, w_down, layer, final_g=None):
    xg, gs, pos_t, cap = _route(x, g, sh, sc, router_w)
    if ctx is None:
        (y,) = _expert_ffn([xg], [gs], w_gate, w_up, w_down, layer)
        return _combine(pos_t, y, x, gate, cap, final_g), None
    xg_c, gs_c, pos_tc, cap_c = _route(ctx, g, csh, csc, router_w)
    y, y_c = _expert_ffn([xg, xg_c], [gs, gs_c], w_gate, w_up, w_down, layer)
    return _combine(pos_t, y, x, gate, cap), _combine(pos_tc, y_c, ctx, cgate, cap_c)


def kernel(x, c, ctx, c_ctx, mod_w, mod_b, norm_mix_g, norm_ffn_g, router_w, moe_w_gate, moe_w_up, moe_w_down, na_w_qkv, na_rpb, na_w_o, mla_w_in, mla_q_norm_g, mla_w_q_b, mla_kv_norm_g, mla_w_kv_b, mla_w_o, hy_w_in, hy_conv_w, hy_conv_b, hy_f_w1, hy_f_b1, hy_f_w2, hy_f_b2, hy_f_w3, hy_f_b3, hy_sin_freq, hy_skip, hy_w_o, ml_w_in, ml_conv_w, ml_conv_b, ml_gate_b, ml_out_norm_g, ml_w_o, final_norm_g):
    bsz, _, d = x.shape
    depth = mod_w.shape[0]
    n_mixers = 4
    cond = jnp.concatenate([c, c_ctx[None, :]], axis=0)
    cond = jnp.pad(cond, ((0, (-cond.shape[0]) % 8), (0, 0)))
    mods = _modulation(cond, mod_w, mod_b)
    for i in range(depth):
        last = i == depth - 1
        mod = mods[i]
        sh1, sc1, g1, sh2, sc2, g2 = [mod[:bsz, k * d:(k + 1) * d] for k in range(6)]
        bc = lambda v: jnp.broadcast_to(v[None, :], (bsz, d))
        csh1, csc1, cg1, csh2, csc2, cg2 = [bc(mod[bsz, k * d:(k + 1) * d]) for k in range(6)]
        if last:
            cg1 = None
        kind, j = i % n_mixers, i // n_mixers
        gm = norm_mix_g[i]
        if kind == 0:
            x, ctx_new = _na_mixer(x, ctx, gm, sh1, sc1, csh1, csc1, na_w_qkv[j], na_rpb[j], na_w_o[j], g1, cg1)
        elif kind == 1:
            x, ctx_new = _mla_mixer(x, ctx, gm, sh1, sc1, csh1, csc1, mla_w_in[j], mla_q_norm_g[j],
                                    mla_w_q_b[j], mla_kv_norm_g[j], mla_w_kv_b[j], mla_w_o[j], g1, cg1)
        elif kind == 2:
            x, ctx_new = _hyena_mixer(x, ctx, gm, sh1, sc1, csh1, csc1, hy_w_in[j], hy_conv_w[j], hy_conv_b[j],
                                      hy_f_w1[j], hy_f_b1[j], hy_f_w2[j], hy_f_b2[j], hy_f_w3[j], hy_f_b3[j],
                                      hy_sin_freq[j], hy_skip[j], hy_w_o[j], g1, cg1)
        else:
            x, ctx_new = _mlstm_mixer(x, ctx, gm, sh1, sc1, csh1, csc1, ml_w_in[j], ml_conv_w[j], ml_conv_b[j],
                                      ml_gate_b[j], ml_out_norm_g[j], ml_w_o[j], g1, cg1)
        if last:
            x, _ = _moe(x, None, norm_ffn_g[i], sh2, sc2, g2, None, None, None,
                        router_w[i], moe_w_gate, moe_w_up, moe_w_down, i, final_norm_g)
        else:
            x, ctx = _moe(x, ctx_new, norm_ffn_g[i], sh2, sc2, g2, csh2, csc2, cg2,
                          router_w[i], moe_w_gate, moe_w_up, moe_w_down, i)
    return x
```
